```python
import math
import jax, jax.numpy as jnp
from jax import lax
import numpy as np

D_MODEL = 2048
BATCH = 4
SEQ = 2048
DEPTH = 4
DEC_BATCH = 32
DEC_SEQ = 8
PAST_LEN = 16384
PAGE_SIZE = 128

N_MIXERS = 3
N_LAYERS_A = (DEPTH + 2) // 3
N_LAYERS_B = (DEPTH + 1) // 3
N_LAYERS_C = DEPTH // 3
RMS_EPS = 1e-6

MLSTM_HEADS = 8
MLSTM_DV = D_MODEL // MLSTM_HEADS
MLSTM_DK = MLSTM_DV // 2
MLSTM_CHUNK = 64
MLSTM_SIZES = (MLSTM_HEADS * MLSTM_DK, MLSTM_HEADS * MLSTM_DK, MLSTM_HEADS * MLSTM_DV,
               MLSTM_HEADS * MLSTM_DV, MLSTM_HEADS, MLSTM_HEADS)
MLSTM_IN = sum(MLSTM_SIZES)

SWA_HEADS = 32
SWA_KV_HEADS = 4
SWA_HEAD_DIM = D_MODEL // SWA_HEADS
SWA_GROUP = SWA_HEADS // SWA_KV_HEADS
WINDOW = 128
SWA_SIZES = (SWA_HEADS * SWA_HEAD_DIM, SWA_KV_HEADS * SWA_HEAD_DIM, SWA_KV_HEADS * SWA_HEAD_DIM)
SWA_IN = sum(SWA_SIZES)

SSD_D_INNER = 2 * D_MODEL
SSD_HEAD_DIM = 64
SSD_HEADS = SSD_D_INNER // SSD_HEAD_DIM
SSD_GROUPS = 8
SSD_HPG = SSD_HEADS // SSD_GROUPS
SSD_STATE = 128
SSD_CONV = 4
SSD_CHUNK = 128
SSD_CONV_DIM = SSD_D_INNER + 2 * SSD_GROUPS * SSD_STATE
SSD_IN = SSD_D_INNER + SSD_CONV_DIM + SSD_HEADS

D_FF = 5632
FFN_CONV = 3

kernel_name = "hybrid_mlstm_swa_ssd_convffn_step"


def _split(a, sizes):
    return jnp.split(a, [int(s) for s in np.cumsum(sizes)[:-1]], axis=-1)


def rmsnorm(x, g):
    xf = x.astype(jnp.float32)
    y = xf * lax.rsqrt(jnp.mean(xf * xf, -1, keepdims=True) + RMS_EPS)
    return (y * g.astype(jnp.float32)).astype(x.dtype)


def causal_dwconv(u, buf, w, b):
    K = w.shape[0]
    T = u.shape[1]
    full = jnp.concatenate([buf.astype(u.dtype), u], axis=1)
    out = full[:, 0:T] * w[0]
    for j in range(1, K):
        out = out + full[:, j:j + T] * w[j]
    return out + b, full[:, T:]


def _to_chunks(a, L):
    Bsz, T = a.shape[0], a.shape[1]
    a = a.astype(jnp.float32).reshape((Bsz, T // L, L) + a.shape[2:])
    return jnp.moveaxis(a, 1, 0)


def mlstm_chunked(q, k, v, ig, lf, c0, n0, m0):
    Bsz, T, H, _ = q.shape
    DV = v.shape[-1]
    L = math.gcd(T, MLSTM_CHUNK)
    causal = jnp.tril(jnp.ones((L, L), bool))

    def step(carry, inp):
        c, n, m = carry
        qc, kc, vc, ic, fc = inp
        b = jnp.moveaxis(jnp.cumsum(fc, axis=1), 1, 2)
        ic = jnp.moveaxis(ic, 1, 2)
        log_d = jnp.where(causal, b[..., :, None] - b[..., None, :] + ic[..., None, :], -jnp.inf)
        log_inter = b + m[..., None]
        m_t = jnp.maximum(log_inter, jnp.max(log_d, -1))
        d = jnp.exp(log_d - m_t[..., None])
        inter = jnp.exp(log_inter - m_t)
        s = jnp.einsum('blhk,bshk->bhls', qc, kc) * d
        num = inter[..., None] * jnp.einsum('blhk,bhkv->bhlv', qc, c) + jnp.einsum('bhls,bshv->bhlv', s, vc)
        den = inter * jnp.einsum('blhk,bhk->bhl', qc, n) + jnp.sum(s, -1)
        h = num / jnp.maximum(jnp.abs(den), jnp.exp(-m_t))[..., None]
        b_last = b[..., -1]
        log_w = b_last[..., None] - b + ic
        m_new = jnp.maximum(b_last + m, jnp.max(log_w, -1))
        w = jnp.exp(log_w - m_new[..., None])
        decay = jnp.exp(b_last + m - m_new)
        c_new = decay[..., None, None] * c + jnp.einsum('bhs,bshk,bshv->bhkv', w, kc, vc)
        n_new = decay[..., None] * n + jnp.einsum('bhs,bshk->bhk', w, kc)
        return (c_new, n_new, m_new), jnp.moveaxis(h, 2, 1)

    init = (c0.astype(jnp.float32), n0.astype(jnp.float32), m0.astype(jnp.float32))
    xs = (_to_chunks(q, L), _to_chunks(k, L), _to_chunks(v, L), _to_chunks(ig, L), _to_chunks(lf, L))
    (c, n, m), h = lax.scan(step, init, xs)
    h = jnp.moveaxis(h, 0, 1).reshape(Bsz, T, H, DV)
    return h, c, n, m


def mlstm_mixer(x, w_in, b_ig, b_fg, g_head, w_out, c0, n0, m0):
    Bsz, T, _ = x.shape
    H, DK, DV = MLSTM_HEADS, MLSTM_DK, MLSTM_DV
    q, k, v, o, ig, fg = _split(x @ w_in, MLSTM_SIZES)
    q = q.reshape(Bsz, T, H, DK) * (DK ** -0.5)
    k = k.reshape(Bsz, T, H, DK)
    v = v.reshape(Bsz, T, H, DV)
    ig = ig.astype(jnp.float32) + b_ig.astype(jnp.float32)
    lf = jax.nn.log_sigmoid(fg.astype(jnp.float32) + b_fg.astype(jnp.float32))
    h, c, n, m = mlstm_chunked(q, k, v, ig, lf, c0, n0, m0)
    hn = h * lax.rsqrt(jnp.mean(h * h, -1, keepdims=True) + RMS_EPS) * g_head.astype(jnp.float32).reshape(H, DV)
    out = (jax.nn.sigmoid(o.astype(jnp.float32)) * hn.reshape(Bsz, T, H * DV)).astype(x.dtype)
    dt = x.dtype
    return out @ w_out, c.astype(dt), n.astype(dt), m.astype(dt)


def alibi_slopes(n):
    return jnp.exp2(-8.0 * jnp.arange(1, n + 1, dtype=jnp.float32) / n)


def band_attention(q, k, v, dist, valid, sinks):
    f32 = jnp.float32
    s = jnp.einsum('bnqkgd,bnskd->bnkgqs', q.astype(f32), k.astype(f32)) * (SWA_HEAD_DIM ** -0.5)
    slopes = alibi_slopes(SWA_HEADS).reshape(SWA_KV_HEADS, SWA_GROUP)
    s = s - slopes[None, None, :, :, None, None] * dist[None, :, None, None]
    s = jnp.where(valid[None, :, None, None], s, -jnp.inf)
    sink = sinks.astype(f32).reshape(SWA_KV_HEADS, SWA_GROUP)[None, None, :, :, None]
    mx = jnp.maximum(jnp.max(s, -1), sink)
    p = jnp.exp(s - mx[..., None])
    p = p / (jnp.sum(p, -1) + jnp.exp(sink - mx))[..., None]
    return jnp.einsum('bnkgqs,bnskd->bnqkgd', p, v.astype(f32))


def swa_mixer(x, w_in, sinks, w_out, k_buf, v_buf):
    Bsz, T, _ = x.shape
    KVH, G, HD = SWA_KV_HEADS, SWA_GROUP, SWA_HEAD_DIM
    q, k, v = _split(x @ w_in, SWA_SIZES)
    q = q.reshape(Bsz, T, KVH, G, HD)
    k = k.reshape(Bsz, T, KVH, HD)
    v = v.reshape(Bsz, T, KVH, HD)
    if k_buf is None:
        W = WINDOW
        nb = T // W
        qb = q.reshape(Bsz, nb, W, KVH, G, HD)
        kb = k.reshape(Bsz, nb, W, KVH, HD)
        vb = v.reshape(Bsz, nb, W, KVH, HD)
        pad = ((0, 0), (1, 0), (0, 0), (0, 0), (0, 0))
        kk = jnp.concatenate([jnp.pad(kb, pad)[:, :-1], kb], axis=2)
        vv = jnp.concatenate([jnp.pad(vb, pad)[:, :-1], vb], axis=2)
        i = jnp.arange(W)[:, None]
        j = jnp.arange(2 * W)[None, :]
        dist = W + i - j
        kpos = (jnp.arange(nb)[:, None, None] - 1) * W + j[None]
        valid = (dist >= 0)[None] & (dist <= WINDOW)[None] & (kpos >= 0)
        dist = jnp.broadcast_to(dist, valid.shape).astype(jnp.float32)
        o = band_attention(qb, kk, vv, dist, valid, sinks)
        new_k, new_v = k[:, -WINDOW:], v[:, -WINDOW:]
    else:
        WB = k_buf.shape[1]
        kk = jnp.concatenate([k_buf.astype(k.dtype), k], axis=1)
        vv = jnp.concatenate([v_buf.astype(v.dtype), v], axis=1)
        dist = (WB + jnp.arange(T))[:, None] - jnp.arange(WB + T)[None, :]
        valid = ((dist >= 0) & (dist <= WINDOW))[None]
        o = band_attention(q[:, None], kk[:, None], vv[:, None], dist[None].astype(jnp.float32), valid, sinks)
        new_k, new_v = kk[:, -WB:], vv[:, -WB:]
    o = o.reshape(Bsz, T, SWA_HEADS * HD).astype(x.dtype)
    return o @ w_out, new_k, new_v


def ssd_chunked(x, dt, a, bm, cm, s0):
    Bsz, T = x.shape[0], x.shape[1]
    L = math.gcd(T, SSD_CHUNK)
    causal = jnp.tril(jnp.ones((L, L), bool))
    a = a.astype(jnp.float32)

    def step(s, inp):
        xc, dtc, bc, cc = inp
        acum = jnp.cumsum(dtc * a, axis=1)
        seg = acum[:, :, None] - acum[:, None, :]
        decay = jnp.exp(jnp.where(causal[None, :, :, None, None], seg, -jnp.inf))
        cb = jnp.einsum('btgn,bsgn->btsg', cc, bc)
        xdt = xc * dtc[..., None]
        y = jnp.einsum('btsg,btsgh,bsghp->btghp', cb, decay, xdt)
        y = y + jnp.einsum('btgn,bghpn->btghp', cc, s) * jnp.exp(acum)[..., None]
        a_last = acum[:, -1]
        w = jnp.exp(a_last[:, None] - acum)
        s_new = jnp.exp(a_last)[..., None, None] * s + jnp.einsum('bsgn,bsgh,bsghp->bghpn', bc, w, xdt)
        return s_new, y

    xs = (_to_chunks(x, L), _to_chunks(dt, L), _to_chunks(bm, L), _to_chunks(cm, L))
    s, y = lax.scan(step, s0.astype(jnp.float32), xs)
    y = jnp.moveaxis(y, 0, 1).reshape(x.shape)
    return y, s


def ssd_mixer(x, w_in, conv_w, conv_b, dt_bias, a_log, d_skip, g_norm, w_out, s0, conv_buf):
    Bsz, T, _ = x.shape
    G, Hg, P, N = SSD_GROUPS, SSD_HPG, SSD_HEAD_DIM, SSD_STATE
    z, xbc, dt = _split(x @ w_in, (SSD_D_INNER, SSD_CONV_DIM, SSD_HEADS))
    xbc, new_buf = causal_dwconv(xbc, conv_buf, conv_w, conv_b)
    xbc = jax.nn.silu(xbc)
    xs, bm, cm = _split(xbc, (SSD_D_INNER, G * N, G * N))
    xs = xs.reshape(Bsz, T, G, Hg, P)
    bm = bm.reshape(Bsz, T, G, N)
    cm = cm.reshape(Bsz, T, G, N)
    dt = jax.nn.softplus(dt.astype(jnp.float32) + dt_bias.astype(jnp.float32)).reshape(Bsz, T, G, Hg)
    a = -jnp.exp(a_log.astype(jnp.float32)).reshape(G, Hg)
    y, s = ssd_chunked(xs, dt, a, bm, cm, s0.reshape(Bsz, G, Hg, P, N))
    y = y + d_skip.astype(jnp.float32).reshape(G, Hg)[..., None] * xs.astype(jnp.float32)
    y = y.reshape(Bsz, T, SSD_D_INNER) * jax.nn.silu(z.astype(jnp.float32))
    y = rmsnorm(y, g_norm).astype(x.dtype)
    return y @ w_out, s.reshape(Bsz, SSD_HEADS, P, N).astype(x.dtype), new_buf


def conv_ffn(x, w_in, conv_w, conv_b, w_out, buf):
    u, g = jnp.split(x @ w_in, 2, axis=-1)
    g, new_buf = causal_dwconv(g, buf, conv_w, conv_b)
    return (jax.nn.silu(g) * u) @ w_out, new_buf


def trunk(x, st_c, st_n, st_m, st_k, st_v, st_ssm, st_conv, st_ffn,
          norm_mix, norm_ffn, norm_final, w_in_a, b_ig_a, b_fg_a, g_head_a, w_out_a,
          w_in_b, sinks_b, w_out_b, w_in_c, conv_w_c, conv_b_c, dt_bias_c, a_log_c, d_skip_c,
          g_norm_c, w_out_c, w_ffn_in, ffn_conv_w, ffn_conv_b, w_ffn_out):
    Bsz = x.shape[0]
    fresh = st_c is None
    dt = x.dtype
    o_c, o_n, o_m, o_k, o_v, o_s, o_cv, o_f = [], [], [], [], [], [], [], []
    for i in range(DEPTH):
        j = i // N_MIXERS
        kind = i % N_MIXERS
        h = rmsnorm(x, norm_mix[i])
        if kind == 0:
            if fresh:
                c0 = jnp.zeros((Bsz, MLSTM_HEADS, MLSTM_DK, MLSTM_DV), dt)
                n0 = jnp.zeros((Bsz, MLSTM_HEADS, MLSTM_DK), dt)
                m0 = jnp.zeros((Bsz, MLSTM_HEADS), dt)
            else:
                c0, n0, m0 = st_c[j], st_n[j], st_m[j]
            y, c, n, m = mlstm_mixer(h, w_in_a[j], b_ig_a[j], b_fg_a[j], g_head_a[j], w_out_a[j], c0, n0, m0)
            o_c.append(c); o_n.append(n); o_m.append(m)
        elif kind == 1:
            kb0 = None if fresh else st_k[j]
            vb0 = None if fresh else st_v[j]
            y, kb, vb = swa_mixer(h, w_in_b[j], sinks_b[j], w_out_b[j], kb0, vb0)
            o_k.append(kb); o_v.append(vb)
        else:
            if fresh:
                s0 = jnp.zeros((Bsz, SSD_HEADS, SSD_HEAD_DIM, SSD_STATE), dt)
                cb0 = jnp.zeros((Bsz, SSD_CONV - 1, SSD_CONV_DIM), dt)
            else:
                s0, cb0 = st_ssm[j], st_conv[j]
            y, s, cb = ssd_mixer(h, w_in_c[j], conv_w_c[j], conv_b_c[j], dt_bias_c[j], a_log_c[j],
                                 d_skip_c[j], g_norm_c[j], w_out_c[j], s0, cb0)
            o_s.append(s); o_cv.append(cb)
        x = x + y
        h = rmsnorm(x, norm_ffn[i])
        fb0 = jnp.zeros((Bsz, FFN_CONV - 1, D_FF), dt) if fresh else st_ffn[i]
        y, fb = conv_ffn(h, w_ffn_in[i], ffn_conv_w[i], ffn_conv_b[i], w_ffn_out[i], fb0)
        o_f.append(fb)
        x = x + y
    return (rmsnorm(x, norm_final), jnp.stack(o_c), jnp.stack(o_n), jnp.stack(o_m), jnp.stack(o_k),
            jnp.stack(o_v), jnp.stack(o_s), jnp.stack(o_cv), jnp.stack(o_f))


def setup_inputs(seed: int = 0) -> dict:
    key = jax.random.key(seed)
    ks = iter(jax.random.split(key, 64))
    nrm = lambda shape, scale: jax.random.normal(next(ks), shape, jnp.float32) * scale
    gain = lambda shape: 1.0 + nrm(shape, 0.05)
    WB = min(WINDOW, PAST_LEN)
    dt0 = jnp.exp(jax.random.uniform(next(ks), (N_LAYERS_C, SSD_HEADS), jnp.float32,
                                     math.log(1e-3), math.log(1e-1)))
    return {
        "x_prompt": nrm((BATCH, SEQ, D_MODEL), 1.0),
        "x_sample": nrm((DEC_BATCH, DEC_SEQ, D_MODEL), 1.0),
        "state_mlstm_c": nrm((N_LAYERS_A, DEC_BATCH, MLSTM_HEADS, MLSTM_DK, MLSTM_DV), 0.5),
        "state_mlstm_n": nrm((N_LAYERS_A, DEC_BATCH, MLSTM_HEADS, MLSTM_DK), 0.5),
        "state_mlstm_m": nrm((N_LAYERS_A, DEC_BATCH, MLSTM_HEADS), 1.0),
        "cache_swa_k": nrm((N_LAYERS_B, DEC_BATCH, WB, SWA_KV_HEADS, SWA_HEAD_DIM), 1.0),
        "cache_swa_v": nrm((N_LAYERS_B, DEC_BATCH, WB, SWA_KV_HEADS, SWA_HEAD_DIM), 1.0),
        "state_ssm": nrm((N_LAYERS_C, DEC_BATCH, SSD_HEADS, SSD_HEAD_DIM, SSD_STATE), 0.1),
        "state_ssm_conv": nrm((N_LAYERS_C, DEC_BATCH, SSD_CONV - 1, SSD_CONV_DIM), 1.0),
        "state_ffn_conv": nrm((DEPTH, DEC_BATCH, FFN_CONV - 1, D_FF), 1.0),
        "norm_mix": gain((DEPTH, D_MODEL)),
        "norm_ffn": gain((DEPTH, D_MODEL)),
        "norm_final": gain((D_MODEL,)),
        "w_in_a": nrm((N_LAYERS_A, D_MODEL, MLSTM_IN), D_MODEL ** -0.5),
        "b_ig_a": nrm((N_LAYERS_A, MLSTM_HEADS), 0.1),
        "b_fg_a": 3.0 + jax.random.uniform(next(ks), (N_LAYERS_A, MLSTM_HEADS), jnp.float32, 0.0, 3.0),
        "g_head_a": gain((N_LAYERS_A, MLSTM_HEADS * MLSTM_DV)),
        "w_out_a": nrm((N_LAYERS_A, MLSTM_HEADS * MLSTM_DV, D_MODEL), (MLSTM_HEADS * MLSTM_DV) ** -0.5),
        "w_in_b": nrm((N_LAYERS_B, D_MODEL, SWA_IN), D_MODEL ** -0.5),
        "sinks_b": nrm((N_LAYERS_B, SWA_HEADS), 0.5),
        "w_out_b": nrm((N_LAYERS_B, SWA_HEADS * SWA_HEAD_DIM, D_MODEL), (SWA_HEADS * SWA_HEAD_DIM) ** -0.5),
        "w_in_c": nrm((N_LAYERS_C, D_MODEL, SSD_IN), D_MODEL ** -0.5),
        "conv_w_c": nrm((N_LAYERS_C, SSD_CONV, SSD_CONV_DIM), SSD_CONV ** -0.5),
        "conv_b_c": nrm((N_LAYERS_C, SSD_CONV_DIM), 0.02),
        "dt_bias_c": dt0 + jnp.log(-jnp.expm1(-dt0)),
        "a_log_c": jnp.log(jax.random.uniform(next(ks), (N_LAYERS_C, SSD_HEADS), jnp.float32, 1.0, 16.0)),
        "d_skip_c": 1.0 + nrm((N_LAYERS_C, SSD_HEADS), 0.1),
        "g_norm_c": gain((N_LAYERS_C, SSD_D_INNER)),
        "w_out_c": nrm((N_LAYERS_C, SSD_D_INNER, D_MODEL), SSD_D_INNER ** -0.5),
        "w_ffn_in": nrm((DEPTH, D_MODEL, 2 * D_FF), D_MODEL ** -0.5),
        "ffn_conv_w": nrm((DEPTH, FFN_CONV, D_FF), FFN_CONV ** -0.5),
        "ffn_conv_b": nrm((DEPTH, D_FF), 0.02),
        "w_ffn_out": nrm((DEPTH, D_FF, D_MODEL), D_FF ** -0.5),
    }


def reference(x_prompt, x_sample, state_mlstm_c, state_mlstm_n, state_mlstm_m, cache_swa_k, cache_swa_v,
              state_ssm, state_ssm_conv, state_ffn_conv, norm_mix, norm_ffn, norm_final,
              w_in_a, b_ig_a, b_fg_a, g_head_a, w_out_a, w_in_b, sinks_b, w_out_b,
              w_in_c, conv_w_c, conv_b_c, dt_bias_c, a_log_c, d_skip_c, g_norm_c, w_out_c,
              w_ffn_in, ffn_conv_w, ffn_conv_b, w_ffn_out):
    weights = (norm_mix, norm_ffn, norm_final, w_in_a, b_ig_a, b_fg_a, g_head_a, w_out_a,
               w_in_b, sinks_b, w_out_b, w_in_c, conv_w_c, conv_b_c, dt_bias_c, a_log_c, d_skip_c,
               g_norm_c, w_out_c, w_ffn_in, ffn_conv_w, ffn_conv_b, w_ffn_out)
    y_prompt, pc, pn, pm, pk, pv, ps, pcv, pf = trunk(x_prompt, None, None, None, None, None, None, None, None,
                                                     *weights)
    y_sample, sc, sn, sm, sk, sv, ss, scv, sf = trunk(x_sample, state_mlstm_c, state_mlstm_n, state_mlstm_m,
                                                     cache_swa_k, cache_swa_v, state_ssm, state_ssm_conv,
                                                     state_ffn_conv, *weights)
    return (y_prompt, y_sample, pc, sc, pn, sn, pm, sm, pk, sk, pv, sv, ps, ss, pcv, scv, pf, sf)
```

```python
import functools
import math

import jax
import jax.numpy as jnp
from jax import lax
from jax.experimental import pallas as pl
from jax.experimental.pallas import tpu as pltpu

F32 = jnp.float32
BF16 = jnp.bfloat16

RMS_EPS = 1e-6
N_MIXERS = 3

MLSTM_HEADS = 8
SWA_HEADS = 32
SWA_KV_HEADS = 4
SWA_GROUP = SWA_HEADS // SWA_KV_HEADS
WINDOW = 128
SSD_HEAD_DIM = 64
SSD_GROUPS = 8
SSD_STATE = 128
SSD_CONV = 4
FFN_CONV = 3

LANES = 128
SUBLANES = 8
VMEM_LIMIT_BYTES = 56 << 20
VMEM_TILE_BUDGET = 40 << 20

MLSTM_CHUNK = 128
SSD_CHUNK = 128


def _cparams(n_axes):
    return pltpu.CompilerParams(dimension_semantics=("arbitrary",) * n_axes,
                                vmem_limit_bytes=VMEM_LIMIT_BYTES)


def _round_up(a, b):
    return (a + b - 1) // b * b


def _largest_divisor(n, candidates):
    for c in candidates:
        if n % c == 0:
            return c
    raise ValueError(f"no tile in {candidates} divides {n}")


def _rmsnorm(x, g):
    return x * lax.rsqrt(jnp.mean(x * x, axis=-1, keepdims=True) + RMS_EPS) * g


def _silu(x):
    return x * jax.nn.sigmoid(x)


def _mm_body(*refs, norm, residual):
    it = iter(refs)
    x_ref = next(it)
    g_ref = next(it) if norm else None
    w_ref = next(it)
    r_ref = next(it) if residual else None
    o_ref = next(it)
    xn_ref = next(it) if norm else None
    if norm:
        @pl.when(pl.program_id(1) == 0)
        def _():
            xn_ref[...] = _rmsnorm(x_ref[...], g_ref[...]).astype(BF16)
        lhs = xn_ref[...]
    else:
        lhs = x_ref[...].astype(BF16)
    acc = jnp.dot(lhs, w_ref[...], preferred_element_type=F32)
    if residual:
        acc = acc + r_ref[...]
    o_ref[...] = acc.astype(o_ref.dtype)


def _mm_tiles(m, k, n, x_bytes, norm, residual):
    tn = _largest_divisor(n, (1024, 896, 512, 256, 128))
    for tm in (1024, 512, 256, 128, 64, 32, 16, 8):
        if m % tm:
            continue
        need = 2 * tm * k * x_bytes + 2 * k * tn * 2 + 2 * tm * tn * 4
        need += tm * k * 2 if norm else 0
        need += 2 * tm * tn * 4 if residual else 0
        if need <= VMEM_TILE_BUDGET:
            return tm, tn
    raise ValueError("no matmul tile fits VMEM")


def _matmul(x, w, gain=None, res=None):
    m, k = x.shape
    n = w.shape[1]
    norm, residual = gain is not None, res is not None
    tm, tn = _mm_tiles(m, k, n, x.dtype.itemsize, norm, residual)
    in_specs = [pl.BlockSpec((tm, k), lambda i, j: (i, 0))]
    args = [x]
    if norm:
        in_specs.append(pl.BlockSpec((1, k), lambda i, j: (0, 0)))
        args.append(gain.reshape(1, k))
    in_specs.append(pl.BlockSpec((k, tn), lambda i, j: (0, j)))
    args.append(w)
    if residual:
        in_specs.append(pl.BlockSpec((tm, tn), lambda i, j: (i, j)))
        args.append(res)
    return pl.pallas_call(
        functools.partial(_mm_body, norm=norm, residual=residual),
        grid=(m // tm, n // tn),
        in_specs=in_specs,
        out_specs=pl.BlockSpec((tm, tn), lambda i, j: (i, j)),
        out_shape=jax.ShapeDtypeStruct((m, n), F32),
        scratch_shapes=[pltpu.VMEM((tm, k), BF16)] if norm else [],
        compiler_params=_cparams(2),
        name="matmul",
    )(*args)


def _ffn_body(x_ref, gn_ref, wu_ref, wg_ref, cw_ref, cb_ref, wo_ref, buf0_ref, gf_ref,
              o_ref, bufo_ref, xn_ref, gs_ref, carry_ref,
              *, tm, shift, halo, tiles_per_seq, nj, final_norm):
    i = pl.program_id(0)
    j = pl.program_id(1)

    @pl.when(j == 0)
    def _():
        xn_ref[...] = _rmsnorm(x_ref[...], gn_ref[...]).astype(BF16)

    xn = xn_ref[...]
    u = jnp.dot(xn, wu_ref[...], preferred_element_type=F32)
    g = jnp.dot(xn, wg_ref[...], preferred_element_type=F32)

    first = (i % tiles_per_seq) == 0

    @pl.when(first)
    def _():
        gs_ref[0:halo, :] = buf0_ref[0]

    @pl.when(jnp.logical_not(first))
    def _():
        gs_ref[0:halo, :] = carry_ref[j]

    gs_ref[halo:halo + tm, :] = g
    tail = gs_ref[tm:tm + halo, :]
    carry_ref[j] = tail
    bufo_ref[0] = tail

    base = halo - (FFN_CONV - 1) * shift
    cw = cw_ref[...]
    c = gs_ref[base:base + tm, :] * cw[0:1, :]
    for t in range(1, FFN_CONV):
        c = c + gs_ref[base + t * shift:base + t * shift + tm, :] * cw[t:t + 1, :]
    c = c + cb_ref[...]
    h = (_silu(c) * u).astype(BF16)
    contrib = jnp.dot(h, wo_ref[...], preferred_element_type=F32)

    @pl.when(j == 0)
    def _():
        o_ref[...] = x_ref[...] + contrib

    @pl.when(j > 0)
    def _():
        o_ref[...] += contrib

    if final_norm:
        @pl.when(j == nj - 1)
        def _():
            o_ref[...] = _rmsnorm(o_ref[...], gf_ref[...])


def _ffn(x, gain, w_in, conv_w, conv_b, w_out, buf0, shift, rows_per_seq, final_gain):
    m, d = x.shape
    d_ff = w_out.shape[0]
    n_seq = m // rows_per_seq
    tm = _largest_divisor(rows_per_seq, (512, 256, 128, 64, 32, 16, 8))
    tf = _largest_divisor(d_ff, (512, 256, 128))
    nj = d_ff // tf
    tiles_per_seq = rows_per_seq // tm
    pre = (FFN_CONV - 1) * shift
    halo = _round_up(pre, SUBLANES)
    assert tm >= halo and tm % SUBLANES == 0
    buf0p = jnp.pad(buf0, ((0, 0), (halo - pre, 0), (0, 0)))
    final_norm = final_gain is not None
    gf = (final_gain if final_norm else gain).reshape(1, d)
    body = functools.partial(_ffn_body, tm=tm, shift=shift, halo=halo, tiles_per_seq=tiles_per_seq,
                             nj=nj, final_norm=final_norm)
    out, bufo = pl.pallas_call(
        body,
        grid=(m // tm, nj),
        in_specs=[
            pl.BlockSpec((tm, d), lambda i, j: (i, 0)),
            pl.BlockSpec((1, d), lambda i, j: (0, 0)),
            pl.BlockSpec((d, tf), lambda i, j: (0, j)),
            pl.BlockSpec((d, tf), lambda i, j: (0, nj + j)),
            pl.BlockSpec((FFN_CONV, tf), lambda i, j: (0, j)),
            pl.BlockSpec((1, tf), lambda i, j: (0, j)),
            pl.BlockSpec((tf, d), lambda i, j: (j, 0)),
            pl.BlockSpec((1, halo, tf), lambda i, j: (i // tiles_per_seq, 0, j)),
            pl.BlockSpec((1, d), lambda i, j: (0, 0)),
        ],
        out_specs=[
            pl.BlockSpec((tm, d), lambda i, j: (i, 0)),
            pl.BlockSpec((1, halo, tf), lambda i, j: (i // tiles_per_seq, 0, j)),
        ],
        out_shape=[jax.ShapeDtypeStruct((m, d), F32),
                   jax.ShapeDtypeStruct((n_seq, halo, d_ff), F32)],
        scratch_shapes=[pltpu.VMEM((tm, d), BF16),
                        pltpu.VMEM((halo + tm, tf), F32),
                        pltpu.VMEM((nj, halo, tf), F32)],
        compiler_params=_cparams(2),
        name="conv_ffn",
    )(x, gain.reshape(1, d), w_in, w_in, conv_w, conv_b.reshape(1, d_ff), w_out, buf0p, gf)
    return out, bufo[:, halo - pre:, :]


def _row_from_col(col, eye):
    return jnp.sum(jnp.where(eye, col, 0.0), axis=0, keepdims=True)


def _cumsum_rows(x, tril):
    return jnp.dot(tril, x, preferred_element_type=F32, precision=lax.Precision.HIGHEST)


def _mlstm_body(q_ref, k_ref, v_ref, o_ref, gate_ref, bias_ref, gh_ref, c0_ref, n0_ref, m0_ref,
                hg_ref, c_ref, n_ref, m_ref, *, chunk, dk, dv):
    L = chunk
    H = MLSTM_HEADS

    @pl.when(pl.program_id(1) == 0)
    def _():
        c_ref[...] = c0_ref[...]
        n_ref[...] = n0_ref[...]
        m_ref[...] = m0_ref[...]

    ri = lax.broadcasted_iota(jnp.int32, (L, L), 0)
    ci = lax.broadcasted_iota(jnp.int32, (L, L), 1)
    causal = ri >= ci
    eye = ri == ci
    tril = causal.astype(F32)

    gates = gate_ref[...] + bias_ref[...]
    x = gates
    log_sig = jnp.minimum(x, 0.0) - jnp.log1p(jnp.exp(-jnp.abs(x)))
    bcum = _cumsum_rows(log_sig, tril)

    for h in range(H):
        ig_col = gates[:, h:h + 1]
        b_col = bcum[:, H + h:H + h + 1]
        ig_row = _row_from_col(ig_col, eye)
        b_row = _row_from_col(b_col, eye)
        m_prev = m_ref[0, h:h + 1, 0:1]

        log_d = jnp.where(causal, b_col - b_row + ig_row, -jnp.inf)
        log_inter = b_col + m_prev
        m_t = jnp.maximum(log_inter, jnp.max(log_d, axis=-1, keepdims=True))
        d = jnp.exp(log_d - m_t)
        inter = jnp.exp(log_inter - m_t)

        qf = q_ref[:, h * dk:(h + 1) * dk] * (dk ** -0.5)
        kf = k_ref[:, h * dk:(h + 1) * dk]
        qb = qf.astype(BF16)
        kb = kf.astype(BF16)
        vb = v_ref[:, h * dv:(h + 1) * dv].astype(BF16)
        c_prev = c_ref[0, h]
        n_prev = n_ref[0, h:h + 1, :]

        s = lax.dot_general(qb, kb, (((1,), (1,)), ((), ())), preferred_element_type=F32) * d
        num = inter * jnp.dot(qb, c_prev.astype(BF16), preferred_element_type=F32)
        num = num + jnp.dot(s.astype(BF16), vb, preferred_element_type=F32)
        den = inter * jnp.sum(qf * n_prev, axis=-1, keepdims=True) + jnp.sum(s, axis=-1, keepdims=True)
        hh = num / jnp.maximum(jnp.abs(den), jnp.exp(-m_t))

        hn = hh * lax.rsqrt(jnp.mean(hh * hh, axis=-1, keepdims=True) + RMS_EPS)
        hn = hn * gh_ref[:, h * dv:(h + 1) * dv]
        out = jax.nn.sigmoid(o_ref[:, h * dv:(h + 1) * dv]) * hn
        hg_ref[:, h * dv:(h + 1) * dv] = out.astype(hg_ref.dtype)

        b_last = b_col[L - 1:L, :]
        log_w = b_last - b_col + ig_col
        m_new = jnp.maximum(b_last + m_prev, jnp.max(log_w, axis=0, keepdims=True))
        w = jnp.exp(log_w - m_new)
        decay = jnp.exp(b_last + m_prev - m_new)
        kw = kf * w
        upd = lax.dot_general(kw.astype(BF16), vb, (((0,), (0,)), ((), ())), preferred_element_type=F32)
        c_ref[0, h] = decay * c_prev + upd
        n_ref[0, h:h + 1, :] = decay * n_prev + jnp.sum(kw, axis=0, keepdims=True)
        m_ref[0, h:h + 1, :] = jnp.broadcast_to(m_new, (1, LANES))


def _mlstm_mix(proj, bias, g_head, c0, n0, m0, B, T):
    H = MLSTM_HEADS
    dk, dv = c0.shape[2], c0.shape[3]
    L = math.gcd(T, MLSTM_CHUNK)
    nc = T // L
    qw, vw = H * dk, H * dv
    assert qw * 2 == vw and vw % LANES == 0
    gate_blk = (2 * qw + 2 * vw) // LANES
    hg_dtype = BF16 if L % 16 == 0 else F32
    m0b = jnp.broadcast_to(m0[:, :, None], (B, H, LANES))
    row = lambda b, c: b * nc + c
    hg, c, n, m = pl.pallas_call(
        functools.partial(_mlstm_body, chunk=L, dk=dk, dv=dv),
        grid=(B, nc),
        in_specs=[
            pl.BlockSpec((L, qw), lambda b, c: (row(b, c), 0)),
            pl.BlockSpec((L, qw), lambda b, c: (row(b, c), 1)),
            pl.BlockSpec((L, vw), lambda b, c: (row(b, c), 1)),
            pl.BlockSpec((L, vw), lambda b, c: (row(b, c), 2)),
            pl.BlockSpec((L, LANES), lambda b, c: (row(b, c), gate_blk)),
            pl.BlockSpec((1, LANES), lambda b, c: (0, 0)),
            pl.BlockSpec((1, vw), lambda b, c: (0, 0)),
            pl.BlockSpec((1, H, dk, dv), lambda b, c: (b, 0, 0, 0)),
            pl.BlockSpec((1, H, dk), lambda b, c: (b, 0, 0)),
            pl.BlockSpec((1, H, LANES), lambda b, c: (b, 0, 0)),
        ],
        out_specs=[
            pl.BlockSpec((L, vw), lambda b, c: (row(b, c), 0)),
            pl.BlockSpec((1, H, dk, dv), lambda b, c: (b, 0, 0, 0)),
            pl.BlockSpec((1, H, dk), lambda b, c: (b, 0, 0)),
            pl.BlockSpec((1, H, LANES), lambda b, c: (b, 0, 0)),
        ],
        out_shape=[jax.ShapeDtypeStruct((B * T, vw), hg_dtype),
                   jax.ShapeDtypeStruct((B, H, dk, dv), F32),
                   jax.ShapeDtypeStruct((B, H, dk), F32),
                   jax.ShapeDtypeStruct((B, H, LANES), F32)],
        compiler_params=_cparams(2),
        name="mlstm",
    )(proj, proj, proj, proj, proj, bias, g_head.reshape(1, vw), c0, n0, m0b)
    return hg, c, n, m[:, :, 0]


def _swa_body(sink_ref, q_ref, kp_ref, kc_ref, vp_ref, vc_ref, o_ref, *, tq, hd, mask_first):
    W = WINDOW
    G = SWA_GROUP
    S = 2 * W
    kk = jnp.concatenate([kp_ref[...], kc_ref[...]], axis=0)
    vv = jnp.concatenate([vp_ref[...], vc_ref[...]], axis=0)
    qi = lax.broadcasted_iota(jnp.int32, (tq, S), 0)
    kj = lax.broadcasted_iota(jnp.int32, (tq, S), 1)
    dist = W + qi - kj
    valid = (dist >= 0) & (dist <= W)
    if mask_first:
        valid = valid & ((kj >= W) | (pl.program_id(1) > 0))
    distf = dist.astype(F32)
    lane_s = lax.broadcasted_iota(jnp.int32, (S, LANES), 1)
    lane_q = lax.broadcasted_iota(jnp.int32, (tq, LANES), 1)
    low_s = lane_s < hd
    low_q = lane_q < hd
    heads_per_slab = LANES // hd
    assert heads_per_slab == 2

    for kh in range(SWA_KV_HEADS):
        slab = (kh // heads_per_slab) * LANES
        keep_low = (kh % heads_per_slab) == 0
        k128 = kk[:, slab:slab + LANES]
        v128 = vv[:, slab:slab + LANES]
        k_rot = pltpu.roll(k128, hd, axis=1)
        v_rot = pltpu.roll(v128, hd, axis=1)
        own = low_s if keep_low else jnp.logical_not(low_s)
        k2 = jnp.where(own, k128, k_rot).astype(BF16)
        v2 = jnp.where(own, v128, v_rot).astype(BF16)

        q_parts = []
        for g in range(G):
            h = kh * G + g
            qs = q_ref[:, (h // 2) * LANES:(h // 2 + 1) * LANES]
            keep = low_q if h % 2 == 0 else jnp.logical_not(low_q)
            q_parts.append(jnp.where(keep, qs, 0.0))
        q_stack = jnp.concatenate(q_parts, axis=0).astype(BF16)
        s_all = lax.dot_general(q_stack, k2, (((1,), (1,)), ((), ())), preferred_element_type=F32)
        s_all = s_all * (hd ** -0.5)

        p_parts = []
        for g in range(G):
            h = kh * G + g
            slope = 2.0 ** (-8.0 * (h + 1) / SWA_HEADS)
            sink = sink_ref[h]
            s = s_all[g * tq:(g + 1) * tq, :] - slope * distf
            s = jnp.where(valid, s, -jnp.inf)
            mx = jnp.maximum(jnp.max(s, axis=-1, keepdims=True), sink)
            p = jnp.exp(s - mx)
            p = p / (jnp.sum(p, axis=-1, keepdims=True) + jnp.exp(sink - mx))
            p_parts.append(p)
        p_stack = jnp.concatenate(p_parts, axis=0).astype(BF16)
        o_all = jnp.dot(p_stack, v2, preferred_element_type=F32)

        for g2 in range(G // 2):
            h = kh * G + 2 * g2
            o_even = o_all[(2 * g2) * tq:(2 * g2 + 1) * tq, :]
            o_odd = o_all[(2 * g2 + 1) * tq:(2 * g2 + 2) * tq, :]
            o_ref[:, (h // 2) * LANES:(h // 2 + 1) * LANES] = jnp.where(low_q, o_even, o_odd).astype(o_ref.dtype)


def _swa_call(sinks, q_arr, q_blk, k_arr, kp_map, kc_map, v_arr, vp_map, vc_map, grid, tq, n_rows, hd,
              mask_first):
    kvw = SWA_KV_HEADS * hd
    qw = SWA_HEADS * hd
    out_dtype = BF16 if tq % 16 == 0 else F32
    return pl.pallas_call(
        functools.partial(_swa_body, tq=tq, hd=hd, mask_first=mask_first),
        grid=grid,
        in_specs=[
            pl.BlockSpec(memory_space=pltpu.SMEM),
            pl.BlockSpec((tq, qw), q_blk),
            pl.BlockSpec((WINDOW, kvw), kp_map),
            pl.BlockSpec((WINDOW, kvw), kc_map),
            pl.BlockSpec((WINDOW, kvw), vp_map),
            pl.BlockSpec((WINDOW, kvw), vc_map),
        ],
        out_specs=pl.BlockSpec((tq, qw), q_blk),
        out_shape=jax.ShapeDtypeStruct((n_rows, qw), out_dtype),
        compiler_params=_cparams(2),
        name="swa",
    )(sinks, q_arr, k_arr, k_arr, v_arr, v_arr)


def _conv_silu(u_ref, init_ref, carry_ref, gs_ref, cw_ref, cb_ref, first, L):
    halo = SUBLANES

    @pl.when(first)
    def _():
        gs_ref[0:halo, :] = init_ref[0]

    @pl.when(jnp.logical_not(first))
    def _():
        gs_ref[0:halo, :] = carry_ref[...]

    gs_ref[halo:halo + L, :] = u_ref[...]
    carry_ref[...] = gs_ref[L:L + halo, :]
    base = halo - (SSD_CONV - 1)
    cw = cw_ref[...]
    c = gs_ref[base:base + L, :] * cw[0:1, :]
    for t in range(1, SSD_CONV):
        c = c + gs_ref[base + t:base + t + L, :] * cw[t:t + 1, :]
    c = c + cb_ref[...]
    return _silu(c)


def _ssd_body(dsk_ref, z_ref, xs_ref, bm_ref, cm_ref, dt_ref, dtb_ref, alog_ref,
              cwx_ref, cwb_ref, cwc_ref, cbx_ref, cbb_ref, cbc_ref,
              ix_ref, ib_ref, ic_ref, s0_ref,
              y_ref, s_ref,
              gx_ref, gb_ref, gc_ref, kx_ref, kb_ref, kc_ref, *, chunk, hpg, p_dim):
    L = chunk
    first = pl.program_id(2) == 0
    grp = pl.program_id(1)

    @pl.when(first)
    def _():
        s_ref[...] = s0_ref[...]

    xs = _conv_silu(xs_ref, ix_ref, kx_ref, gx_ref, cwx_ref, cbx_ref, first, L)
    bm = _conv_silu(bm_ref, ib_ref, kb_ref, gb_ref, cwb_ref, cbb_ref, first, L)
    cm = _conv_silu(cm_ref, ic_ref, kc_ref, gc_ref, cwc_ref, cbc_ref, first, L)
    bmb = bm.astype(BF16)
    cmb = cm.astype(BF16)

    ri = lax.broadcasted_iota(jnp.int32, (L, L), 0)
    ci = lax.broadcasted_iota(jnp.int32, (L, L), 1)
    causal = ri >= ci
    eye = ri == ci
    tril = causal.astype(F32)

    dt = jax.nn.softplus(dt_ref[...] + dtb_ref[...])
    a = -jnp.exp(alog_ref[...])
    acum = _cumsum_rows(dt * a, tril)
    a_last = acum[L - 1:L, :]
    eacum = jnp.exp(acum)
    dtw = dt * jnp.exp(a_last - acum)
    ea_last = jnp.exp(a_last)

    cb = lax.dot_general(cmb, bmb, (((1,), (1,)), ((), ())), preferred_element_type=F32)

    lane = lax.broadcasted_iota(jnp.int32, (L, LANES), 1)
    low = lane < p_dim
    srow = lax.broadcasted_iota(jnp.int32, (LANES, 1), 0)
    heads_per_slab = LANES // p_dim
    assert heads_per_slab == 2

    def expand(arr, h0):
        c0 = jnp.broadcast_to(arr[:, h0:h0 + 1], (L, LANES))
        c1 = jnp.broadcast_to(arr[:, h0 + 1:h0 + 2], (L, LANES))
        return jnp.where(low, c0, c1)

    for pr in range(hpg // heads_per_slab):
        h0 = 2 * pr
        ms = []
        for hh in (h0, h0 + 1):
            a_col = acum[:, hh:hh + 1]
            a_row = _row_from_col(a_col, eye)
            dec = jnp.exp(jnp.where(causal, a_col - a_row, -jnp.inf))
            ms.append((cb * dec).astype(BF16))
        mcat = jnp.concatenate(ms, axis=1)

        xsp = xs[:, pr * LANES:(pr + 1) * LANES]
        xb = (xsp * expand(dt, h0)).astype(BF16)
        zero = jnp.zeros_like(xb)
        bd = jnp.concatenate([jnp.where(low, xb, zero), jnp.where(low, zero, xb)], axis=0)
        y = jnp.dot(mcat, bd, preferred_element_type=F32)

        sp = s_ref[0, pr * LANES:(pr + 1) * LANES, :]
        y_inter = lax.dot_general(cmb, sp.astype(BF16), (((1,), (1,)), ((), ())), preferred_element_type=F32)
        y = y + y_inter * expand(eacum, h0)

        d0 = dsk_ref[grp * hpg + h0]
        d1 = dsk_ref[grp * hpg + h0 + 1]
        y = y + jnp.where(low, d0, d1) * xsp
        y = y * _silu(z_ref[:, pr * LANES:(pr + 1) * LANES])
        y_ref[:, pr * LANES:(pr + 1) * LANES] = y

        xw = (xsp * expand(dtw, h0)).astype(BF16)
        upd = lax.dot_general(xw, bmb, (((0,), (0,)), ((), ())), preferred_element_type=F32)
        scale = jnp.where(srow < p_dim, ea_last[:, h0:h0 + 1], ea_last[:, h0 + 1:h0 + 2])
        s_ref[0, pr * LANES:(pr + 1) * LANES, :] = sp * scale + upd


def _ssd_mix(proj, d_inner, conv_w, conv_b, dt_bias, a_log, d_skip, s0, conv0, B, T):
    G, P, N = SSD_GROUPS, SSD_HEAD_DIM, SSD_STATE
    heads = d_inner // P
    hpg = heads // G
    gw = hpg * P
    L = math.gcd(T, SSD_CHUNK)
    nc = T // L
    conv_dim = d_inner + 2 * G * N
    xs_blk0 = d_inner // gw
    b_blk0 = (2 * d_inner) // N
    c_blk0 = b_blk0 + G
    dt_blk0 = (d_inner + conv_dim) // LANES
    cw_b0 = d_inner // N
    pad_heads = lambda v: jnp.pad(v.reshape(G, hpg), ((0, 0), (0, LANES - hpg))).reshape(1, G * LANES)
    halo = SUBLANES
    conv0p = jnp.pad(conv0, ((0, 0), (halo - (SSD_CONV - 1), 0), (0, 0)))
    row = lambda b, g, c: b * nc + c
    y, s = pl.pallas_call(
        functools.partial(_ssd_body, chunk=L, hpg=hpg, p_dim=P),
        grid=(B, G, nc),
        in_specs=[
            pl.BlockSpec(memory_space=pltpu.SMEM),
            pl.BlockSpec((L, gw), lambda b, g, c: (row(b, g, c), g)),
            pl.BlockSpec((L, gw), lambda b, g, c: (row(b, g, c), xs_blk0 + g)),
            pl.BlockSpec((L, N), lambda b, g, c: (row(b, g, c), b_blk0 + g)),
            pl.BlockSpec((L, N), lambda b, g, c: (row(b, g, c), c_blk0 + g)),
            pl.BlockSpec((L, LANES), lambda b, g, c: (row(b, g, c), dt_blk0 + g)),
            pl.BlockSpec((1, LANES), lambda b, g, c: (0, g)),
            pl.BlockSpec((1, LANES), lambda b, g, c: (0, g)),
            pl.BlockSpec((SSD_CONV, gw), lambda b, g, c: (0, g)),
            pl.BlockSpec((SSD_CONV, N), lambda b, g, c: (0, cw_b0 + g)),
            pl.BlockSpec((SSD_CONV, N), lambda b, g, c: (0, cw_b0 + G + g)),
            pl.BlockSpec((1, gw), lambda b, g, c: (0, g)),
            pl.BlockSpec((1, N), lambda b, g, c: (0, cw_b0 + g)),
            pl.BlockSpec((1, N), lambda b, g, c: (0, cw_b0 + G + g)),
            pl.BlockSpec((1, halo, gw), lambda b, g, c: (b, 0, g)),
            pl.BlockSpec((1, halo, N), lambda b, g, c: (b, 0, cw_b0 + g)),
            pl.BlockSpec((1, halo, N), lambda b, g, c: (b, 0, cw_b0 + G + g)),
            pl.BlockSpec((1, gw, N), lambda b, g, c: (b, g, 0)),
        ],
        out_specs=[
            pl.BlockSpec((L, gw), lambda b, g, c: (row(b, g, c), g)),
            pl.BlockSpec((1, gw, N), lambda b, g, c: (b, g, 0)),
        ],
        out_shape=[jax.ShapeDtypeStruct((B * T, d_inner), F32),
                   jax.ShapeDtypeStruct((B, d_inner, N), F32)],
        scratch_shapes=[pltpu.VMEM((halo + L, gw), F32), pltpu.VMEM((halo + L, N), F32),
                        pltpu.VMEM((halo + L, N), F32),
                        pltpu.VMEM((halo, gw), F32), pltpu.VMEM((halo, N), F32), pltpu.VMEM((halo, N), F32)],
        compiler_params=_cparams(3),
        name="ssd",
    )(d_skip, proj, proj, proj, proj, proj, pad_heads(dt_bias), pad_heads(a_log),
      conv_w, conv_w, conv_w, conv_b.reshape(1, conv_dim), conv_b.reshape(1, conv_dim),
      conv_b.reshape(1, conv_dim), conv0p, conv0p, conv0p, s0.reshape(B, d_inner, N))
    return y, s.reshape(B, heads, P, N)


def _prep_weights(w_in_a, w_in_c, d_inner):
    gates = 2 * MLSTM_HEADS
    wa_main = w_in_a[:, :, :-gates]
    wa_gate = jnp.pad(w_in_a[:, :, -gates:], ((0, 0), (0, 0), (0, LANES - gates)))
    wa = jnp.concatenate([wa_main, wa_gate], axis=-1).astype(BF16)
    heads = d_inner // SSD_HEAD_DIM
    hpg = heads // SSD_GROUPS
    wc_main = w_in_c[:, :, :-heads]
    wc_dt = w_in_c[:, :, -heads:].reshape(w_in_c.shape[0], w_in_c.shape[1], SSD_GROUPS, hpg)
    wc_dt = jnp.pad(wc_dt, ((0, 0), (0, 0), (0, 0), (0, LANES - hpg)))
    wc_dt = wc_dt.reshape(w_in_c.shape[0], w_in_c.shape[1], SSD_GROUPS * LANES)
    wc = jnp.concatenate([wc_main, wc_dt], axis=-1).astype(BF16)
    return wa, wc


def _trunk(x3, st, W):
    B, T, D = x3.shape
    fresh = st is None
    x = x3.reshape(B * T, D)
    depth = W["norm_mix"].shape[0]
    o_c, o_n, o_m, o_k, o_v, o_s, o_cv, o_f = [], [], [], [], [], [], [], []
    for i in range(depth):
        j = i // N_MIXERS
        kind = i % N_MIXERS
        if kind == 0:
            dv = D // MLSTM_HEADS
            dk = dv // 2
            if fresh:
                c0 = jnp.zeros((B, MLSTM_HEADS, dk, dv), F32)
                n0 = jnp.zeros((B, MLSTM_HEADS, dk), F32)
                m0 = jnp.zeros((B, MLSTM_HEADS), F32)
            else:
                c0, n0, m0 = st["c"][j], st["n"][j], st["m"][j]
            proj = _matmul(x, W["w_in_a"][j], gain=W["norm_mix"][i])
            bias = jnp.pad(jnp.concatenate([W["b_ig_a"][j], W["b_fg_a"][j]]), (0, LANES - 2 * MLSTM_HEADS))
            hg, c, n, m = _mlstm_mix(proj, bias.reshape(1, LANES), W["g_head_a"][j], c0, n0, m0, B, T)
            x = _matmul(hg, W["w_out_a"][j], res=x)
            o_c.append(c); o_n.append(n); o_m.append(m)
        elif kind == 1:
            hd = D // SWA_HEADS
            kvw = SWA_KV_HEADS * hd
            qw = SWA_HEADS * hd
            proj = _matmul(x, W["w_in_b"][j], gain=W["norm_mix"][i])
            p3 = proj.reshape(B, T, qw + 2 * kvw)
            k_new, v_new = p3[:, :, qw:qw + kvw], p3[:, :, qw + kvw:]
            kb, vb = qw // kvw, qw // kvw + 1
            if fresh:
                nb = T // WINDOW
                prev = lambda b, n: (b * nb + jnp.maximum(n - 1, 0), kb)
                cur = lambda b, n: (b * nb + n, kb)
                prev_v = lambda b, n: (b * nb + jnp.maximum(n - 1, 0), vb)
                cur_v = lambda b, n: (b * nb + n, vb)
                o = _swa_call(W["sinks_b"][j], proj, lambda b, n: (b * nb + n, 0), proj, prev, cur,
                              proj, prev_v, cur_v, (B, nb), WINDOW, B * T, hd, True)
                kc, vc = k_new[:, -WINDOW:], v_new[:, -WINDOW:]
            else:
                kbuf = st["k"][j].reshape(B, WINDOW, kvw)
                vbuf = st["v"][j].reshape(B, WINDOW, kvw)
                zpad = jnp.zeros((B, WINDOW - T, kvw), F32)
                kk = jnp.concatenate([kbuf, k_new, zpad], axis=1).reshape(B * 2 * WINDOW, kvw)
                vv = jnp.concatenate([vbuf, v_new, zpad], axis=1).reshape(B * 2 * WINDOW, kvw)
                o = _swa_call(W["sinks_b"][j], proj, lambda b, n: (b, 0), kk, lambda b, n: (2 * b, 0),
                              lambda b, n: (2 * b + 1, 0), vv, lambda b, n: (2 * b, 0),
                              lambda b, n: (2 * b + 1, 0), (B, 1), T, B * T, hd, False)
                kc = jnp.concatenate([kbuf[:, T:], k_new], axis=1)
                vc = jnp.concatenate([vbuf[:, T:], v_new], axis=1)
            x = _matmul(o, W["w_out_b"][j], res=x)
            o_k.append(kc.reshape(B, WINDOW, SWA_KV_HEADS, hd))
            o_v.append(vc.reshape(B, WINDOW, SWA_KV_HEADS, hd))
        else:
            d_inner = W["w_out_c"].shape[1]
            heads = d_inner // SSD_HEAD_DIM
            conv_dim = d_inner + 2 * SSD_GROUPS * SSD_STATE
            if fresh:
                s0 = jnp.zeros((B, heads, SSD_HEAD_DIM, SSD_STATE), F32)
                cb0 = jnp.zeros((B, SSD_CONV - 1, conv_dim), F32)
            else:
                s0, cb0 = st["ssm"][j], st["conv"][j]
            proj = _matmul(x, W["w_in_c"][j], gain=W["norm_mix"][i])
            y, s = _ssd_mix(proj, d_inner, W["conv_w_c"][j], W["conv_b_c"][j], W["dt_bias_c"][j],
                            W["a_log_c"][j], W["d_skip_c"][j], s0, cb0, B, T)
            x = _matmul(y, W["w_out_c"][j], gain=W["g_norm_c"][j], res=x)
            xbc = proj.reshape(B, T, -1)[:, -(SSD_CONV - 1):, d_inner:d_inner + conv_dim]
            o_s.append(s); o_cv.append(xbc)

        final_gain = W["norm_final"] if i == depth - 1 else None
        d_ff = W["w_ffn_out"].shape[1]
        if fresh:
            fb0 = jnp.zeros((B, FFN_CONV - 1, d_ff), F32)
            x, fb = _ffn(x, W["norm_ffn"][i], W["w_ffn_in"][i], W["ffn_conv_w"][i], W["ffn_conv_b"][i],
                         W["w_ffn_out"][i], fb0, 1, T, final_gain)
        else:
            xt = x.reshape(B, T, D).transpose(1, 0, 2).reshape(T * B, D)
            fb0 = st["ffn"][i].transpose(1, 0, 2).reshape(1, (FFN_CONV - 1) * B, d_ff)
            xt, fb = _ffn(xt, W["norm_ffn"][i], W["w_ffn_in"][i], W["ffn_conv_w"][i], W["ffn_conv_b"][i],
                          W["w_ffn_out"][i], fb0, B, T * B, final_gain)
            x = xt.reshape(T, B, D).transpose(1, 0, 2).reshape(B * T, D)
            fb = fb.reshape(FFN_CONV - 1, B, d_ff).transpose(1, 0, 2)
        o_f.append(fb)
    return (x.reshape(B, T, D), jnp.stack(o_c), jnp.stack(o_n), jnp.stack(o_m), jnp.stack(o_k),
            jnp.stack(o_v), jnp.stack(o_s), jnp.stack(o_cv), jnp.stack(o_f))


def kernel(x_prompt, x_sample, state_mlstm_c, state_mlstm_n, state_mlstm_m, cache_swa_k, cache_swa_v, state_ssm, state_ssm_conv, state_ffn_conv, norm_mix, norm_ffn, norm_final, w_in_a, b_ig_a, b_fg_a, g_head_a, w_out_a, w_in_b, sinks_b, w_out_b, w_in_c, conv_w_c, conv_b_c, dt_bias_c, a_log_c, d_skip_c, g_norm_c, w_out_c, w_ffn_in, ffn_conv_w, ffn_conv_b, w_ffn_out):
    wa, wc = _prep_weights(w_in_a, w_in_c, w_out_c.shape[1])
    W = dict(norm_mix=norm_mix, norm_ffn=norm_ffn, norm_final=norm_final,
             w_in_a=wa, b_ig_a=b_ig_a, b_fg_a=b_fg_a, g_head_a=g_head_a, w_out_a=w_out_a.astype(BF16),
             w_in_b=w_in_b.astype(BF16), sinks_b=sinks_b, w_out_b=w_out_b.astype(BF16),
             w_in_c=wc, conv_w_c=conv_w_c, conv_b_c=conv_b_c, dt_bias_c=dt_bias_c, a_log_c=a_log_c,
             d_skip_c=d_skip_c, g_norm_c=g_norm_c, w_out_c=w_out_c.astype(BF16),
             w_ffn_in=w_ffn_in.astype(BF16), ffn_conv_w=ffn_conv_w, ffn_conv_b=ffn_conv_b,
             w_ffn_out=w_ffn_out.astype(BF16))
    st = dict(c=state_mlstm_c, n=state_mlstm_n, m=state_mlstm_m, k=cache_swa_k, v=cache_swa_v,
              ssm=state_ssm, conv=state_ssm_conv, ffn=state_ffn_conv)
    yp, pc, pn, pm, pk, pv, ps, pcv, pf = _trunk(x_prompt, None, W)
    ys, sc, sn, sm, sk, sv, ss, scv, sf = _trunk(x_sample, st, W)
    return (yp, ys, pc, sc, pn, sn, pm, sm, pk, sk, pv, sv, ps, ss, pcv, scv, pf, sf)
```

```python
import functools
import math

import jax
import jax.numpy as jnp
from jax import lax
from jax.experimental import pallas as pl
from jax.experimental.pallas import tpu as pltpu

F32 = jnp.float32
BF16 = jnp.bfloat16

RMS_EPS = 1e-6
N_MIXERS = 3

MLSTM_HEADS = 8
SWA_HEADS = 32
SWA_KV_HEADS = 4
SWA_GROUP = SWA_HEADS // SWA_KV_HEADS
WINDOW = 128
SSD_HEAD_DIM = 64
SSD_GROUPS = 8
SSD_STATE = 128
SSD_CONV = 4
FFN_CONV = 3

LANES = 128
SUBLANES = 8
VMEM_LIMIT_BYTES = 56 << 20
VMEM_TILE_BUDGET = 40 << 20

FFN_ROW_TILE = 1024
MLSTM_CHUNK = 128
SSD_CHUNK = 128


def _cparams(n_axes):
    return pltpu.CompilerParams(dimension_semantics=("arbitrary",) * n_axes,
                                vmem_limit_bytes=VMEM_LIMIT_BYTES)


def _round_up(a, b):
    return (a + b - 1) // b * b


def _largest_divisor(n, candidates):
    for c in candidates:
        if n % c == 0:
            return c
    raise ValueError(f"no tile in {candidates} divides {n}")


def _rmsnorm(x, g):
    return x * lax.rsqrt(jnp.mean(x * x, axis=-1, keepdims=True) + RMS_EPS) * g


def _silu(x):
    return x * jax.nn.sigmoid(x)


def _mm_body(*refs, norm, residual):
    it = iter(refs)
    x_ref = next(it)
    g_ref = next(it) if norm else None
    w_ref = next(it)
    r_ref = next(it) if residual else None
    o_ref = next(it)
    xn_ref = next(it) if norm else None
    if norm:
        @pl.when(pl.program_id(1) == 0)
        def _():
            xn_ref[...] = _rmsnorm(x_ref[...], g_ref[...]).astype(BF16)
        lhs = xn_ref[...]
    else:
        lhs = x_ref[...].astype(BF16)
    acc = jnp.dot(lhs, w_ref[...], preferred_element_type=F32)
    if residual:
        acc = acc + r_ref[...]
    o_ref[...] = acc.astype(o_ref.dtype)


def _mm_tiles(m, k, n, x_bytes, norm, residual):
    for tm in (1024, 512, 256, 128, 64, 32, 16, 8):
        if m % tm:
            continue
        for tn in (1024, 896, 512, 256, 128):
            if n % tn:
                continue
            need = 2 * tm * k * x_bytes + 2 * k * tn * 2 + 2 * tm * tn * 4
            need += tm * k * 2 if norm else 0
            need += 2 * tm * tn * 4 if residual else 0
            if need <= VMEM_TILE_BUDGET:
                return tm, tn
    raise ValueError("no matmul tile fits VMEM")


def _matmul(x, w, gain=None, res=None):
    m, k = x.shape
    n = w.shape[1]
    norm, residual = gain is not None, res is not None
    tm, tn = _mm_tiles(m, k, n, x.dtype.itemsize, norm, residual)
    in_specs = [pl.BlockSpec((tm, k), lambda i, j: (i, 0))]
    args = [x]
    if norm:
        in_specs.append(pl.BlockSpec((1, k), lambda i, j: (0, 0)))
        args.append(gain.reshape(1, k))
    in_specs.append(pl.BlockSpec((k, tn), lambda i, j: (0, j)))
    args.append(w)
    if residual:
        in_specs.append(pl.BlockSpec((tm, tn), lambda i, j: (i, j)))
        args.append(res)
    return pl.pallas_call(
        functools.partial(_mm_body, norm=norm, residual=residual),
        grid=(m // tm, n // tn),
        in_specs=in_specs,
        out_specs=pl.BlockSpec((tm, tn), lambda i, j: (i, j)),
        out_shape=jax.ShapeDtypeStruct((m, n), F32),
        scratch_shapes=[pltpu.VMEM((tm, k), BF16)] if norm else [],
        compiler_params=_cparams(2),
        name="matmul",
    )(*args)


def _ffn_body(x_ref, gn_ref, wu_ref, wg_ref, cw_ref, cb_ref, wo_ref, buf0_ref, gf_ref,
              o_ref, bufo_ref, xn_ref, gs_ref, carry_ref,
              *, tm, shift, halo, tiles_per_seq, nj, final_norm):
    i = pl.program_id(0)
    j = pl.program_id(1)

    @pl.when(j == 0)
    def _():
        x = x_ref[...]
        xn_ref[...] = _rmsnorm(x, gn_ref[...]).astype(BF16)
        o_ref[...] = x

    xn = xn_ref[...]
    u = jnp.dot(xn, wu_ref[0], preferred_element_type=F32)
    g = jnp.dot(xn, wg_ref[0], preferred_element_type=F32)

    first = (i % tiles_per_seq) == 0
    gs_ref[0:halo, :] = jnp.where(first, buf0_ref[0], carry_ref[j])
    gs_ref[halo:halo + tm, :] = g
    tail = gs_ref[tm:tm + halo, :]
    carry_ref[j] = tail
    bufo_ref[0] = tail

    base = halo - (FFN_CONV - 1) * shift
    cw = cw_ref[...]
    c = gs_ref[base:base + tm, :] * cw[0:1, :]
    for t in range(1, FFN_CONV):
        c = c + gs_ref[base + t * shift:base + t * shift + tm, :] * cw[t:t + 1, :]
    c = c + cb_ref[...]
    h = (_silu(c) * u).astype(BF16)
    o_ref[...] += jnp.dot(h, wo_ref[...], preferred_element_type=F32)

    if final_norm:
        @pl.when(j == nj - 1)
        def _():
            o_ref[...] = _rmsnorm(o_ref[...], gf_ref[...])


def _ffn(x, gain, w_in, conv_w, conv_b, w_out, buf0, shift, rows_per_seq, final_gain):
    m, d = x.shape
    d_ff = w_out.shape[0]
    tm = _largest_divisor(rows_per_seq, (FFN_ROW_TILE, 512, 256, 128, 64, 32, 16, 8))
    tf = w_in.shape[2]
    once = pl.Buffered(1)
    nj = d_ff // tf
    tiles_per_seq = rows_per_seq // tm
    pre = (FFN_CONV - 1) * shift
    halo = _round_up(pre, SUBLANES)
    assert tm >= halo and tm % SUBLANES == 0
    buf0p = jnp.pad(buf0, ((0, 0), (halo - pre, 0), (0, 0)))
    final_norm = final_gain is not None
    gf = (final_gain if final_norm else gain).reshape(1, d)
    body = functools.partial(_ffn_body, tm=tm, shift=shift, halo=halo, tiles_per_seq=tiles_per_seq,
                             nj=nj, final_norm=final_norm)
    out, bufo = pl.pallas_call(
        body,
        grid=(m // tm, nj),
        in_specs=[
            pl.BlockSpec((tm, d), lambda i, j: (i, 0), pipeline_mode=once),
            pl.BlockSpec((1, d), lambda i, j: (0, 0)),
            pl.BlockSpec((1, d, tf), lambda i, j: (j, 0, 0)),
            pl.BlockSpec((1, d, tf), lambda i, j: (nj + j, 0, 0)),
            pl.BlockSpec((FFN_CONV, tf), lambda i, j: (0, j)),
            pl.BlockSpec((1, tf), lambda i, j: (0, j)),
            pl.BlockSpec((tf, d), lambda i, j: (j, 0)),
            pl.BlockSpec((1, halo, tf), lambda i, j: (i // tiles_per_seq, 0, j)),
            pl.BlockSpec((1, d), lambda i, j: (0, 0)),
        ],
        out_specs=[
            pl.BlockSpec((tm, d), lambda i, j: (i, 0), pipeline_mode=once),
            pl.BlockSpec((1, halo, tf), lambda i, j: (i, 0, j)),
        ],
        out_shape=[jax.ShapeDtypeStruct((m, d), F32),
                   jax.ShapeDtypeStruct((m // tm, halo, d_ff), F32)],
        scratch_shapes=[pltpu.VMEM((tm, d), BF16),
                        pltpu.VMEM((halo + tm, tf), F32),
                        pltpu.VMEM((nj, halo, tf), F32)],
        compiler_params=_cparams(2),
        name="conv_ffn",
    )(x, gain.reshape(1, d), w_in, w_in, conv_w, conv_b.reshape(1, d_ff), w_out, buf0p, gf)
    return out, bufo[tiles_per_seq - 1::tiles_per_seq, halo - pre:, :]


def _row_from_col(col, eye):
    return jnp.sum(jnp.where(eye, col, 0.0), axis=0, keepdims=True)


def _cumsum_rows(x, tril):
    return jnp.dot(tril, x, preferred_element_type=F32, precision=lax.Precision.HIGHEST)


def _mlstm_body(q_ref, k_ref, v_ref, o_ref, gate_ref, bias_ref, gh_ref, c0_ref, n0_ref, m0_ref,
                hg_ref, c_ref, n_ref, m_ref, *, chunk, dk, dv):
    L = chunk
    H = MLSTM_HEADS

    @pl.when(pl.program_id(1) == 0)
    def _():
        c_ref[...] = c0_ref[...]
        n_ref[...] = n0_ref[...]
        m_ref[...] = m0_ref[...]

    ri = lax.broadcasted_iota(jnp.int32, (L, L), 0)
    ci = lax.broadcasted_iota(jnp.int32, (L, L), 1)
    causal = ri >= ci
    eye = ri == ci
    tril = causal.astype(F32)

    gates = gate_ref[...] + bias_ref[...]
    x = gates
    log_sig = jnp.minimum(x, 0.0) - jnp.log1p(jnp.exp(-jnp.abs(x)))
    bcum = _cumsum_rows(log_sig, tril)

    for h in range(H):
        ig_col = gates[:, h:h + 1]
        b_col = bcum[:, H + h:H + h + 1]
        ig_row = _row_from_col(ig_col, eye)
        b_row = _row_from_col(b_col, eye)
        m_prev = m_ref[0, h:h + 1, 0:1]

        log_d = jnp.where(causal, b_col - b_row + ig_row, -jnp.inf)
        log_inter = b_col + m_prev
        m_t = jnp.maximum(log_inter, jnp.max(log_d, axis=-1, keepdims=True))
        d = jnp.exp(log_d - m_t)
        inter = jnp.exp(log_inter - m_t)

        qf = q_ref[:, h * dk:(h + 1) * dk] * (dk ** -0.5)
        kf = k_ref[:, h * dk:(h + 1) * dk]
        qb = qf.astype(BF16)
        kb = kf.astype(BF16)
        vb = v_ref[:, h * dv:(h + 1) * dv].astype(BF16)
        c_prev = c_ref[0, h]
        n_prev = n_ref[0, h:h + 1, :]

        s = lax.dot_general(qb, kb, (((1,), (1,)), ((), ())), preferred_element_type=F32) * d
        num = inter * jnp.dot(qb, c_prev.astype(BF16), preferred_element_type=F32)
        num = num + jnp.dot(s.astype(BF16), vb, preferred_element_type=F32)
        den = inter * jnp.sum(qf * n_prev, axis=-1, keepdims=True) + jnp.sum(s, axis=-1, keepdims=True)
        hh = num / jnp.maximum(jnp.abs(den), jnp.exp(-m_t))

        hn = hh * lax.rsqrt(jnp.mean(hh * hh, axis=-1, keepdims=True) + RMS_EPS)
        hn = hn * gh_ref[:, h * dv:(h + 1) * dv]
        out = jax.nn.sigmoid(o_ref[:, h * dv:(h + 1) * dv]) * hn
        hg_ref[:, h * dv:(h + 1) * dv] = out.astype(hg_ref.dtype)

        b_last = b_col[L - 1:L, :]
        log_w = b_last - b_col + ig_col
        m_new = jnp.maximum(b_last + m_prev, jnp.max(log_w, axis=0, keepdims=True))
        w = jnp.exp(log_w - m_new)
        decay = jnp.exp(b_last + m_prev - m_new)
        kw = kf * w
        upd = lax.dot_general(kw.astype(BF16), vb, (((0,), (0,)), ((), ())), preferred_element_type=F32)
        c_ref[0, h] = decay * c_prev + upd
        n_ref[0, h:h + 1, :] = decay * n_prev + jnp.sum(kw, axis=0, keepdims=True)
        m_ref[0, h:h + 1, :] = jnp.broadcast_to(m_new, (1, LANES))


def _mlstm_mix(proj, bias, g_head, c0, n0, m0, B, T):
    H = MLSTM_HEADS
    dk, dv = c0.shape[2], c0.shape[3]
    L = math.gcd(T, MLSTM_CHUNK)
    nc = T // L
    qw, vw = H * dk, H * dv
    assert qw * 2 == vw and vw % LANES == 0
    gate_blk = (2 * qw + 2 * vw) // LANES
    hg_dtype = BF16 if L % 16 == 0 else F32
    m0b = jnp.broadcast_to(m0[:, :, None], (B, H, LANES))
    row = lambda b, c: b * nc + c
    hg, c, n, m = pl.pallas_call(
        functools.partial(_mlstm_body, chunk=L, dk=dk, dv=dv),
        grid=(B, nc),
        in_specs=[
            pl.BlockSpec((L, qw), lambda b, c: (row(b, c), 0)),
            pl.BlockSpec((L, qw), lambda b, c: (row(b, c), 1)),
            pl.BlockSpec((L, vw), lambda b, c: (row(b, c), 1)),
            pl.BlockSpec((L, vw), lambda b, c: (row(b, c), 2)),
            pl.BlockSpec((L, LANES), lambda b, c: (row(b, c), gate_blk)),
            pl.BlockSpec((1, LANES), lambda b, c: (0, 0)),
            pl.BlockSpec((1, vw), lambda b, c: (0, 0)),
            pl.BlockSpec((1, H, dk, dv), lambda b, c: (b, 0, 0, 0)),
            pl.BlockSpec((1, H, dk), lambda b, c: (b, 0, 0)),
            pl.BlockSpec((1, H, LANES), lambda b, c: (b, 0, 0)),
        ],
        out_specs=[
            pl.BlockSpec((L, vw), lambda b, c: (row(b, c), 0)),
            pl.BlockSpec((1, H, dk, dv), lambda b, c: (b, 0, 0, 0)),
            pl.BlockSpec((1, H, dk), lambda b, c: (b, 0, 0)),
            pl.BlockSpec((1, H, LANES), lambda b, c: (b, 0, 0)),
        ],
        out_shape=[jax.ShapeDtypeStruct((B * T, vw), hg_dtype),
                   jax.ShapeDtypeStruct((B, H, dk, dv), F32),
                   jax.ShapeDtypeStruct((B, H, dk), F32),
                   jax.ShapeDtypeStruct((B, H, LANES), F32)],
        compiler_params=_cparams(2),
        name="mlstm",
    )(proj, proj, proj, proj, proj, bias, g_head.reshape(1, vw), c0, n0, m0b)
    return hg, c, n, m[:, :, 0]


def _swa_body(sink_ref, q_ref, kp_ref, kc_ref, vp_ref, vc_ref, o_ref, *, tq, hd, mask_first):
    W = WINDOW
    G = SWA_GROUP
    S = 2 * W
    kk = jnp.concatenate([kp_ref[...], kc_ref[...]], axis=0)
    vv = jnp.concatenate([vp_ref[...], vc_ref[...]], axis=0)
    qi = lax.broadcasted_iota(jnp.int32, (tq, S), 0)
    kj = lax.broadcasted_iota(jnp.int32, (tq, S), 1)
    dist = W + qi - kj
    valid = (dist >= 0) & (dist <= W)
    if mask_first:
        valid = valid & ((kj >= W) | (pl.program_id(1) > 0))
    distf = dist.astype(F32)
    lane_s = lax.broadcasted_iota(jnp.int32, (S, LANES), 1)
    lane_q = lax.broadcasted_iota(jnp.int32, (tq, LANES), 1)
    low_s = lane_s < hd
    low_q = lane_q < hd
    heads_per_slab = LANES // hd
    assert heads_per_slab == 2

    for kh in range(SWA_KV_HEADS):
        slab = (kh // heads_per_slab) * LANES
        keep_low = (kh % heads_per_slab) == 0
        k128 = kk[:, slab:slab + LANES]
        v128 = vv[:, slab:slab + LANES]
        k_rot = pltpu.roll(k128, hd, axis=1)
        v_rot = pltpu.roll(v128, hd, axis=1)
        own = low_s if keep_low else jnp.logical_not(low_s)
        k2 = jnp.where(own, k128, k_rot).astype(BF16)
        v2 = jnp.where(own, v128, v_rot).astype(BF16)

        q_parts = []
        for g in range(G):
            h = kh * G + g
            qs = q_ref[:, (h // 2) * LANES:(h // 2 + 1) * LANES]
            keep = low_q if h % 2 == 0 else jnp.logical_not(low_q)
            q_parts.append(jnp.where(keep, qs, 0.0))
        q_stack = jnp.concatenate(q_parts, axis=0).astype(BF16)
        s_all = lax.dot_general(q_stack, k2, (((1,), (1,)), ((), ())), preferred_element_type=F32)
        s_all = s_all * (hd ** -0.5)

        p_parts = []
        for g in range(G):
            h = kh * G + g
            slope = 2.0 ** (-8.0 * (h + 1) / SWA_HEADS)
            sink = sink_ref[h]
            s = s_all[g * tq:(g + 1) * tq, :] - slope * distf
            s = jnp.where(valid, s, -jnp.inf)
            mx = jnp.maximum(jnp.max(s, axis=-1, keepdims=True), sink)
            p = jnp.exp(s - mx)
            p = p / (jnp.sum(p, axis=-1, keepdims=True) + jnp.exp(sink - mx))
            p_parts.append(p)
        p_stack = jnp.concatenate(p_parts, axis=0).astype(BF16)
        o_all = jnp.dot(p_stack, v2, preferred_element_type=F32)

        for g2 in range(G // 2):
            h = kh * G + 2 * g2
            o_even = o_all[(2 * g2) * tq:(2 * g2 + 1) * tq, :]
            o_odd = o_all[(2 * g2 + 1) * tq:(2 * g2 + 2) * tq, :]
            o_ref[:, (h // 2) * LANES:(h // 2 + 1) * LANES] = jnp.where(low_q, o_even, o_odd).astype(o_ref.dtype)


def _swa_call(sinks, q_arr, q_blk, k_arr, kp_map, kc_map, v_arr, vp_map, vc_map, grid, tq, n_rows, hd,
              mask_first):
    kvw = SWA_KV_HEADS * hd
    qw = SWA_HEADS * hd
    out_dtype = BF16 if tq % 16 == 0 else F32
    return pl.pallas_call(
        functools.partial(_swa_body, tq=tq, hd=hd, mask_first=mask_first),
        grid=grid,
        in_specs=[
            pl.BlockSpec(memory_space=pltpu.SMEM),
            pl.BlockSpec((tq, qw), q_blk),
            pl.BlockSpec((WINDOW, kvw), kp_map),
            pl.BlockSpec((WINDOW, kvw), kc_map),
            pl.BlockSpec((WINDOW, kvw), vp_map),
            pl.BlockSpec((WINDOW, kvw), vc_map),
        ],
        out_specs=pl.BlockSpec((tq, qw), q_blk),
        out_shape=jax.ShapeDtypeStruct((n_rows, qw), out_dtype),
        compiler_params=_cparams(2),
        name="swa",
    )(sinks, q_arr, k_arr, k_arr, v_arr, v_arr)


def _conv_silu(u_ref, init_ref, carry_ref, gs_ref, cw_ref, cb_ref, first, L):
    halo = SUBLANES
    gs_ref[0:halo, :] = jnp.where(first, init_ref[0], carry_ref[...])
    gs_ref[halo:halo + L, :] = u_ref[...]
    carry_ref[...] = gs_ref[L:L + halo, :]
    base = halo - (SSD_CONV - 1)
    cw = cw_ref[...]
    c = gs_ref[base:base + L, :] * cw[0:1, :]
    for t in range(1, SSD_CONV):
        c = c + gs_ref[base + t:base + t + L, :] * cw[t:t + 1, :]
    c = c + cb_ref[...]
    return _silu(c)


def _split3_bf16(x):
    hi = x.astype(BF16)
    r = x - hi.astype(F32)
    mid = r.astype(BF16)
    lo = (r - mid.astype(F32)).astype(BF16)
    return hi, mid, lo


def _ssd_body(z_ref, xs_ref, bc_ref, dt_ref, dtb_ref, alog_ref, dsk_ref,
              cwx_ref, cwbc_ref, cbx_ref, cbbc_ref, ix_ref, ibc_ref, s0_ref,
              y_ref, s_ref,
              gx_ref, gbc_ref, kx_ref, kbc_ref, *, chunk, groups, hpg, p_dim):
    L = chunk
    N = SSD_STATE
    gw = hpg * p_dim
    first = pl.program_id(1) == 0

    @pl.when(first)
    def _():
        s_ref[...] = s0_ref[...]

    xs_all = _conv_silu(xs_ref, ix_ref, kx_ref, gx_ref, cwx_ref, cbx_ref, first, L)
    bc_all = _conv_silu(bc_ref, ibc_ref, kbc_ref, gbc_ref, cwbc_ref, cbbc_ref, first, L)

    ri = lax.broadcasted_iota(jnp.int32, (L, L), 0)
    ci = lax.broadcasted_iota(jnp.int32, (L, L), 1)
    causal = ri >= ci
    eye = ri == ci
    tril = causal.astype(F32)

    dt_all = jax.nn.softplus(dt_ref[...] + dtb_ref[...])
    acum_all = _cumsum_rows(dt_all * (-jnp.exp(alog_ref[...])), tril)
    a_last_all = acum_all[L - 1:L, :]
    eacum_all = jnp.exp(acum_all)
    dtw_all = dt_all * jnp.exp(a_last_all - acum_all)
    ea_last_all = jnp.exp(a_last_all)

    er = lax.broadcasted_iota(jnp.int32, (3 * LANES, gw), 0)
    ec = lax.broadcasted_iota(jnp.int32, (3 * LANES, gw), 1)
    assert p_dim & (p_dim - 1) == 0
    head_of_lane = lax.shift_right_logical(ec, p_dim.bit_length() - 1)
    expand3 = (jnp.bitwise_and(er, LANES - 1) == head_of_lane).astype(BF16)

    lane = lax.broadcasted_iota(jnp.int32, (L, LANES), 1)
    low = lane < p_dim
    srow = lax.broadcasted_iota(jnp.int32, (LANES, 1), 0)
    heads_per_slab = LANES // p_dim
    assert heads_per_slab == 2

    for g in range(groups):
        hs = slice(g * LANES, (g + 1) * LANES)
        acum = acum_all[:, hs]
        ea_last = ea_last_all[:, hs]
        per_head = jnp.concatenate([dt_all[:, hs], eacum_all[:, hs], dtw_all[:, hs]], axis=0)
        ex = jnp.dot(jnp.concatenate(_split3_bf16(per_head), axis=1), expand3,
                     preferred_element_type=F32)
        dt_e, ea_e, dtw_e = ex[0:L], ex[L:2 * L], ex[2 * L:3 * L]

        xs = xs_all[:, g * gw:(g + 1) * gw]
        bmb = bc_all[:, g * N:(g + 1) * N].astype(BF16)
        cmb = bc_all[:, (groups + g) * N:(groups + g + 1) * N].astype(BF16)
        cb = lax.dot_general(cmb, bmb, (((1,), (1,)), ((), ())), preferred_element_type=F32)
        xdt = (xs * dt_e).astype(BF16)
        xw = (xs * dtw_e).astype(BF16)
        s_prev = s_ref[0, g * gw:(g + 1) * gw, :]
        y_inter = lax.dot_general(cmb, s_prev.astype(BF16), (((1,), (1,)), ((), ())),
                                  preferred_element_type=F32)
        upd = lax.dot_general(xw, bmb, (((0,), (0,)), ((), ())), preferred_element_type=F32)
        acum_t = acum.T if L % LANES == 0 else None

        y_parts = []
        for pr in range(hpg // heads_per_slab):
            h0 = 2 * pr
            ms = []
            for hh in (h0, h0 + 1):
                a_col = acum[:, hh:hh + 1]
                a_row = acum_t[hh:hh + 1, :] if acum_t is not None else _row_from_col(a_col, eye)
                dec = jnp.exp(jnp.where(causal, a_col - a_row, -jnp.inf))
                ms.append((cb * dec).astype(BF16))
            mcat = jnp.concatenate(ms, axis=1)
            xb = xdt[:, pr * LANES:(pr + 1) * LANES]
            zero = jnp.zeros_like(xb)
            bd = jnp.concatenate([jnp.where(low, xb, zero), jnp.where(low, zero, xb)], axis=0)
            y_parts.append(jnp.dot(mcat, bd, preferred_element_type=F32))

            scale = jnp.where(srow < p_dim, ea_last[:, h0:h0 + 1], ea_last[:, h0 + 1:h0 + 2])
            rows = slice(g * gw + pr * LANES, g * gw + (pr + 1) * LANES)
            s_ref[0, rows, :] = s_prev[pr * LANES:(pr + 1) * LANES] * scale + upd[pr * LANES:(pr + 1) * LANES]

        y = jnp.concatenate(y_parts, axis=1) + y_inter * ea_e
        y = y + dsk_ref[:, g * gw:(g + 1) * gw] * xs
        y_ref[:, g * gw:(g + 1) * gw] = y * _silu(z_ref[:, g * gw:(g + 1) * gw])


def _ssd_mix(proj, d_inner, conv_w, conv_b, dt_bias, a_log, d_skip, s0, conv0, B, T):
    G, P, N = SSD_GROUPS, SSD_HEAD_DIM, SSD_STATE
    heads = d_inner // P
    hpg = heads // G
    L = math.gcd(T, SSD_CHUNK)
    nc = T // L
    bcw = 2 * G * N
    dtw = G * LANES
    conv_dim = d_inner + bcw
    assert d_inner % bcw == 0 and (d_inner + conv_dim) % dtw == 0
    bc_blk = (2 * d_inner) // bcw
    dt_blk = (d_inner + conv_dim) // dtw
    pad_heads = lambda v: jnp.pad(v.reshape(G, hpg), ((0, 0), (0, LANES - hpg))).reshape(1, dtw)
    halo = SUBLANES
    conv0p = jnp.pad(conv0, ((0, 0), (halo - (SSD_CONV - 1), 0), (0, 0)))
    conv_b2 = conv_b.reshape(1, conv_dim)
    dsk = jnp.repeat(d_skip, P).reshape(1, d_inner)
    row = lambda b, c: b * nc + c
    const = lambda b, c: (0, 0)
    y, s = pl.pallas_call(
        functools.partial(_ssd_body, chunk=L, groups=G, hpg=hpg, p_dim=P),
        grid=(B, nc),
        in_specs=[
            pl.BlockSpec((L, d_inner), lambda b, c: (row(b, c), 0)),
            pl.BlockSpec((L, d_inner), lambda b, c: (row(b, c), 1)),
            pl.BlockSpec((L, bcw), lambda b, c: (row(b, c), bc_blk)),
            pl.BlockSpec((L, dtw), lambda b, c: (row(b, c), dt_blk)),
            pl.BlockSpec((1, dtw), const),
            pl.BlockSpec((1, dtw), const),
            pl.BlockSpec((1, d_inner), const),
            pl.BlockSpec((SSD_CONV, d_inner), const),
            pl.BlockSpec((SSD_CONV, bcw), lambda b, c: (0, d_inner // bcw)),
            pl.BlockSpec((1, d_inner), const),
            pl.BlockSpec((1, bcw), lambda b, c: (0, d_inner // bcw)),
            pl.BlockSpec((1, halo, d_inner), lambda b, c: (b, 0, 0)),
            pl.BlockSpec((1, halo, bcw), lambda b, c: (b, 0, d_inner // bcw)),
            pl.BlockSpec((1, d_inner, N), lambda b, c: (b, 0, 0)),
        ],
        out_specs=[
            pl.BlockSpec((L, d_inner), lambda b, c: (row(b, c), 0)),
            pl.BlockSpec((1, d_inner, N), lambda b, c: (b, 0, 0)),
        ],
        out_shape=[jax.ShapeDtypeStruct((B * T, d_inner), F32),
                   jax.ShapeDtypeStruct((B, d_inner, N), F32)],
        scratch_shapes=[pltpu.VMEM((halo + L, d_inner), F32), pltpu.VMEM((halo + L, bcw), F32),
                        pltpu.VMEM((halo, d_inner), F32), pltpu.VMEM((halo, bcw), F32)],
        compiler_params=_cparams(2),
        name="ssd",
    )(proj, proj, proj, proj, pad_heads(dt_bias), pad_heads(a_log), dsk,
      conv_w, conv_w, conv_b2, conv_b2, conv0p, conv0p, s0.reshape(B, d_inner, N))
    return y, s.reshape(B, heads, P, N)


def _ffn_in_blocks(w):
    depth, d, two_ff = w.shape
    tf = _largest_divisor(two_ff // 2, (512, 256, 128))
    return w.reshape(depth, d, two_ff // tf, tf).transpose(0, 2, 1, 3).astype(BF16)


def _prep_weights(w_in_a, w_in_c, d_inner):
    gates = 2 * MLSTM_HEADS
    wa_main = w_in_a[:, :, :-gates]
    wa_gate = jnp.pad(w_in_a[:, :, -gates:], ((0, 0), (0, 0), (0, LANES - gates)))
    wa = jnp.concatenate([wa_main, wa_gate], axis=-1).astype(BF16)
    heads = d_inner // SSD_HEAD_DIM
    hpg = heads // SSD_GROUPS
    wc_main = w_in_c[:, :, :-heads]
    wc_dt = w_in_c[:, :, -heads:].reshape(w_in_c.shape[0], w_in_c.shape[1], SSD_GROUPS, hpg)
    wc_dt = jnp.pad(wc_dt, ((0, 0), (0, 0), (0, 0), (0, LANES - hpg)))
    wc_dt = wc_dt.reshape(w_in_c.shape[0], w_in_c.shape[1], SSD_GROUPS * LANES)
    wc = jnp.concatenate([wc_main, wc_dt], axis=-1).astype(BF16)
    return wa, wc


def _trunk(x3, st, W):
    B, T, D = x3.shape
    fresh = st is None
    x = x3.reshape(B * T, D)
    depth = W["norm_mix"].shape[0]
    o_c, o_n, o_m, o_k, o_v, o_s, o_cv, o_f = [], [], [], [], [], [], [], []
    for i in range(depth):
        j = i // N_MIXERS
        kind = i % N_MIXERS
        if kind == 0:
            dv = D // MLSTM_HEADS
            dk = dv // 2
            if fresh:
                c0 = jnp.zeros((B, MLSTM_HEADS, dk, dv), F32)
                n0 = jnp.zeros((B, MLSTM_HEADS, dk), F32)
                m0 = jnp.zeros((B, MLSTM_HEADS), F32)
            else:
                c0, n0, m0 = st["c"][j], st["n"][j], st["m"][j]
            proj = _matmul(x, W["w_in_a"][j], gain=W["norm_mix"][i])
            bias = jnp.pad(jnp.concatenate([W["b_ig_a"][j], W["b_fg_a"][j]]), (0, LANES - 2 * MLSTM_HEADS))
            hg, c, n, m = _mlstm_mix(proj, bias.reshape(1, LANES), W["g_head_a"][j], c0, n0, m0, B, T)
            x = _matmul(hg, W["w_out_a"][j], res=x)
            o_c.append(c); o_n.append(n); o_m.append(m)
        elif kind == 1:
            hd = D // SWA_HEADS
            kvw = SWA_KV_HEADS * hd
            qw = SWA_HEADS * hd
            proj = _matmul(x, W["w_in_b"][j], gain=W["norm_mix"][i])
            p3 = proj.reshape(B, T, qw + 2 * kvw)
            k_new, v_new = p3[:, :, qw:qw + kvw], p3[:, :, qw + kvw:]
            kb, vb = qw // kvw, qw // kvw + 1
            if fresh:
                nb = T // WINDOW
                prev = lambda b, n: (b * nb + jnp.maximum(n - 1, 0), kb)
                cur = lambda b, n: (b * nb + n, kb)
                prev_v = lambda b, n: (b * nb + jnp.maximum(n - 1, 0), vb)
                cur_v = lambda b, n: (b * nb + n, vb)
                o = _swa_call(W["sinks_b"][j], proj, lambda b, n: (b * nb + n, 0), proj, prev, cur,
                              proj, prev_v, cur_v, (B, nb), WINDOW, B * T, hd, True)
                kc, vc = k_new[:, -WINDOW:], v_new[:, -WINDOW:]
            else:
                kbuf = st["k"][j].reshape(B, WINDOW, kvw)
                vbuf = st["v"][j].reshape(B, WINDOW, kvw)
                zpad = jnp.zeros((B, WINDOW - T, kvw), F32)
                kk = jnp.concatenate([kbuf, k_new, zpad], axis=1).reshape(B * 2 * WINDOW, kvw)
                vv = jnp.concatenate([vbuf, v_new, zpad], axis=1).reshape(B * 2 * WINDOW, kvw)
                o = _swa_call(W["sinks_b"][j], proj, lambda b, n: (b, 0), kk, lambda b, n: (2 * b, 0),
                              lambda b, n: (2 * b + 1, 0), vv, lambda b, n: (2 * b, 0),
                              lambda b, n: (2 * b + 1, 0), (B, 1), T, B * T, hd, False)
                kc = jnp.concatenate([kbuf[:, T:], k_new], axis=1)
                vc = jnp.concatenate([vbuf[:, T:], v_new], axis=1)
            x = _matmul(o, W["w_out_b"][j], res=x)
            o_k.append(kc.reshape(B, WINDOW, SWA_KV_HEADS, hd))
            o_v.append(vc.reshape(B, WINDOW, SWA_KV_HEADS, hd))
        else:
            d_inner = W["w_out_c"].shape[1]
            heads = d_inner // SSD_HEAD_DIM
            conv_dim = d_inner + 2 * SSD_GROUPS * SSD_STATE
            if fresh:
                s0 = jnp.zeros((B, heads, SSD_HEAD_DIM, SSD_STATE), F32)
                cb0 = jnp.zeros((B, SSD_CONV - 1, conv_dim), F32)
            else:
                s0, cb0 = st["ssm"][j], st["conv"][j]
            proj = _matmul(x, W["w_in_c"][j], gain=W["norm_mix"][i])
            y, s = _ssd_mix(proj, d_inner, W["conv_w_c"][j], W["conv_b_c"][j], W["dt_bias_c"][j],
                            W["a_log_c"][j], W["d_skip_c"][j], s0, cb0, B, T)
            x = _matmul(y, W["w_out_c"][j], gain=W["g_norm_c"][j], res=x)
            xbc = proj.reshape(B, T, -1)[:, -(SSD_CONV - 1):, d_inner:d_inner + conv_dim]
            o_s.append(s); o_cv.append(xbc)

        final_gain = W["norm_final"] if i == depth - 1 else None
        d_ff = W["w_ffn_out"].shape[1]
        if fresh:
            fb0 = jnp.zeros((B, FFN_CONV - 1, d_ff), F32)
            x, fb = _ffn(x, W["norm_ffn"][i], W["w_ffn_in"][i], W["ffn_conv_w"][i], W["ffn_conv_b"][i],
                         W["w_ffn_out"][i], fb0, 1, T, final_gain)
        else:
            xt = x.reshape(B, T, D).transpose(1, 0, 2).reshape(T * B, D)
            fb0 = st["ffn"][i].transpose(1, 0, 2).reshape(1, (FFN_CONV - 1) * B, d_ff)
            xt, fb = _ffn(xt, W["norm_ffn"][i], W["w_ffn_in"][i], W["ffn_conv_w"][i], W["ffn_conv_b"][i],
                          W["w_ffn_out"][i], fb0, B, T * B, final_gain)
            x = xt.reshape(T, B, D).transpose(1, 0, 2).reshape(B * T, D)
            fb = fb.reshape(FFN_CONV - 1, B, d_ff).transpose(1, 0, 2)
        o_f.append(fb)
    return (x.reshape(B, T, D), jnp.stack(o_c), jnp.stack(o_n), jnp.stack(o_m), jnp.stack(o_k),
            jnp.stack(o_v), jnp.stack(o_s), jnp.stack(o_cv), jnp.stack(o_f))


def kernel(x_prompt, x_sample, state_mlstm_c, state_mlstm_n, state_mlstm_m, cache_swa_k, cache_swa_v, state_ssm, state_ssm_conv, state_ffn_conv, norm_mix, norm_ffn, norm_final, w_in_a, b_ig_a, b_fg_a, g_head_a, w_out_a, w_in_b, sinks_b, w_out_b, w_in_c, conv_w_c, conv_b_c, dt_bias_c, a_log_c, d_skip_c, g_norm_c, w_out_c, w_ffn_in, ffn_conv_w, ffn_conv_b, w_ffn_out):
    wa, wc = _prep_weights(w_in_a, w_in_c, w_out_c.shape[1])
    W = dict(norm_mix=norm_mix, norm_ffn=norm_ffn, norm_final=norm_final,
             w_in_a=wa, b_ig_a=b_ig_a, b_fg_a=b_fg_a, g_head_a=g_head_a, w_out_a=w_out_a.astype(BF16),
             w_in_b=w_in_b.astype(BF16), sinks_b=sinks_b, w_out_b=w_out_b.astype(BF16),
             w_in_c=wc, conv_w_c=conv_w_c, conv_b_c=conv_b_c, dt_bias_c=dt_bias_c, a_log_c=a_log_c,
             d_skip_c=d_skip_c, g_norm_c=g_norm_c, w_out_c=w_out_c.astype(BF16),
             w_ffn_in=_ffn_in_blocks(w_ffn_in), ffn_conv_w=ffn_conv_w, ffn_conv_b=ffn_conv_b,
             w_ffn_out=w_ffn_out.astype(BF16))
    st = dict(c=state_mlstm_c, n=state_mlstm_n, m=state_mlstm_m, k=cache_swa_k, v=cache_swa_v,
              ssm=state_ssm, conv=state_ssm_conv, ffn=state_ffn_conv)
    yp, pc, pn, pm, pk, pv, ps, pcv, pf = _trunk(x_prompt, None, W)
    ys, sc, sn, sm, sk, sv, ss, scv, sf = _trunk(x_sample, st, W)
    return (yp, ys, pc, sc, pn, sn, pm, sm, pk, sk, pv, sv, ps, ss, pcv, scv, pf, sf)
```

```python
import functools
import math

import jax
import jax.numpy as jnp
from jax import lax
from jax.experimental import pallas as pl
from jax.experimental.pallas import tpu as pltpu

F32 = jnp.float32
BF16 = jnp.bfloat16

RMS_EPS = 1e-6
N_MIXERS = 3

MLSTM_HEADS = 8
SWA_HEADS = 32
SWA_KV_HEADS = 4
SWA_GROUP = SWA_HEADS // SWA_KV_HEADS
WINDOW = 128
SSD_HEAD_DIM = 64
SSD_GROUPS = 8
SSD_STATE = 128
SSD_CONV = 4
FFN_CONV = 3

LANES = 128
SUBLANES = 8
VMEM_LIMIT_BYTES = 56 << 20
VMEM_TILE_BUDGET = 40 << 20

FFN_ROW_TILE = 1024
FFN_COL_TILE = 512
FFN_SUB_TILE = 256
MLSTM_CHUNK = 128
SSD_CHUNK = 128


def _cparams(n_axes):
    return pltpu.CompilerParams(dimension_semantics=("arbitrary",) * n_axes,
                                vmem_limit_bytes=VMEM_LIMIT_BYTES)


def _round_up(a, b):
    return (a + b - 1) // b * b


def _largest_divisor(n, candidates):
    for c in candidates:
        if n % c == 0:
            return c
    raise ValueError(f"no tile in {candidates} divides {n}")


def _rmsnorm(x, g):
    return x * lax.rsqrt(jnp.mean(x * x, axis=-1, keepdims=True) + RMS_EPS) * g


def _silu(x):
    return x * jax.nn.sigmoid(x)


def _mm_body(*refs, norm, residual):
    it = iter(refs)
    x_ref = next(it)
    g_ref = next(it) if norm else None
    w_ref = next(it)
    r_ref = next(it) if residual else None
    o_ref = next(it)
    xn_ref = next(it) if norm else None
    if norm:
        @pl.when(pl.program_id(1) == 0)
        def _():
            xn_ref[...] = _rmsnorm(x_ref[...], g_ref[...]).astype(BF16)
        lhs = xn_ref[...]
    else:
        lhs = x_ref[...].astype(BF16)
    acc = jnp.dot(lhs, w_ref[...], preferred_element_type=F32)
    if residual:
        acc = acc + r_ref[...]
    o_ref[...] = acc.astype(o_ref.dtype)


def _mm_tiles(m, k, n, x_bytes, norm, residual):
    for tm in (1024, 512, 256, 128, 64, 32, 16, 8):
        if m % tm:
            continue
        for tn in (1024, 896, 512, 256, 128):
            if n % tn:
                continue
            need = 2 * tm * k * x_bytes + 2 * k * tn * 2 + 2 * tm * tn * 4
            need += tm * k * 2 if norm else 0
            need += 2 * tm * tn * 4 if residual else 0
            if need <= VMEM_TILE_BUDGET:
                return tm, tn
    raise ValueError("no matmul tile fits VMEM")


def _matmul(x, w, gain=None, res=None):
    m, k = x.shape
    n = w.shape[1]
    norm, residual = gain is not None, res is not None
    tm, tn = _mm_tiles(m, k, n, x.dtype.itemsize, norm, residual)
    in_specs = [pl.BlockSpec((tm, k), lambda i, j: (i, 0))]
    args = [x]
    if norm:
        in_specs.append(pl.BlockSpec((1, k), lambda i, j: (0, 0)))
        args.append(gain.reshape(1, k))
    in_specs.append(pl.BlockSpec((k, tn), lambda i, j: (0, j)))
    args.append(w)
    if residual:
        in_specs.append(pl.BlockSpec((tm, tn), lambda i, j: (i, j)))
        args.append(res)
    return pl.pallas_call(
        functools.partial(_mm_body, norm=norm, residual=residual),
        grid=(m // tm, n // tn),
        in_specs=in_specs,
        out_specs=pl.BlockSpec((tm, tn), lambda i, j: (i, j)),
        out_shape=jax.ShapeDtypeStruct((m, n), F32),
        scratch_shapes=[pltpu.VMEM((tm, k), BF16)] if norm else [],
        compiler_params=_cparams(2),
        name="matmul",
    )(*args)


def _ffn_body(x_ref, gn_ref, wu_ref, wg_ref, cw_ref, cb_ref, wo_ref, buf0_ref, gf_ref,
              o_ref, bufo_ref, xn_ref, gs_ref, carry_ref,
              *, tm, shift, halo, tiles_per_seq, nj, final_norm):
    i = pl.program_id(0)
    j = pl.program_id(1)

    @pl.when(j == 0)
    def _():
        x = x_ref[...]
        xn_ref[...] = _rmsnorm(x, gn_ref[...]).astype(BF16)
        o_ref[...] = x

    xn = xn_ref[...]
    first = (i % tiles_per_seq) == 0
    gs_ref[0:halo, :] = jnp.where(first, buf0_ref[0], carry_ref[j])
    base = halo - (FFN_CONV - 1) * shift
    tf = wu_ref.shape[1]
    sub = min(tf, FFN_SUB_TILE)
    h_parts = []
    for s0 in range(0, tf, sub):
        cols = slice(s0, s0 + sub)
        u = jnp.dot(xn, wu_ref[:, cols], preferred_element_type=F32)
        gs_ref[halo:halo + tm, cols] = jnp.dot(xn, wg_ref[:, cols], preferred_element_type=F32)
        c = gs_ref[base:base + tm, cols] * cw_ref[0:1, cols]
        for t in range(1, FFN_CONV):
            c = c + gs_ref[base + t * shift:base + t * shift + tm, cols] * cw_ref[t:t + 1, cols]
        c = c + cb_ref[:, cols]
        h_parts.append((_silu(c) * u).astype(BF16))
    tail = gs_ref[tm:tm + halo, :]
    carry_ref[j] = tail
    bufo_ref[0] = tail
    h = h_parts[0] if len(h_parts) == 1 else jnp.concatenate(h_parts, axis=1)
    o_ref[...] += jnp.dot(h, wo_ref[...], preferred_element_type=F32)

    if final_norm:
        @pl.when(j == nj - 1)
        def _():
            o_ref[...] = _rmsnorm(o_ref[...], gf_ref[...])


def _ffn(x, gain, w_in, conv_w, conv_b, w_out, buf0, shift, rows_per_seq, final_gain):
    m, d = x.shape
    d_ff = w_out.shape[0]
    tm = _largest_divisor(rows_per_seq, (FFN_ROW_TILE, 512, 256, 128, 64, 32, 16, 8))
    tf = _largest_divisor(d_ff, (FFN_COL_TILE, 256, 128))
    once = pl.Buffered(1)
    nj = d_ff // tf
    tiles_per_seq = rows_per_seq // tm
    pre = (FFN_CONV - 1) * shift
    halo = _round_up(pre, SUBLANES)
    assert tm >= halo and tm % SUBLANES == 0
    buf0p = jnp.pad(buf0, ((0, 0), (halo - pre, 0), (0, 0)))
    final_norm = final_gain is not None
    gf = (final_gain if final_norm else gain).reshape(1, d)
    body = functools.partial(_ffn_body, tm=tm, shift=shift, halo=halo, tiles_per_seq=tiles_per_seq,
                             nj=nj, final_norm=final_norm)
    out, bufo = pl.pallas_call(
        body,
        grid=(m // tm, nj),
        in_specs=[
            pl.BlockSpec((tm, d), lambda i, j: (i, 0)),
            pl.BlockSpec((1, d), lambda i, j: (0, 0)),
            pl.BlockSpec((d, tf), lambda i, j: (0, j)),
            pl.BlockSpec((d, tf), lambda i, j: (0, nj + j)),
            pl.BlockSpec((FFN_CONV, tf), lambda i, j: (0, j)),
            pl.BlockSpec((1, tf), lambda i, j: (0, j)),
            pl.BlockSpec((tf, d), lambda i, j: (j, 0)),
            pl.BlockSpec((1, halo, tf), lambda i, j: (i // tiles_per_seq, 0, j)),
            pl.BlockSpec((1, d), lambda i, j: (0, 0)),
        ],
        out_specs=[
            pl.BlockSpec((tm, d), lambda i, j: (i, 0), pipeline_mode=once),
            pl.BlockSpec((1, halo, tf), lambda i, j: (i, 0, j)),
        ],
        out_shape=[jax.ShapeDtypeStruct((m, d), F32),
                   jax.ShapeDtypeStruct((m // tm, halo, d_ff), F32)],
        scratch_shapes=[pltpu.VMEM((tm, d), BF16),
                        pltpu.VMEM((halo + tm, tf), F32),
                        pltpu.VMEM((nj, halo, tf), F32)],
        compiler_params=_cparams(2),
        name="conv_ffn",
    )(x, gain.reshape(1, d), w_in, w_in, conv_w, conv_b.reshape(1, d_ff), w_out, buf0p, gf)
    return out, bufo[tiles_per_seq - 1::tiles_per_seq, halo - pre:, :]


def _row_from_col(col, eye):
    return jnp.sum(jnp.where(eye, col, 0.0), axis=0, keepdims=True)


def _cumsum_rows(x, tril):
    return jnp.dot(tril, x, preferred_element_type=F32, precision=lax.Precision.HIGHEST)


def _mlstm_body(q_ref, k_ref, v_ref, o_ref, gate_ref, bias_ref, gh_ref, c0_ref, n0_ref, m0_ref,
                hg_ref, c_ref, n_ref, m_ref, *, chunk, dk, dv):
    L = chunk
    H = MLSTM_HEADS

    @pl.when(pl.program_id(1) == 0)
    def _():
        c_ref[...] = c0_ref[...]
        n_ref[...] = n0_ref[...]
        m_ref[...] = m0_ref[...]

    ri = lax.broadcasted_iota(jnp.int32, (L, L), 0)
    ci = lax.broadcasted_iota(jnp.int32, (L, L), 1)
    causal = ri >= ci
    eye = ri == ci
    tril = causal.astype(F32)

    gates = gate_ref[...] + bias_ref[...]
    x = gates
    log_sig = jnp.minimum(x, 0.0) - jnp.log1p(jnp.exp(-jnp.abs(x)))
    bcum = _cumsum_rows(log_sig, tril)

    n_all = n_ref[0]
    m_all = m_ref[0]
    if L % LANES == 0:
        gates_t, bcum_t = gates.T, bcum.T
        row_of = lambda arr_t, col, lane: arr_t[lane:lane + 1, :]
    else:
        gates_t = bcum_t = None
        row_of = lambda arr_t, col, lane: _row_from_col(col, eye)

    stack = lambda parts: jnp.concatenate(parts, axis=0)
    ig_cols = [gates[:, h:h + 1] for h in range(H)]
    b_cols = [bcum[:, H + h:H + h + 1] for h in range(H)]
    m_prevs = [m_all[h:h + 1, 0:1] for h in range(H)]
    ig_col, b_col = stack(ig_cols), stack(b_cols)
    m_prev = stack([jnp.broadcast_to(mp, (L, 1)) for mp in m_prevs])
    ig_row = stack([jnp.broadcast_to(row_of(gates_t, ig_cols[h], h), (L, L)) for h in range(H)])
    b_row = stack([jnp.broadcast_to(row_of(bcum_t, b_cols[h], H + h), (L, L)) for h in range(H)])
    causal_all = stack([causal] * H)

    log_d = jnp.where(causal_all, b_col - b_row + ig_row, -jnp.inf)
    log_inter = b_col + m_prev
    m_t = jnp.maximum(log_inter, jnp.max(log_d, axis=-1, keepdims=True))
    d = jnp.exp(log_d - m_t)
    inter = jnp.exp(log_inter - m_t)

    qk, qc, qn, kfs, vbs, c_prevs = [], [], [], [], [], []
    for h in range(H):
        qf = q_ref[:, h * dk:(h + 1) * dk] * (dk ** -0.5)
        kf = k_ref[:, h * dk:(h + 1) * dk]
        qb = qf.astype(BF16)
        c_prev = c_ref[0, h]
        qk.append(lax.dot_general(qb, kf.astype(BF16), (((1,), (1,)), ((), ())), preferred_element_type=F32))
        qc.append(jnp.dot(qb, c_prev.astype(BF16), preferred_element_type=F32))
        qn.append(jnp.sum(qf * n_all[h:h + 1, :], axis=-1, keepdims=True))
        kfs.append(kf)
        vbs.append(v_ref[:, h * dv:(h + 1) * dv].astype(BF16))
        c_prevs.append(c_prev)

    s = stack(qk) * d
    sb = s.astype(BF16)
    sv = stack([jnp.dot(sb[h * L:(h + 1) * L], vbs[h], preferred_element_type=F32) for h in range(H)])
    num = inter * stack(qc) + sv
    den = inter * stack(qn) + jnp.sum(s, axis=-1, keepdims=True)
    hh = num / jnp.maximum(jnp.abs(den), jnp.exp(-m_t))
    hn = hh * lax.rsqrt(jnp.mean(hh * hh, axis=-1, keepdims=True) + RMS_EPS)
    for h in range(H):
        lanes = slice(h * dv, (h + 1) * dv)
        out = jax.nn.sigmoid(o_ref[:, lanes]) * (hn[h * L:(h + 1) * L] * gh_ref[:, lanes])
        hg_ref[:, lanes] = out.astype(hg_ref.dtype)

    b_lasts = [bc[L - 1:L, :] for bc in b_cols]
    log_w = stack([jnp.broadcast_to(bl, (L, 1)) for bl in b_lasts]) - b_col + ig_col
    m_news = [jnp.maximum(b_lasts[h] + m_prevs[h], jnp.max(log_w[h * L:(h + 1) * L], axis=0, keepdims=True))
              for h in range(H)]
    w = jnp.exp(log_w - stack([jnp.broadcast_to(mn, (L, 1)) for mn in m_news]))

    head_row = lax.broadcasted_iota(jnp.int32, (H, LANES), 0)
    n_new, m_new_all = n_all, m_all
    for h in range(H):
        decay = jnp.exp(b_lasts[h] + m_prevs[h] - m_news[h])
        kw = kfs[h] * w[h * L:(h + 1) * L]
        upd = lax.dot_general(kw.astype(BF16), vbs[h], (((0,), (0,)), ((), ())), preferred_element_type=F32)
        c_ref[0, h] = decay * c_prevs[h] + upd
        n_row = decay * n_all[h:h + 1, :] + jnp.sum(kw, axis=0, keepdims=True)
        n_new = jnp.where(head_row == h, n_row, n_new)
        m_new_all = jnp.where(head_row == h, m_news[h], m_new_all)
    n_ref[0] = n_new
    m_ref[0] = m_new_all


def _mlstm_mix(proj, bias, g_head, c0, n0, m0, B, T):
    H = MLSTM_HEADS
    dk, dv = c0.shape[2], c0.shape[3]
    L = math.gcd(T, MLSTM_CHUNK)
    nc = T // L
    qw, vw = H * dk, H * dv
    assert qw * 2 == vw and vw % LANES == 0
    gate_blk = (2 * qw + 2 * vw) // LANES
    hg_dtype = BF16 if L % 16 == 0 else F32
    m0b = jnp.broadcast_to(m0[:, :, None], (B, H, LANES))
    row = lambda b, c: b * nc + c
    hg, c, n, m = pl.pallas_call(
        functools.partial(_mlstm_body, chunk=L, dk=dk, dv=dv),
        grid=(B, nc),
        in_specs=[
            pl.BlockSpec((L, qw), lambda b, c: (row(b, c), 0)),
            pl.BlockSpec((L, qw), lambda b, c: (row(b, c), 1)),
            pl.BlockSpec((L, vw), lambda b, c: (row(b, c), 1)),
            pl.BlockSpec((L, vw), lambda b, c: (row(b, c), 2)),
            pl.BlockSpec((L, LANES), lambda b, c: (row(b, c), gate_blk)),
            pl.BlockSpec((1, LANES), lambda b, c: (0, 0)),
            pl.BlockSpec((1, vw), lambda b, c: (0, 0)),
            pl.BlockSpec((1, H, dk, dv), lambda b, c: (b, 0, 0, 0)),
            pl.BlockSpec((1, H, dk), lambda b, c: (b, 0, 0)),
            pl.BlockSpec((1, H, LANES), lambda b, c: (b, 0, 0)),
        ],
        out_specs=[
            pl.BlockSpec((L, vw), lambda b, c: (row(b, c), 0)),
            pl.BlockSpec((1, H, dk, dv), lambda b, c: (b, 0, 0, 0)),
            pl.BlockSpec((1, H, dk), lambda b, c: (b, 0, 0)),
            pl.BlockSpec((1, H, LANES), lambda b, c: (b, 0, 0)),
        ],
        out_shape=[jax.ShapeDtypeStruct((B * T, vw), hg_dtype),
                   jax.ShapeDtypeStruct((B, H, dk, dv), F32),
                   jax.ShapeDtypeStruct((B, H, dk), F32),
                   jax.ShapeDtypeStruct((B, H, LANES), F32)],
        compiler_params=_cparams(2),
        name="mlstm",
    )(proj, proj, proj, proj, proj, bias, g_head.reshape(1, vw), c0, n0, m0b)
    return hg, c, n, m[:, :, 0]


def _swa_body(sink_ref, q_ref, kp_ref, kc_ref, vp_ref, vc_ref, o_ref, *, tq, hd, mask_first):
    W = WINDOW
    G = SWA_GROUP
    S = 2 * W
    kk = jnp.concatenate([kp_ref[...], kc_ref[...]], axis=0)
    vv = jnp.concatenate([vp_ref[...], vc_ref[...]], axis=0)
    qi = lax.broadcasted_iota(jnp.int32, (tq, S), 0)
    kj = lax.broadcasted_iota(jnp.int32, (tq, S), 1)
    dist = W + qi - kj
    valid = (dist >= 0) & (dist <= W)
    if mask_first:
        valid = valid & ((kj >= W) | (pl.program_id(1) > 0))
    distf = dist.astype(F32)
    lane_s = lax.broadcasted_iota(jnp.int32, (S, LANES), 1)
    lane_q = lax.broadcasted_iota(jnp.int32, (tq, LANES), 1)
    low_s = lane_s < hd
    low_q = lane_q < hd
    heads_per_slab = LANES // hd
    assert heads_per_slab == 2

    for kh in range(SWA_KV_HEADS):
        slab = (kh // heads_per_slab) * LANES
        keep_low = (kh % heads_per_slab) == 0
        k128 = kk[:, slab:slab + LANES]
        v128 = vv[:, slab:slab + LANES]
        k_rot = pltpu.roll(k128, hd, axis=1)
        v_rot = pltpu.roll(v128, hd, axis=1)
        own = low_s if keep_low else jnp.logical_not(low_s)
        k2 = jnp.where(own, k128, k_rot).astype(BF16)
        v2 = jnp.where(own, v128, v_rot).astype(BF16)

        q_parts = []
        for g in range(G):
            h = kh * G + g
            qs = q_ref[:, (h // 2) * LANES:(h // 2 + 1) * LANES]
            keep = low_q if h % 2 == 0 else jnp.logical_not(low_q)
            q_parts.append(jnp.where(keep, qs, 0.0))
        q_stack = jnp.concatenate(q_parts, axis=0).astype(BF16)
        s_all = lax.dot_general(q_stack, k2, (((1,), (1,)), ((), ())), preferred_element_type=F32)
        s_all = s_all * (hd ** -0.5)

        p_parts = []
        for g in range(G):
            h = kh * G + g
            slope = 2.0 ** (-8.0 * (h + 1) / SWA_HEADS)
            sink = sink_ref[h]
            s = s_all[g * tq:(g + 1) * tq, :] - slope * distf
            s = jnp.where(valid, s, -jnp.inf)
            mx = jnp.maximum(jnp.max(s, axis=-1, keepdims=True), sink)
            p = jnp.exp(s - mx)
            p = p / (jnp.sum(p, axis=-1, keepdims=True) + jnp.exp(sink - mx))
            p_parts.append(p)
        p_stack = jnp.concatenate(p_parts, axis=0).astype(BF16)
        o_all = jnp.dot(p_stack, v2, preferred_element_type=F32)

        for g2 in range(G // 2):
            h = kh * G + 2 * g2
            o_even = o_all[(2 * g2) * tq:(2 * g2 + 1) * tq, :]
            o_odd = o_all[(2 * g2 + 1) * tq:(2 * g2 + 2) * tq, :]
            o_ref[:, (h // 2) * LANES:(h // 2 + 1) * LANES] = jnp.where(low_q, o_even, o_odd).astype(o_ref.dtype)


def _swa_call(sinks, q_arr, q_blk, k_arr, kp_map, kc_map, v_arr, vp_map, vc_map, grid, tq, n_rows, hd,
              mask_first):
    kvw = SWA_KV_HEADS * hd
    qw = SWA_HEADS * hd
    out_dtype = BF16 if tq % 16 == 0 else F32
    return pl.pallas_call(
        functools.partial(_swa_body, tq=tq, hd=hd, mask_first=mask_first),
        grid=grid,
        in_specs=[
            pl.BlockSpec(memory_space=pltpu.SMEM),
            pl.BlockSpec((tq, qw), q_blk),
            pl.BlockSpec((WINDOW, kvw), kp_map),
            pl.BlockSpec((WINDOW, kvw), kc_map),
            pl.BlockSpec((WINDOW, kvw), vp_map),
            pl.BlockSpec((WINDOW, kvw), vc_map),
        ],
        out_specs=pl.BlockSpec((tq, qw), q_blk),
        out_shape=jax.ShapeDtypeStruct((n_rows, qw), out_dtype),
        compiler_params=_cparams(2),
        name="swa",
    )(sinks, q_arr, k_arr, k_arr, v_arr, v_arr)


def _conv_silu(u_ref, init_ref, carry_ref, gs_ref, cw_ref, cb_ref, first, L):
    halo = SUBLANES
    gs_ref[0:halo, :] = jnp.where(first, init_ref[0], carry_ref[...])
    gs_ref[halo:halo + L, :] = u_ref[...]
    carry_ref[...] = gs_ref[L:L + halo, :]
    base = halo - (SSD_CONV - 1)
    cw = cw_ref[...]
    c = gs_ref[base:base + L, :] * cw[0:1, :]
    for t in range(1, SSD_CONV):
        c = c + gs_ref[base + t:base + t + L, :] * cw[t:t + 1, :]
    c = c + cb_ref[...]
    return _silu(c)


def _split3_bf16(x):
    hi = x.astype(BF16)
    r = x - hi.astype(F32)
    mid = r.astype(BF16)
    lo = (r - mid.astype(F32)).astype(BF16)
    return hi, mid, lo


def _ssd_body(z_ref, xs_ref, bc_ref, dt_ref, dtb_ref, alog_ref, dsk_ref, gn_ref,
              cwx_ref, cwbc_ref, cbx_ref, cbbc_ref, ix_ref, ibc_ref, s0_ref,
              y_ref, s_ref,
              gx_ref, gbc_ref, kx_ref, kbc_ref, *, chunk, groups, hpg, p_dim):
    L = chunk
    N = SSD_STATE
    gw = hpg * p_dim
    first = pl.program_id(1) == 0

    @pl.when(first)
    def _():
        s_ref[...] = s0_ref[...]

    xs_all = _conv_silu(xs_ref, ix_ref, kx_ref, gx_ref, cwx_ref, cbx_ref, first, L)
    bc_all = _conv_silu(bc_ref, ibc_ref, kbc_ref, gbc_ref, cwbc_ref, cbbc_ref, first, L)

    ri = lax.broadcasted_iota(jnp.int32, (L, L), 0)
    ci = lax.broadcasted_iota(jnp.int32, (L, L), 1)
    causal = ri >= ci
    eye = ri == ci
    tril = causal.astype(F32)

    dt_all = jax.nn.softplus(dt_ref[...] + dtb_ref[...])
    acum_all = _cumsum_rows(dt_all * (-jnp.exp(alog_ref[...])), tril)
    a_last_all = acum_all[L - 1:L, :]
    eacum_all = jnp.exp(acum_all)
    dtw_all = dt_all * jnp.exp(a_last_all - acum_all)
    ea_last_all = jnp.exp(a_last_all)

    er = lax.broadcasted_iota(jnp.int32, (3 * LANES, gw), 0)
    ec = lax.broadcasted_iota(jnp.int32, (3 * LANES, gw), 1)
    assert p_dim & (p_dim - 1) == 0
    head_of_lane = lax.shift_right_logical(ec, p_dim.bit_length() - 1)
    expand3 = (jnp.bitwise_and(er, LANES - 1) == head_of_lane).astype(BF16)

    lane = lax.broadcasted_iota(jnp.int32, (L, LANES), 1)
    low = lane < p_dim
    srow = lax.broadcasted_iota(jnp.int32, (LANES, 1), 0)
    heads_per_slab = LANES // p_dim
    assert heads_per_slab == 2

    gated = []
    for g in range(groups):
        hs = slice(g * LANES, (g + 1) * LANES)
        acum = acum_all[:, hs]
        ea_last = ea_last_all[:, hs]
        per_head = jnp.concatenate([dt_all[:, hs], eacum_all[:, hs], dtw_all[:, hs]], axis=0)
        ex = jnp.dot(jnp.concatenate(_split3_bf16(per_head), axis=1), expand3,
                     preferred_element_type=F32)
        dt_e, ea_e, dtw_e = ex[0:L], ex[L:2 * L], ex[2 * L:3 * L]

        xs = xs_all[:, g * gw:(g + 1) * gw]
        bmb = bc_all[:, g * N:(g + 1) * N].astype(BF16)
        cmb = bc_all[:, (groups + g) * N:(groups + g + 1) * N].astype(BF16)
        cb = lax.dot_general(cmb, bmb, (((1,), (1,)), ((), ())), preferred_element_type=F32)
        xdt = (xs * dt_e).astype(BF16)
        xw = (xs * dtw_e).astype(BF16)
        s_prev = s_ref[0, g * gw:(g + 1) * gw, :]
        y_inter = lax.dot_general(cmb, s_prev.astype(BF16), (((1,), (1,)), ((), ())),
                                  preferred_element_type=F32)
        upd = lax.dot_general(xw, bmb, (((0,), (0,)), ((), ())), preferred_element_type=F32)
        acum_t = acum.T if L % LANES == 0 else None

        y_parts = []
        for pr in range(hpg // heads_per_slab):
            h0 = 2 * pr
            ms = []
            for hh in (h0, h0 + 1):
                a_col = acum[:, hh:hh + 1]
                a_row = acum_t[hh:hh + 1, :] if acum_t is not None else _row_from_col(a_col, eye)
                dec = jnp.exp(jnp.where(causal, a_col - a_row, -jnp.inf))
                ms.append((cb * dec).astype(BF16))
            mcat = jnp.concatenate(ms, axis=1)
            xb = xdt[:, pr * LANES:(pr + 1) * LANES]
            zero = jnp.zeros_like(xb)
            bd = jnp.concatenate([jnp.where(low, xb, zero), jnp.where(low, zero, xb)], axis=0)
            y_parts.append(jnp.dot(mcat, bd, preferred_element_type=F32))

            scale = jnp.where(srow < p_dim, ea_last[:, h0:h0 + 1], ea_last[:, h0 + 1:h0 + 2])
            rows = slice(g * gw + pr * LANES, g * gw + (pr + 1) * LANES)
            s_ref[0, rows, :] = s_prev[pr * LANES:(pr + 1) * LANES] * scale + upd[pr * LANES:(pr + 1) * LANES]

        y = jnp.concatenate(y_parts, axis=1) + y_inter * ea_e
        y = y + dsk_ref[:, g * gw:(g + 1) * gw] * xs
        gated.append(y * _silu(z_ref[:, g * gw:(g + 1) * gw]))

    sq = gated[0] * gated[0]
    for y in gated[1:]:
        sq = sq + y * y
    inv = lax.rsqrt(jnp.sum(sq, axis=-1, keepdims=True) * (1.0 / (groups * gw)) + RMS_EPS)
    for g, y in enumerate(gated):
        y_ref[:, g * gw:(g + 1) * gw] = (y * inv * gn_ref[:, g * gw:(g + 1) * gw]).astype(y_ref.dtype)


def _ssd_mix(proj, d_inner, conv_w, conv_b, dt_bias, a_log, d_skip, g_norm, s0, conv0, B, T):
    G, P, N = SSD_GROUPS, SSD_HEAD_DIM, SSD_STATE
    heads = d_inner // P
    hpg = heads // G
    L = math.gcd(T, SSD_CHUNK)
    nc = T // L
    bcw = 2 * G * N
    dtw = G * LANES
    conv_dim = d_inner + bcw
    assert d_inner % bcw == 0 and (d_inner + conv_dim) % dtw == 0
    bc_blk = (2 * d_inner) // bcw
    dt_blk = (d_inner + conv_dim) // dtw
    pad_heads = lambda v: jnp.pad(v.reshape(G, hpg), ((0, 0), (0, LANES - hpg))).reshape(1, dtw)
    halo = SUBLANES
    conv0p = jnp.pad(conv0, ((0, 0), (halo - (SSD_CONV - 1), 0), (0, 0)))
    conv_b2 = conv_b.reshape(1, conv_dim)
    dsk = jnp.repeat(d_skip, P).reshape(1, d_inner)
    row = lambda b, c: b * nc + c
    const = lambda b, c: (0, 0)
    y, s = pl.pallas_call(
        functools.partial(_ssd_body, chunk=L, groups=G, hpg=hpg, p_dim=P),
        grid=(B, nc),
        in_specs=[
            pl.BlockSpec((L, d_inner), lambda b, c: (row(b, c), 0)),
            pl.BlockSpec((L, d_inner), lambda b, c: (row(b, c), 1)),
            pl.BlockSpec((L, bcw), lambda b, c: (row(b, c), bc_blk)),
            pl.BlockSpec((L, dtw), lambda b, c: (row(b, c), dt_blk)),
            pl.BlockSpec((1, dtw), const),
            pl.BlockSpec((1, dtw), const),
            pl.BlockSpec((1, d_inner), const),
            pl.BlockSpec((1, d_inner), const),
            pl.BlockSpec((SSD_CONV, d_inner), const),
            pl.BlockSpec((SSD_CONV, bcw), lambda b, c: (0, d_inner // bcw)),
            pl.BlockSpec((1, d_inner), const),
            pl.BlockSpec((1, bcw), lambda b, c: (0, d_inner // bcw)),
            pl.BlockSpec((1, halo, d_inner), lambda b, c: (b, 0, 0)),
            pl.BlockSpec((1, halo, bcw), lambda b, c: (b, 0, d_inner // bcw)),
            pl.BlockSpec((1, d_inner, N), lambda b, c: (b, 0, 0)),
        ],
        out_specs=[
            pl.BlockSpec((L, d_inner), lambda b, c: (row(b, c), 0)),
            pl.BlockSpec((1, d_inner, N), lambda b, c: (b, 0, 0)),
        ],
        out_shape=[jax.ShapeDtypeStruct((B * T, d_inner), BF16 if L % 16 == 0 else F32),
                   jax.ShapeDtypeStruct((B, d_inner, N), F32)],
        scratch_shapes=[pltpu.VMEM((halo + L, d_inner), F32), pltpu.VMEM((halo + L, bcw), F32),
                        pltpu.VMEM((halo, d_inner), F32), pltpu.VMEM((halo, bcw), F32)],
        compiler_params=_cparams(2),
        name="ssd",
    )(proj, proj, proj, proj, pad_heads(dt_bias), pad_heads(a_log), dsk, g_norm.reshape(1, d_inner),
      conv_w, conv_w, conv_b2, conv_b2, conv0p, conv0p, s0.reshape(B, d_inner, N))
    return y, s.reshape(B, heads, P, N)


def _per_layer_bf16(w):
    return [w[i].astype(BF16) for i in range(w.shape[0])]


def _mlstm_in_weight(w):
    gates = 2 * MLSTM_HEADS
    w_gate = jnp.pad(w[:, -gates:], ((0, 0), (0, LANES - gates)))
    return jnp.concatenate([w[:, :-gates], w_gate], axis=-1).astype(BF16)


def _ssd_in_weight(w, d_inner):
    heads = d_inner // SSD_HEAD_DIM
    hpg = heads // SSD_GROUPS
    w_dt = w[:, -heads:].reshape(w.shape[0], SSD_GROUPS, hpg)
    w_dt = jnp.pad(w_dt, ((0, 0), (0, 0), (0, LANES - hpg))).reshape(w.shape[0], SSD_GROUPS * LANES)
    return jnp.concatenate([w[:, :-heads], w_dt], axis=-1).astype(BF16)


def _trunk(x3, st, W):
    B, T, D = x3.shape
    fresh = st is None
    x = x3.reshape(B * T, D)
    depth = W["norm_mix"].shape[0]
    o_c, o_n, o_m, o_k, o_v, o_s, o_cv, o_f = [], [], [], [], [], [], [], []
    for i in range(depth):
        j = i // N_MIXERS
        kind = i % N_MIXERS
        if kind == 0:
            dv = D // MLSTM_HEADS
            dk = dv // 2
            if fresh:
                c0 = jnp.zeros((B, MLSTM_HEADS, dk, dv), F32)
                n0 = jnp.zeros((B, MLSTM_HEADS, dk), F32)
                m0 = jnp.zeros((B, MLSTM_HEADS), F32)
            else:
                c0, n0, m0 = st["c"][j], st["n"][j], st["m"][j]
            proj = _matmul(x, W["w_in_a"][j], gain=W["norm_mix"][i])
            bias = jnp.pad(jnp.concatenate([W["b_ig_a"][j], W["b_fg_a"][j]]), (0, LANES - 2 * MLSTM_HEADS))
            hg, c, n, m = _mlstm_mix(proj, bias.reshape(1, LANES), W["g_head_a"][j], c0, n0, m0, B, T)
            x = _matmul(hg, W["w_out_a"][j], res=x)
            o_c.append(c); o_n.append(n); o_m.append(m)
        elif kind == 1:
            hd = D // SWA_HEADS
            kvw = SWA_KV_HEADS * hd
            qw = SWA_HEADS * hd
            proj = _matmul(x, W["w_in_b"][j], gain=W["norm_mix"][i])
            p3 = proj.reshape(B, T, qw + 2 * kvw)
            k_new, v_new = p3[:, :, qw:qw + kvw], p3[:, :, qw + kvw:]
            kb, vb = qw // kvw, qw // kvw + 1
            if fresh:
                nb = T // WINDOW
                prev = lambda b, n: (b * nb + jnp.maximum(n - 1, 0), kb)
                cur = lambda b, n: (b * nb + n, kb)
                prev_v = lambda b, n: (b * nb + jnp.maximum(n - 1, 0), vb)
                cur_v = lambda b, n: (b * nb + n, vb)
                o = _swa_call(W["sinks_b"][j], proj, lambda b, n: (b * nb + n, 0), proj, prev, cur,
                              proj, prev_v, cur_v, (B, nb), WINDOW, B * T, hd, True)
                kc, vc = k_new[:, -WINDOW:], v_new[:, -WINDOW:]
            else:
                kbuf = st["k"][j].reshape(B, WINDOW, kvw)
                vbuf = st["v"][j].reshape(B, WINDOW, kvw)
                zpad = jnp.zeros((B, WINDOW - T, kvw), F32)
                kk = jnp.concatenate([kbuf, k_new, zpad], axis=1).reshape(B * 2 * WINDOW, kvw)
                vv = jnp.concatenate([vbuf, v_new, zpad], axis=1).reshape(B * 2 * WINDOW, kvw)
                o = _swa_call(W["sinks_b"][j], proj, lambda b, n: (b, 0), kk, lambda b, n: (2 * b, 0),
                              lambda b, n: (2 * b + 1, 0), vv, lambda b, n: (2 * b, 0),
                              lambda b, n: (2 * b + 1, 0), (B, 1), T, B * T, hd, False)
                kc = jnp.concatenate([kbuf[:, T:], k_new], axis=1)
                vc = jnp.concatenate([vbuf[:, T:], v_new], axis=1)
            x = _matmul(o, W["w_out_b"][j], res=x)
            o_k.append(kc.reshape(B, WINDOW, SWA_KV_HEADS, hd))
            o_v.append(vc.reshape(B, WINDOW, SWA_KV_HEADS, hd))
        else:
            d_inner = W["w_out_c"][j].shape[0]
            heads = d_inner // SSD_HEAD_DIM
            conv_dim = d_inner + 2 * SSD_GROUPS * SSD_STATE
            if fresh:
                s0 = jnp.zeros((B, heads, SSD_HEAD_DIM, SSD_STATE), F32)
                cb0 = jnp.zeros((B, SSD_CONV - 1, conv_dim), F32)
            else:
                s0, cb0 = st["ssm"][j], st["conv"][j]
            proj = _matmul(x, W["w_in_c"][j], gain=W["norm_mix"][i])
            y, s = _ssd_mix(proj, d_inner, W["conv_w_c"][j], W["conv_b_c"][j], W["dt_bias_c"][j],
                            W["a_log_c"][j], W["d_skip_c"][j], W["g_norm_c"][j], s0, cb0, B, T)
            x = _matmul(y, W["w_out_c"][j], res=x)
            xbc = proj.reshape(B, T, -1)[:, -(SSD_CONV - 1):, d_inner:d_inner + conv_dim]
            o_s.append(s); o_cv.append(xbc)

        final_gain = W["norm_final"] if i == depth - 1 else None
        d_ff = W["w_ffn_out"][i].shape[0]
        if fresh:
            fb0 = jnp.zeros((B, FFN_CONV - 1, d_ff), F32)
            x, fb = _ffn(x, W["norm_ffn"][i], W["w_ffn_in"][i], W["ffn_conv_w"][i], W["ffn_conv_b"][i],
                         W["w_ffn_out"][i], fb0, 1, T, final_gain)
        else:
            xt = x.reshape(B, T, D).transpose(1, 0, 2).reshape(T * B, D)
            fb0 = st["ffn"][i].transpose(1, 0, 2).reshape(1, (FFN_CONV - 1) * B, d_ff)
            xt, fb = _ffn(xt, W["norm_ffn"][i], W["w_ffn_in"][i], W["ffn_conv_w"][i], W["ffn_conv_b"][i],
                          W["w_ffn_out"][i], fb0, B, T * B, final_gain)
            x = xt.reshape(T, B, D).transpose(1, 0, 2).reshape(B * T, D)
            fb = fb.reshape(FFN_CONV - 1, B, d_ff).transpose(1, 0, 2)
        o_f.append(fb)
    return (x.reshape(B, T, D), jnp.stack(o_c), jnp.stack(o_n), jnp.stack(o_m), jnp.stack(o_k),
            jnp.stack(o_v), jnp.stack(o_s), jnp.stack(o_cv), jnp.stack(o_f))


def kernel(x_prompt, x_sample, state_mlstm_c, state_mlstm_n, state_mlstm_m, cache_swa_k, cache_swa_v, state_ssm, state_ssm_conv, state_ffn_conv, norm_mix, norm_ffn, norm_final, w_in_a, b_ig_a, b_fg_a, g_head_a, w_out_a, w_in_b, sinks_b, w_out_b, w_in_c, conv_w_c, conv_b_c, dt_bias_c, a_log_c, d_skip_c, g_norm_c, w_out_c, w_ffn_in, ffn_conv_w, ffn_conv_b, w_ffn_out):
    d_inner = w_out_c.shape[1]
    W = dict(norm_mix=norm_mix, norm_ffn=norm_ffn, norm_final=norm_final,
             w_in_a=[_mlstm_in_weight(w_in_a[i]) for i in range(w_in_a.shape[0])],
             b_ig_a=b_ig_a, b_fg_a=b_fg_a, g_head_a=g_head_a, w_out_a=_per_layer_bf16(w_out_a),
             w_in_b=_per_layer_bf16(w_in_b), sinks_b=sinks_b, w_out_b=_per_layer_bf16(w_out_b),
             w_in_c=[_ssd_in_weight(w_in_c[i], d_inner) for i in range(w_in_c.shape[0])],
             conv_w_c=conv_w_c, conv_b_c=conv_b_c, dt_bias_c=dt_bias_c, a_log_c=a_log_c,
             d_skip_c=d_skip_c, g_norm_c=g_norm_c, w_out_c=_per_layer_bf16(w_out_c),
             w_ffn_in=_per_layer_bf16(w_ffn_in), ffn_conv_w=ffn_conv_w, ffn_conv_b=ffn_conv_b,
             w_ffn_out=_per_layer_bf16(w_ffn_out))
    st = dict(c=state_mlstm_c, n=state_mlstm_n, m=state_mlstm_m, k=cache_swa_k, v=cache_swa_v,
              ssm=state_ssm, conv=state_ssm_conv, ffn=state_ffn_conv)
    yp, pc, pn, pm, pk, pv, ps, pcv, pf = _trunk(x_prompt, None, W)
    ys, sc, sn, sm, sk, sv, ss, scv, sf = _trunk(x_sample, st, W)
    return (yp, ys, pc, sc, pn, sn, pm, sm, pk, sk, pv, sv, ps, ss, pcv, scv, pf, sf)
```

```python
import functools
import math

import jax
import jax.numpy as jnp
from jax import lax
from jax.experimental import pallas as pl
from jax.experimental.pallas import tpu as pltpu

F32 = jnp.float32
BF16 = jnp.bfloat16

RMS_EPS = 1e-6
N_MIXERS = 3

MLSTM_HEADS = 8
SWA_HEADS = 32
SWA_KV_HEADS = 4
SWA_GROUP = SWA_HEADS // SWA_KV_HEADS
WINDOW = 128
SSD_HEAD_DIM = 64
SSD_GROUPS = 8
SSD_STATE = 128
SSD_CONV = 4
FFN_CONV = 3

LANES = 128
SUBLANES = 8
VMEM_LIMIT_BYTES = 56 << 20
VMEM_TILE_BUDGET = 44 << 20

FFN_ROW_TILE = 1024
FFN_COL_TILE = 512
FFN_SUB_TILE = 256
MLSTM_CHUNK = 128
SSD_CHUNK = 128


def _cparams(n_axes):
    return pltpu.CompilerParams(dimension_semantics=("arbitrary",) * n_axes,
                                vmem_limit_bytes=VMEM_LIMIT_BYTES)


def _round_up(a, b):
    return (a + b - 1) // b * b


def _largest_divisor(n, candidates):
    for c in candidates:
        if n % c == 0:
            return c
    raise ValueError(f"no tile in {candidates} divides {n}")


def _rmsnorm(x, g):
    return x * lax.rsqrt(jnp.mean(x * x, axis=-1, keepdims=True) + RMS_EPS) * g


def _silu(x):
    return x * jax.nn.sigmoid(x)


def _mm_body(*refs, norm, residual):
    it = iter(refs)
    x_ref = next(it)
    g_ref = next(it) if norm else None
    w_ref = next(it)
    r_ref = next(it) if residual else None
    o_ref = next(it)
    xn_ref = next(it) if norm else None
    if norm:
        @pl.when(pl.program_id(1) == 0)
        def _():
            xn_ref[...] = _rmsnorm(x_ref[...], g_ref[...]).astype(BF16)
        lhs = xn_ref[...]
    else:
        lhs = x_ref[...].astype(BF16)
    acc = jnp.dot(lhs, w_ref[...], preferred_element_type=F32)
    if residual:
        acc = acc + r_ref[...]
    o_ref[...] = acc.astype(o_ref.dtype)


def _mm_tiles(m, k, n, x_bytes, norm, residual):
    for tm in (1024, 512, 256, 128, 64, 32, 16, 8):
        if m % tm:
            continue
        for tn in (2048, 1280, 1024, 896, 512, 256, 128):
            if n % tn:
                continue
            need = 2 * tm * k * x_bytes + 2 * k * tn * 2 + 2 * tm * tn * 4
            need += tm * k * 2 if norm else 0
            need += 2 * tm * tn * 4 if residual else 0
            if need <= VMEM_TILE_BUDGET:
                return tm, tn
    raise ValueError("no matmul tile fits VMEM")


def _matmul(x, w, layer, gain=None, res=None):
    m, k = x.shape
    n = w.shape[2]
    norm, residual = gain is not None, res is not None
    tm, tn = _mm_tiles(m, k, n, x.dtype.itemsize, norm, residual)
    in_specs = [pl.BlockSpec((tm, k), lambda i, j: (i, 0))]
    args = [x]
    if norm:
        in_specs.append(pl.BlockSpec((1, k), lambda i, j: (0, 0)))
        args.append(gain.reshape(1, k))
    in_specs.append(pl.BlockSpec((None, k, tn), lambda i, j: (layer, 0, j)))
    args.append(w)
    if residual:
        in_specs.append(pl.BlockSpec((tm, tn), lambda i, j: (i, j)))
        args.append(res)
    return pl.pallas_call(
        functools.partial(_mm_body, norm=norm, residual=residual),
        grid=(m // tm, n // tn),
        in_specs=in_specs,
        out_specs=pl.BlockSpec((tm, tn), lambda i, j: (i, j)),
        out_shape=jax.ShapeDtypeStruct((m, n), F32),
        scratch_shapes=[pltpu.VMEM((tm, k), BF16)] if norm else [],
        compiler_params=_cparams(2),
        name="matmul",
    )(*args)


def _ffn_body(x_ref, gn_ref, wu_ref, wg_ref, cw_ref, cb_ref, wo_ref, buf0_ref, gf_ref,
              o_ref, bufo_ref, xn_ref, gs_ref, carry_ref,
              *, tm, shift, halo, tiles_per_seq, nj, final_norm):
    i = pl.program_id(0)
    j = pl.program_id(1)

    @pl.when(j == 0)
    def _():
        x = x_ref[...]
        xn_ref[...] = _rmsnorm(x, gn_ref[...]).astype(BF16)
        o_ref[...] = x

    xn = xn_ref[...]
    first = (i % tiles_per_seq) == 0
    gs_ref[0:halo, :] = jnp.where(first, buf0_ref[0], carry_ref[j])
    base = halo - (FFN_CONV - 1) * shift
    tf = wu_ref.shape[1]
    sub = min(tf, FFN_SUB_TILE)
    h_parts = []
    for s0 in range(0, tf, sub):
        cols = slice(s0, s0 + sub)
        u = jnp.dot(xn, wu_ref[:, cols], preferred_element_type=F32)
        gs_ref[halo:halo + tm, cols] = jnp.dot(xn, wg_ref[:, cols], preferred_element_type=F32)
        c = gs_ref[base:base + tm, cols] * cw_ref[0:1, cols]
        for t in range(1, FFN_CONV):
            c = c + gs_ref[base + t * shift:base + t * shift + tm, cols] * cw_ref[t:t + 1, cols]
        c = c + cb_ref[:, cols]
        h_parts.append((_silu(c) * u).astype(BF16))
    tail = gs_ref[tm:tm + halo, :]
    carry_ref[j] = tail
    bufo_ref[0] = tail
    h = h_parts[0] if len(h_parts) == 1 else jnp.concatenate(h_parts, axis=1)
    o_ref[...] += jnp.dot(h, wo_ref[...], preferred_element_type=F32)

    if final_norm:
        @pl.when(j == nj - 1)
        def _():
            o_ref[...] = _rmsnorm(o_ref[...], gf_ref[...])


def _ffn(x, gain, w_in, conv_w, conv_b, w_out, layer, buf0, shift, rows_per_seq, final_gain):
    m, d = x.shape
    d_ff = w_out.shape[1]
    tm = _largest_divisor(rows_per_seq, (FFN_ROW_TILE, 512, 256, 128, 64, 32, 16, 8))
    tf = _largest_divisor(d_ff, (FFN_COL_TILE, 256, 128))
    nj = d_ff // tf
    tiles_per_seq = rows_per_seq // tm
    pre = (FFN_CONV - 1) * shift
    halo = _round_up(pre, SUBLANES)
    assert tm >= halo and tm % SUBLANES == 0
    buf0p = jnp.pad(buf0, ((0, 0), (halo - pre, 0), (0, 0)))
    final_norm = final_gain is not None
    gf = (final_gain if final_norm else gain).reshape(1, d)
    body = functools.partial(_ffn_body, tm=tm, shift=shift, halo=halo, tiles_per_seq=tiles_per_seq,
                             nj=nj, final_norm=final_norm)
    out, bufo = pl.pallas_call(
        body,
        grid=(m // tm, nj),
        in_specs=[
            pl.BlockSpec((tm, d), lambda i, j: (i, 0)),
            pl.BlockSpec((1, d), lambda i, j: (0, 0)),
            pl.BlockSpec((None, d, tf), lambda i, j: (layer, 0, j)),
            pl.BlockSpec((None, d, tf), lambda i, j: (layer, 0, nj + j)),
            pl.BlockSpec((FFN_CONV, tf), lambda i, j: (0, j)),
            pl.BlockSpec((1, tf), lambda i, j: (0, j)),
            pl.BlockSpec((None, tf, d), lambda i, j: (layer, j, 0)),
            pl.BlockSpec((1, halo, tf), lambda i, j: (i // tiles_per_seq, 0, j)),
            pl.BlockSpec((1, d), lambda i, j: (0, 0)),
        ],
        out_specs=[
            pl.BlockSpec((tm, d), lambda i, j: (i, 0)),
            pl.BlockSpec((1, halo, tf), lambda i, j: (i, 0, j)),
        ],
        out_shape=[jax.ShapeDtypeStruct((m, d), F32),
                   jax.ShapeDtypeStruct((m // tm, halo, d_ff), F32)],
        scratch_shapes=[pltpu.VMEM((tm, d), BF16),
                        pltpu.VMEM((halo + tm, tf), F32),
                        pltpu.VMEM((nj, halo, tf), F32)],
        compiler_params=_cparams(2),
        name="conv_ffn",
    )(x, gain.reshape(1, d), w_in, w_in, conv_w, conv_b.reshape(1, d_ff), w_out, buf0p, gf)
    return out, bufo[tiles_per_seq - 1::tiles_per_seq, halo - pre:, :]


def _row_from_col(col, eye):
    return jnp.sum(jnp.where(eye, col, 0.0), axis=0, keepdims=True)


def _cumsum_rows(x, tril):
    return jnp.dot(tril, x, preferred_element_type=F32, precision=lax.Precision.HIGHEST)


def _mlstm_body(*refs, chunk, dk, dv, has_init):
    q_ref, k_ref, v_ref, o_ref, gate_ref, bias_ref, gh_ref = refs[:7]
    c0_ref, n0_ref, m0_ref = refs[7:10] if has_init else (None, None, None)
    hg_ref, c_ref, n_ref, m_ref = refs[-4:]
    L = chunk
    H = MLSTM_HEADS

    @pl.when(pl.program_id(1) == 0)
    def _():
        if has_init:
            c_ref[...] = c0_ref[...]
            n_ref[...] = n0_ref[...]
            m_ref[...] = m0_ref[...]
        else:
            c_ref[...] = jnp.zeros_like(c_ref)
            n_ref[...] = jnp.zeros_like(n_ref)
            m_ref[...] = jnp.zeros_like(m_ref)

    ri = lax.broadcasted_iota(jnp.int32, (L, L), 0)
    ci = lax.broadcasted_iota(jnp.int32, (L, L), 1)
    causal = ri >= ci
    eye = ri == ci
    tril = causal.astype(F32)

    gates = gate_ref[...] + bias_ref[...]
    x = gates
    log_sig = jnp.minimum(x, 0.0) - jnp.log1p(jnp.exp(-jnp.abs(x)))
    bcum = _cumsum_rows(log_sig, tril)

    n_all = n_ref[0]
    m_all = m_ref[0]
    if L % LANES == 0:
        gates_t, bcum_t = gates.T, bcum.T
        row_of = lambda arr_t, col, lane: arr_t[lane:lane + 1, :]
    else:
        gates_t = bcum_t = None
        row_of = lambda arr_t, col, lane: _row_from_col(col, eye)

    stack = lambda parts: jnp.concatenate(parts, axis=0)
    ig_cols = [gates[:, h:h + 1] for h in range(H)]
    b_cols = [bcum[:, H + h:H + h + 1] for h in range(H)]
    m_prevs = [m_all[h:h + 1, 0:1] for h in range(H)]
    ig_col, b_col = stack(ig_cols), stack(b_cols)
    m_prev = stack([jnp.broadcast_to(mp, (L, 1)) for mp in m_prevs])
    ig_row = stack([jnp.broadcast_to(row_of(gates_t, ig_cols[h], h), (L, L)) for h in range(H)])
    b_row = stack([jnp.broadcast_to(row_of(bcum_t, b_cols[h], H + h), (L, L)) for h in range(H)])
    causal_all = stack([causal] * H)

    log_d = jnp.where(causal_all, b_col - b_row + ig_row, -jnp.inf)
    log_inter = b_col + m_prev
    m_t = jnp.maximum(log_inter, jnp.max(log_d, axis=-1, keepdims=True))
    d = jnp.exp(log_d - m_t)
    inter = jnp.exp(log_inter - m_t)

    qk, qc, qn, kfs, vbs, c_prevs = [], [], [], [], [], []
    for h in range(H):
        qf = q_ref[:, h * dk:(h + 1) * dk] * (dk ** -0.5)
        kf = k_ref[:, h * dk:(h + 1) * dk]
        qb = qf.astype(BF16)
        c_prev = c_ref[0, h]
        qk.append(lax.dot_general(qb, kf.astype(BF16), (((1,), (1,)), ((), ())), preferred_element_type=F32))
        qc.append(jnp.dot(qb, c_prev.astype(BF16), preferred_element_type=F32))
        qn.append(jnp.sum(qf * n_all[h:h + 1, :], axis=-1, keepdims=True))
        kfs.append(kf)
        vbs.append(v_ref[:, h * dv:(h + 1) * dv].astype(BF16))
        c_prevs.append(c_prev)

    s = stack(qk) * d
    sb = s.astype(BF16)
    sv = stack([jnp.dot(sb[h * L:(h + 1) * L], vbs[h], preferred_element_type=F32) for h in range(H)])
    num = inter * stack(qc) + sv
    den = inter * stack(qn) + jnp.sum(s, axis=-1, keepdims=True)
    hh = num / jnp.maximum(jnp.abs(den), jnp.exp(-m_t))
    hn = hh * lax.rsqrt(jnp.mean(hh * hh, axis=-1, keepdims=True) + RMS_EPS)
    for h in range(H):
        lanes = slice(h * dv, (h + 1) * dv)
        out = jax.nn.sigmoid(o_ref[:, lanes]) * (hn[h * L:(h + 1) * L] * gh_ref[:, lanes])
        hg_ref[:, lanes] = out.astype(hg_ref.dtype)

    b_lasts = [bc[L - 1:L, :] for bc in b_cols]
    log_w = stack([jnp.broadcast_to(bl, (L, 1)) for bl in b_lasts]) - b_col + ig_col
    m_news = [jnp.maximum(b_lasts[h] + m_prevs[h], jnp.max(log_w[h * L:(h + 1) * L], axis=0, keepdims=True))
              for h in range(H)]
    w = jnp.exp(log_w - stack([jnp.broadcast_to(mn, (L, 1)) for mn in m_news]))

    head_row = lax.broadcasted_iota(jnp.int32, (H, LANES), 0)
    n_new, m_new_all = n_all, m_all
    for h in range(H):
        decay = jnp.exp(b_lasts[h] + m_prevs[h] - m_news[h])
        kw = kfs[h] * w[h * L:(h + 1) * L]
        upd = lax.dot_general(kw.astype(BF16), vbs[h], (((0,), (0,)), ((), ())), preferred_element_type=F32)
        c_ref[0, h] = decay * c_prevs[h] + upd
        n_row = decay * n_all[h:h + 1, :] + jnp.sum(kw, axis=0, keepdims=True)
        n_new = jnp.where(head_row == h, n_row, n_new)
        m_new_all = jnp.where(head_row == h, m_news[h], m_new_all)
    n_ref[0] = n_new
    m_ref[0] = m_new_all


def _mlstm_mix(proj, bias, g_head, init, layer, B, T, dk, dv):
    H = MLSTM_HEADS
    L = math.gcd(T, MLSTM_CHUNK)
    nc = T // L
    qw, vw = H * dk, H * dv
    assert qw * 2 == vw and vw % LANES == 0
    gate_blk = (2 * qw + 2 * vw) // LANES
    hg_dtype = BF16 if L % 16 == 0 else F32
    row = lambda b, c: b * nc + c
    c_spec = pl.BlockSpec((None, 1, H, dk, dv), lambda b, c: (layer, b, 0, 0, 0))
    in_specs = [
        pl.BlockSpec((L, qw), lambda b, c: (row(b, c), 0)),
        pl.BlockSpec((L, qw), lambda b, c: (row(b, c), 1)),
        pl.BlockSpec((L, vw), lambda b, c: (row(b, c), 1)),
        pl.BlockSpec((L, vw), lambda b, c: (row(b, c), 2)),
        pl.BlockSpec((L, LANES), lambda b, c: (row(b, c), gate_blk)),
        pl.BlockSpec((1, LANES), lambda b, c: (0, 0)),
        pl.BlockSpec((1, vw), lambda b, c: (0, 0)),
    ]
    args = [proj, proj, proj, proj, proj, bias, g_head.reshape(1, vw)]
    if init is not None:
        c_all, n0, m0 = init
        in_specs += [c_spec,
                     pl.BlockSpec((1, H, dk), lambda b, c: (b, 0, 0)),
                     pl.BlockSpec((1, H, LANES), lambda b, c: (b, 0, 0))]
        args += [c_all, n0, jnp.broadcast_to(m0[:, :, None], (B, H, LANES))]
    hg, c, n, m = pl.pallas_call(
        functools.partial(_mlstm_body, chunk=L, dk=dk, dv=dv, has_init=init is not None),
        grid=(B, nc),
        in_specs=in_specs,
        out_specs=[
            pl.BlockSpec((L, vw), lambda b, c: (row(b, c), 0)),
            pl.BlockSpec((1, H, dk, dv), lambda b, c: (b, 0, 0, 0)),
            pl.BlockSpec((1, H, dk), lambda b, c: (b, 0, 0)),
            pl.BlockSpec((1, H, LANES), lambda b, c: (b, 0, 0)),
        ],
        out_shape=[jax.ShapeDtypeStruct((B * T, vw), hg_dtype),
                   jax.ShapeDtypeStruct((B, H, dk, dv), F32),
                   jax.ShapeDtypeStruct((B, H, dk), F32),
                   jax.ShapeDtypeStruct((B, H, LANES), F32)],
        compiler_params=_cparams(2),
        name="mlstm",
    )(*args)
    return hg, c, n, m[:, :, 0]


def _swa_body(sink_ref, q_ref, kp_ref, kc_ref, vp_ref, vc_ref, o_ref, *, tq, hd, mask_first):
    W = WINDOW
    G = SWA_GROUP
    S = 2 * W
    kk = jnp.concatenate([kp_ref[...], kc_ref[...]], axis=0)
    vv = jnp.concatenate([vp_ref[...], vc_ref[...]], axis=0)
    qi = lax.broadcasted_iota(jnp.int32, (tq, S), 0)
    kj = lax.broadcasted_iota(jnp.int32, (tq, S), 1)
    dist = W + qi - kj
    valid = (dist >= 0) & (dist <= W)
    if mask_first:
        valid = valid & ((kj >= W) | (pl.program_id(1) > 0))
    distf = dist.astype(F32)
    lane_s = lax.broadcasted_iota(jnp.int32, (S, LANES), 1)
    lane_q = lax.broadcasted_iota(jnp.int32, (tq, LANES), 1)
    low_s = lane_s < hd
    low_q = lane_q < hd
    heads_per_slab = LANES // hd
    assert heads_per_slab == 2

    for kh in range(SWA_KV_HEADS):
        slab = (kh // heads_per_slab) * LANES
        keep_low = (kh % heads_per_slab) == 0
        k128 = kk[:, slab:slab + LANES]
        v128 = vv[:, slab:slab + LANES]
        k_rot = pltpu.roll(k128, hd, axis=1)
        v_rot = pltpu.roll(v128, hd, axis=1)
        own = low_s if keep_low else jnp.logical_not(low_s)
        k2 = jnp.where(own, k128, k_rot).astype(BF16)
        v2 = jnp.where(own, v128, v_rot).astype(BF16)

        q_parts = []
        for g in range(G):
            h = kh * G + g
            qs = q_ref[:, (h // 2) * LANES:(h // 2 + 1) * LANES]
            keep = low_q if h % 2 == 0 else jnp.logical_not(low_q)
            q_parts.append(jnp.where(keep, qs, 0.0))
        q_stack = jnp.concatenate(q_parts, axis=0).astype(BF16)
        s_all = lax.dot_general(q_stack, k2, (((1,), (1,)), ((), ())), preferred_element_type=F32)
        s_all = s_all * (hd ** -0.5)

        p_parts = []
        for g in range(G):
            h = kh * G + g
            slope = 2.0 ** (-8.0 * (h + 1) / SWA_HEADS)
            sink = sink_ref[h]
            s = s_all[g * tq:(g + 1) * tq, :] - slope * distf
            s = jnp.where(valid, s, -jnp.inf)
            mx = jnp.maximum(jnp.max(s, axis=-1, keepdims=True), sink)
            p = jnp.exp(s - mx)
            p = p / (jnp.sum(p, axis=-1, keepdims=True) + jnp.exp(sink - mx))
            p_parts.append(p)
        p_stack = jnp.concatenate(p_parts, axis=0).astype(BF16)
        o_all = jnp.dot(p_stack, v2, preferred_element_type=F32)

        for g2 in range(G // 2):
            h = kh * G + 2 * g2
            o_even = o_all[(2 * g2) * tq:(2 * g2 + 1) * tq, :]
            o_odd = o_all[(2 * g2 + 1) * tq:(2 * g2 + 2) * tq, :]
            o_ref[:, (h // 2) * LANES:(h // 2 + 1) * LANES] = jnp.where(low_q, o_even, o_odd).astype(o_ref.dtype)


def _swa_call(sinks, q_arr, q_blk, k_arr, kp_map, kc_map, v_arr, vp_map, vc_map, grid, tq, n_rows, hd,
              mask_first):
    kvw = SWA_KV_HEADS * hd
    qw = SWA_HEADS * hd
    out_dtype = BF16 if tq % 16 == 0 else F32
    return pl.pallas_call(
        functools.partial(_swa_body, tq=tq, hd=hd, mask_first=mask_first),
        grid=grid,
        in_specs=[
            pl.BlockSpec(memory_space=pltpu.SMEM),
            pl.BlockSpec((tq, qw), q_blk),
            pl.BlockSpec((WINDOW, kvw), kp_map),
            pl.BlockSpec((WINDOW, kvw), kc_map),
            pl.BlockSpec((WINDOW, kvw), vp_map),
            pl.BlockSpec((WINDOW, kvw), vc_map),
        ],
        out_specs=pl.BlockSpec((tq, qw), q_blk),
        out_shape=jax.ShapeDtypeStruct((n_rows, qw), out_dtype),
        compiler_params=_cparams(2),
        name="swa",
    )(sinks, q_arr, k_arr, k_arr, v_arr, v_arr)


def _conv_silu(u_ref, init_ref, carry_ref, gs_ref, cw_ref, cb_ref, first, L):
    halo = SUBLANES
    gs_ref[0:halo, :] = jnp.where(first, init_ref[0], carry_ref[...])
    gs_ref[halo:halo + L, :] = u_ref[...]
    carry_ref[...] = gs_ref[L:L + halo, :]
    base = halo - (SSD_CONV - 1)
    cw = cw_ref[...]
    c = gs_ref[base:base + L, :] * cw[0:1, :]
    for t in range(1, SSD_CONV):
        c = c + gs_ref[base + t:base + t + L, :] * cw[t:t + 1, :]
    c = c + cb_ref[...]
    return _silu(c)


def _split3_bf16(x):
    hi = x.astype(BF16)
    r = x - hi.astype(F32)
    mid = r.astype(BF16)
    lo = (r - mid.astype(F32)).astype(BF16)
    return hi, mid, lo


def _ssd_body(z_ref, xs_ref, bc_ref, dt_ref, dtb_ref, alog_ref, dsk_ref, gn_ref,
              cwx_ref, cwbc_ref, cbx_ref, cbbc_ref, ix_ref, ibc_ref, s0_ref,
              y_ref, s_ref,
              gx_ref, gbc_ref, kx_ref, kbc_ref, *, chunk, groups, hpg, p_dim):
    L = chunk
    N = SSD_STATE
    gw = hpg * p_dim
    first = pl.program_id(1) == 0

    @pl.when(first)
    def _():
        s_ref[...] = s0_ref[...]

    xs_all = _conv_silu(xs_ref, ix_ref, kx_ref, gx_ref, cwx_ref, cbx_ref, first, L)
    bc_all = _conv_silu(bc_ref, ibc_ref, kbc_ref, gbc_ref, cwbc_ref, cbbc_ref, first, L)

    ri = lax.broadcasted_iota(jnp.int32, (L, L), 0)
    ci = lax.broadcasted_iota(jnp.int32, (L, L), 1)
    causal = ri >= ci
    eye = ri == ci
    tril = causal.astype(F32)

    dt_all = jax.nn.softplus(dt_ref[...] + dtb_ref[...])
    acum_all = _cumsum_rows(dt_all * (-jnp.exp(alog_ref[...])), tril)
    a_last_all = acum_all[L - 1:L, :]
    eacum_all = jnp.exp(acum_all)
    dtw_all = dt_all * jnp.exp(a_last_all - acum_all)
    ea_last_all = jnp.exp(a_last_all)

    er = lax.broadcasted_iota(jnp.int32, (3 * LANES, gw), 0)
    ec = lax.broadcasted_iota(jnp.int32, (3 * LANES, gw), 1)
    assert p_dim & (p_dim - 1) == 0
    head_of_lane = lax.shift_right_logical(ec, p_dim.bit_length() - 1)
    expand3 = (jnp.bitwise_and(er, LANES - 1) == head_of_lane).astype(BF16)

    lane = lax.broadcasted_iota(jnp.int32, (L, LANES), 1)
    low = lane < p_dim
    srow = lax.broadcasted_iota(jnp.int32, (LANES, 1), 0)
    heads_per_slab = LANES // p_dim
    assert heads_per_slab == 2

    gated = []
    for g in range(groups):
        hs = slice(g * LANES, (g + 1) * LANES)
        acum = acum_all[:, hs]
        ea_last = ea_last_all[:, hs]
        per_head = jnp.concatenate([dt_all[:, hs], eacum_all[:, hs], dtw_all[:, hs]], axis=0)
        ex = jnp.dot(jnp.concatenate(_split3_bf16(per_head), axis=1), expand3,
                     preferred_element_type=F32)
        dt_e, ea_e, dtw_e = ex[0:L], ex[L:2 * L], ex[2 * L:3 * L]

        xs = xs_all[:, g * gw:(g + 1) * gw]
        bmb = bc_all[:, g * N:(g + 1) * N].astype(BF16)
        cmb = bc_all[:, (groups + g) * N:(groups + g + 1) * N].astype(BF16)
        cb = lax.dot_general(cmb, bmb, (((1,), (1,)), ((), ())), preferred_element_type=F32)
        xdt = (xs * dt_e).astype(BF16)
        xw = (xs * dtw_e).astype(BF16)
        s_prev = s_ref[0, g * gw:(g + 1) * gw, :]
        y_inter = lax.dot_general(cmb, s_prev.astype(BF16), (((1,), (1,)), ((), ())),
                                  preferred_element_type=F32)
        upd = lax.dot_general(xw, bmb, (((0,), (0,)), ((), ())), preferred_element_type=F32)
        acum_t = acum.T if L % LANES == 0 else None

        y_parts = []
        for pr in range(hpg // heads_per_slab):
            h0 = 2 * pr
            ms = []
            for hh in (h0, h0 + 1):
                a_col = acum[:, hh:hh + 1]
                a_row = acum_t[hh:hh + 1, :] if acum_t is not None else _row_from_col(a_col, eye)
                dec = jnp.exp(jnp.where(causal, a_col - a_row, -jnp.inf))
                ms.append((cb * dec).astype(BF16))
            mcat = jnp.concatenate(ms, axis=1)
            xb = xdt[:, pr * LANES:(pr + 1) * LANES]
            zero = jnp.zeros_like(xb)
            bd = jnp.concatenate([jnp.where(low, xb, zero), jnp.where(low, zero, xb)], axis=0)
            y_parts.append(jnp.dot(mcat, bd, preferred_element_type=F32))

            scale = jnp.where(srow < p_dim, ea_last[:, h0:h0 + 1], ea_last[:, h0 + 1:h0 + 2])
            rows = slice(g * gw + pr * LANES, g * gw + (pr + 1) * LANES)
            s_ref[0, rows, :] = s_prev[pr * LANES:(pr + 1) * LANES] * scale + upd[pr * LANES:(pr + 1) * LANES]

        y = jnp.concatenate(y_parts, axis=1) + y_inter * ea_e
        y = y + dsk_ref[:, g * gw:(g + 1) * gw] * xs
        gated.append(y * _silu(z_ref[:, g * gw:(g + 1) * gw]))

    sq = gated[0] * gated[0]
    for y in gated[1:]:
        sq = sq + y * y
    inv = lax.rsqrt(jnp.sum(sq, axis=-1, keepdims=True) * (1.0 / (groups * gw)) + RMS_EPS)
    for g, y in enumerate(gated):
        y_ref[:, g * gw:(g + 1) * gw] = (y * inv * gn_ref[:, g * gw:(g + 1) * gw]).astype(y_ref.dtype)


def _ssd_mix(proj, d_inner, conv_w, conv_b, dt_bias, a_log, d_skip, g_norm, s0, conv0, B, T):
    G, P, N = SSD_GROUPS, SSD_HEAD_DIM, SSD_STATE
    heads = d_inner // P
    hpg = heads // G
    L = math.gcd(T, SSD_CHUNK)
    nc = T // L
    bcw = 2 * G * N
    dtw = G * LANES
    conv_dim = d_inner + bcw
    assert d_inner % bcw == 0 and (d_inner + conv_dim) % dtw == 0
    bc_blk = (2 * d_inner) // bcw
    dt_blk = (d_inner + conv_dim) // dtw
    pad_heads = lambda v: jnp.pad(v.reshape(G, hpg), ((0, 0), (0, LANES - hpg))).reshape(1, dtw)
    halo = SUBLANES
    conv0p = jnp.pad(conv0, ((0, 0), (halo - (SSD_CONV - 1), 0), (0, 0)))
    conv_b2 = conv_b.reshape(1, conv_dim)
    dsk = jnp.repeat(d_skip, P).reshape(1, d_inner)
    row = lambda b, c: b * nc + c
    const = lambda b, c: (0, 0)
    y, s = pl.pallas_call(
        functools.partial(_ssd_body, chunk=L, groups=G, hpg=hpg, p_dim=P),
        grid=(B, nc),
        in_specs=[
            pl.BlockSpec((L, d_inner), lambda b, c: (row(b, c), 0)),
            pl.BlockSpec((L, d_inner), lambda b, c: (row(b, c), 1)),
            pl.BlockSpec((L, bcw), lambda b, c: (row(b, c), bc_blk)),
            pl.BlockSpec((L, dtw), lambda b, c: (row(b, c), dt_blk)),
            pl.BlockSpec((1, dtw), const),
            pl.BlockSpec((1, dtw), const),
            pl.BlockSpec((1, d_inner), const),
            pl.BlockSpec((1, d_inner), const),
            pl.BlockSpec((SSD_CONV, d_inner), const),
            pl.BlockSpec((SSD_CONV, bcw), lambda b, c: (0, d_inner // bcw)),
            pl.BlockSpec((1, d_inner), const),
            pl.BlockSpec((1, bcw), lambda b, c: (0, d_inner // bcw)),
            pl.BlockSpec((1, halo, d_inner), lambda b, c: (b, 0, 0)),
            pl.BlockSpec((1, halo, bcw), lambda b, c: (b, 0, d_inner // bcw)),
            pl.BlockSpec((1, d_inner, N), lambda b, c: (b, 0, 0)),
        ],
        out_specs=[
            pl.BlockSpec((L, d_inner), lambda b, c: (row(b, c), 0)),
            pl.BlockSpec((1, d_inner, N), lambda b, c: (b, 0, 0)),
        ],
        out_shape=[jax.ShapeDtypeStruct((B * T, d_inner), BF16 if L % 16 == 0 else F32),
                   jax.ShapeDtypeStruct((B, d_inner, N), F32)],
        scratch_shapes=[pltpu.VMEM((halo + L, d_inner), F32), pltpu.VMEM((halo + L, bcw), F32),
                        pltpu.VMEM((halo, d_inner), F32), pltpu.VMEM((halo, bcw), F32)],
        compiler_params=_cparams(2),
        name="ssd",
    )(proj, proj, proj, proj, pad_heads(dt_bias), pad_heads(a_log), dsk, g_norm.reshape(1, d_inner),
      conv_w, conv_w, conv_b2, conv_b2, conv0p, conv0p, s0.reshape(B, d_inner, N))
    return y, s.reshape(B, heads, P, N)


MLSTM_GATE_PAD = 2 * LANES


def _mlstm_in_weight(w):
    gates = 2 * MLSTM_HEADS
    w = w.astype(BF16)
    w_gate = jnp.pad(w[:, :, -gates:], ((0, 0), (0, 0), (0, MLSTM_GATE_PAD - gates)))
    return jnp.concatenate([w[:, :, :-gates], w_gate], axis=-1)


def _ssd_in_weight(w, d_inner):
    heads = d_inner // SSD_HEAD_DIM
    hpg = heads // SSD_GROUPS
    w = w.astype(BF16)
    w_dt = w[:, :, -heads:].reshape(w.shape[0], w.shape[1], SSD_GROUPS, hpg)
    w_dt = jnp.pad(w_dt, ((0, 0), (0, 0), (0, 0), (0, LANES - hpg)))
    w_dt = w_dt.reshape(w.shape[0], w.shape[1], SSD_GROUPS * LANES)
    return jnp.concatenate([w[:, :, :-heads], w_dt], axis=-1)


def _trunk(x3, st, W):
    B, T, D = x3.shape
    fresh = st is None
    x = x3.reshape(B * T, D)
    depth = W["norm_mix"].shape[0]
    o_c, o_n, o_m, o_k, o_v, o_s, o_cv, o_f = [], [], [], [], [], [], [], []
    for i in range(depth):
        j = i // N_MIXERS
        kind = i % N_MIXERS
        if kind == 0:
            dv = D // MLSTM_HEADS
            dk = dv // 2
            init = None if fresh else (st["c"], st["n"][j], st["m"][j])
            proj = _matmul(x, W["w_in_a"], j, gain=W["norm_mix"][i])
            bias = jnp.pad(jnp.concatenate([W["b_ig_a"][j], W["b_fg_a"][j]]), (0, LANES - 2 * MLSTM_HEADS))
            hg, c, n, m = _mlstm_mix(proj, bias.reshape(1, LANES), W["g_head_a"][j], init, j, B, T, dk, dv)
            x = _matmul(hg, W["w_out_a"], j, res=x)
            o_c.append(c); o_n.append(n); o_m.append(m)
        elif kind == 1:
            hd = D // SWA_HEADS
            kvw = SWA_KV_HEADS * hd
            qw = SWA_HEADS * hd
            proj = _matmul(x, W["w_in_b"], j, gain=W["norm_mix"][i])
            p3 = proj.reshape(B, T, qw + 2 * kvw)
            k_new, v_new = p3[:, :, qw:qw + kvw], p3[:, :, qw + kvw:]
            kb, vb = qw // kvw, qw // kvw + 1
            if fresh:
                nb = T // WINDOW
                prev = lambda b, n: (b * nb + jnp.maximum(n - 1, 0), kb)
                cur = lambda b, n: (b * nb + n, kb)
                prev_v = lambda b, n: (b * nb + jnp.maximum(n - 1, 0), vb)
                cur_v = lambda b, n: (b * nb + n, vb)
                o = _swa_call(W["sinks_b"][j], proj, lambda b, n: (b * nb + n, 0), proj, prev, cur,
                              proj, prev_v, cur_v, (B, nb), WINDOW, B * T, hd, True)
                kc, vc = k_new[:, -WINDOW:], v_new[:, -WINDOW:]
            else:
                kbuf = st["k"][j].reshape(B, WINDOW, kvw)
                vbuf = st["v"][j].reshape(B, WINDOW, kvw)
                zpad = jnp.zeros((B, WINDOW - T, kvw), F32)
                kk = jnp.concatenate([kbuf, k_new, zpad], axis=1).reshape(B * 2 * WINDOW, kvw)
                vv = jnp.concatenate([vbuf, v_new, zpad], axis=1).reshape(B * 2 * WINDOW, kvw)
                o = _swa_call(W["sinks_b"][j], proj, lambda b, n: (b, 0), kk, lambda b, n: (2 * b, 0),
                              lambda b, n: (2 * b + 1, 0), vv, lambda b, n: (2 * b, 0),
                              lambda b, n: (2 * b + 1, 0), (B, 1), T, B * T, hd, False)
                kc = jnp.concatenate([kbuf[:, T:], k_new], axis=1)
                vc = jnp.concatenate([vbuf[:, T:], v_new], axis=1)
            x = _matmul(o, W["w_out_b"], j, res=x)
            o_k.append(kc.reshape(B, WINDOW, SWA_KV_HEADS, hd))
            o_v.append(vc.reshape(B, WINDOW, SWA_KV_HEADS, hd))
        else:
            d_inner = W["w_out_c"].shape[1]
            heads = d_inner // SSD_HEAD_DIM
            conv_dim = d_inner + 2 * SSD_GROUPS * SSD_STATE
            if fresh:
                s0 = jnp.zeros((B, heads, SSD_HEAD_DIM, SSD_STATE), F32)
                cb0 = jnp.zeros((B, SSD_CONV - 1, conv_dim), F32)
            else:
                s0, cb0 = st["ssm"][j], st["conv"][j]
            proj = _matmul(x, W["w_in_c"], j, gain=W["norm_mix"][i])
            y, s = _ssd_mix(proj, d_inner, W["conv_w_c"][j], W["conv_b_c"][j], W["dt_bias_c"][j],
                            W["a_log_c"][j], W["d_skip_c"][j], W["g_norm_c"][j], s0, cb0, B, T)
            x = _matmul(y, W["w_out_c"], j, res=x)
            xbc = proj.reshape(B, T, -1)[:, -(SSD_CONV - 1):, d_inner:d_inner + conv_dim]
            o_s.append(s); o_cv.append(xbc)

        final_gain = W["norm_final"] if i == depth - 1 else None
        d_ff = W["w_ffn_out"].shape[1]
        if fresh:
            fb0 = jnp.zeros((B, FFN_CONV - 1, d_ff), F32)
            x, fb = _ffn(x, W["norm_ffn"][i], W["w_ffn_in"], W["ffn_conv_w"][i], W["ffn_conv_b"][i],
                         W["w_ffn_out"], i, fb0, 1, T, final_gain)
        else:
            xt = x.reshape(B, T, D).transpose(1, 0, 2).reshape(T * B, D)
            fb0 = st["ffn"][i].transpose(1, 0, 2).reshape(1, (FFN_CONV - 1) * B, d_ff)
            xt, fb = _ffn(xt, W["norm_ffn"][i], W["w_ffn_in"], W["ffn_conv_w"][i], W["ffn_conv_b"][i],
                          W["w_ffn_out"], i, fb0, B, T * B, final_gain)
            x = xt.reshape(T, B, D).transpose(1, 0, 2).reshape(B * T, D)
            fb = fb.reshape(FFN_CONV - 1, B, d_ff).transpose(1, 0, 2)
        o_f.append(fb)
    return (x.reshape(B, T, D), jnp.stack(o_c), jnp.stack(o_n), jnp.stack(o_m), jnp.stack(o_k),
            jnp.stack(o_v), jnp.stack(o_s), jnp.stack(o_cv), jnp.stack(o_f))


def kernel(x_prompt, x_sample, state_mlstm_c, state_mlstm_n, state_mlstm_m, cache_swa_k, cache_swa_v, state_ssm, state_ssm_conv, state_ffn_conv, norm_mix, norm_ffn, norm_final, w_in_a, b_ig_a, b_fg_a, g_head_a, w_out_a, w_in_b, sinks_b, w_out_b, w_in_c, conv_w_c, conv_b_c, dt_bias_c, a_log_c, d_skip_c, g_norm_c, w_out_c, w_ffn_in, ffn_conv_w, ffn_conv_b, w_ffn_out):
    d_inner = w_out_c.shape[1]
    W = dict(norm_mix=norm_mix, norm_ffn=norm_ffn, norm_final=norm_final,
             w_in_a=_mlstm_in_weight(w_in_a),
             b_ig_a=b_ig_a, b_fg_a=b_fg_a, g_head_a=g_head_a, w_out_a=w_out_a.astype(BF16),
             w_in_b=w_in_b.astype(BF16), sinks_b=sinks_b, w_out_b=w_out_b.astype(BF16),
             w_in_c=_ssd_in_weight(w_in_c, d_inner),
             conv_w_c=conv_w_c, conv_b_c=conv_b_c, dt_bias_c=dt_bias_c, a_log_c=a_log_c,
             d_skip_c=d_skip_c, g_norm_c=g_norm_c, w_out_c=w_out_c.astype(BF16),
             w_ffn_in=w_ffn_in.astype(BF16), ffn_conv_w=ffn_conv_w, ffn_conv_b=ffn_conv_b,
             w_ffn_out=w_ffn_out.astype(BF16))
    st = dict(c=state_mlstm_c, n=state_mlstm_n, m=state_mlstm_m, k=cache_swa_k, v=cache_swa_v,
              ssm=state_ssm, conv=state_ssm_conv, ffn=state_ffn_conv)
    yp, pc, pn, pm, pk, pv, ps, pcv, pf = _trunk(x_prompt, None, W)
    ys, sc, sn, sm, sk, sv, ss, scv, sf = _trunk(x_sample, st, W)
    return (yp, ys, pc, sc, pn, sn, pm, sm, pk, sk, pv, sv, ps, ss, pcv, scv, pf, sf)
```

```python
import functools
import math

import jax
import jax.numpy as jnp
from jax import lax
from jax.experimental import pallas as pl
from jax.experimental.pallas import tpu as pltpu

F32 = jnp.float32
BF16 = jnp.bfloat16

RMS_EPS = 1e-6
N_MIXERS = 3

MLSTM_HEADS = 8
SWA_HEADS = 32
SWA_KV_HEADS = 4
SWA_GROUP = SWA_HEADS // SWA_KV_HEADS
WINDOW = 128
SSD_HEAD_DIM = 64
SSD_GROUPS = 8
SSD_STATE = 128
SSD_CONV = 4
FFN_CONV = 3

LANES = 128
SUBLANES = 8
VMEM_LIMIT_BYTES = 56 << 20
VMEM_TILE_BUDGET = 44 << 20

FFN_ROW_TILE = 1024
FFN_COL_TILE = 512
FFN_SUB_TILE = 256
MLSTM_CHUNK = 128
SSD_CHUNK = 128


def _cparams(n_axes):
    return pltpu.CompilerParams(dimension_semantics=("arbitrary",) * n_axes,
                                vmem_limit_bytes=VMEM_LIMIT_BYTES)


def _round_up(a, b):
    return (a + b - 1) // b * b


def _largest_divisor(n, candidates):
    for c in candidates:
        if n % c == 0:
            return c
    raise ValueError(f"no tile in {candidates} divides {n}")


def _rmsnorm(x, g):
    return x * lax.rsqrt(jnp.mean(x * x, axis=-1, keepdims=True) + RMS_EPS) * g


def _silu(x):
    return x * jax.nn.sigmoid(x)


def _mm_body(*refs, norm, residual, small, emit):
    it = iter(refs)
    x_ref = next(it)
    g_ref = next(it) if norm else None
    w_ref = next(it)
    w2_ref = next(it) if small else None
    r_ref = next(it) if residual else None
    o_ref = next(it)
    o2_ref = next(it) if small else None
    wb_ref = next(it) if emit else None
    xn_ref = next(it) if norm else None
    first = pl.program_id(1) == 0
    if norm:
        @pl.when(first)
        def _():
            xn_ref[...] = _rmsnorm(x_ref[...], g_ref[...]).astype(BF16)
        lhs = xn_ref[...]
    else:
        lhs = x_ref[...].astype(BF16)
    if small:
        @pl.when(first)
        def _():
            o2_ref[...] = jnp.dot(lhs, w2_ref[...], preferred_element_type=F32)
    w = w_ref[...]
    if emit:
        w = w.astype(BF16)
        wb_ref[...] = w
    acc = jnp.dot(lhs, w, preferred_element_type=F32)
    if residual:
        acc = acc + r_ref[...]
    o_ref[...] = acc.astype(o_ref.dtype)


def _mm_tiles(m, k, n, ns, x_bytes, w_bytes, norm, residual, emit):
    for tm in (1024, 512, 256, 128, 64, 32, 16, 8):
        if m % tm:
            continue
        for tn in (2048, 1280, 1024, 896, 512, 256, 128):
            if n % tn:
                continue
            need = 2 * tm * k * x_bytes + 2 * k * tn * w_bytes + 2 * tm * tn * 4
            need += tm * k * 2 if norm else 0
            need += 2 * tm * tn * 4 if residual else 0
            need += 3 * k * tn * 2 if emit else 0
            need += 2 * k * ns * 2 + 2 * tm * ns * 4
            if need <= VMEM_TILE_BUDGET:
                return tm, tn
    raise ValueError("no matmul tile fits VMEM")


def _w_spec(w, layer, block, index):
    if layer is None:
        return pl.BlockSpec(block, index)
    return pl.BlockSpec((None,) + block, lambda *g: (layer,) + index(*g))


def _matmul(x, w, layer, n, gain=None, res=None, small=None, emit=False):
    m, k = x.shape
    norm, residual, has_small = gain is not None, res is not None, small is not None
    ns = small.shape[1] if has_small else 0
    tm, tn = _mm_tiles(m, k, n, ns, x.dtype.itemsize, w.dtype.itemsize, norm, residual, emit)
    assert not emit or tm == m
    in_specs = [pl.BlockSpec((tm, k), lambda i, j: (i, 0))]
    args = [x]
    if norm:
        in_specs.append(pl.BlockSpec((1, k), lambda i, j: (0, 0)))
        args.append(gain.reshape(1, k))
    in_specs.append(_w_spec(w, layer, (k, tn), lambda i, j: (0, j)))
    args.append(w)
    if has_small:
        in_specs.append(pl.BlockSpec((k, ns), lambda i, j: (0, 0)))
        args.append(small)
    if residual:
        in_specs.append(pl.BlockSpec((tm, tn), lambda i, j: (i, j)))
        args.append(res)
    out_specs = [pl.BlockSpec((tm, tn), lambda i, j: (i, j))]
    out_shape = [jax.ShapeDtypeStruct((m, n), F32)]
    if has_small:
        out_specs.append(pl.BlockSpec((tm, ns), lambda i, j: (i, 0)))
        out_shape.append(jax.ShapeDtypeStruct((m, ns), F32))
    if emit:
        out_specs.append(pl.BlockSpec((k, tn), lambda i, j: (0, j)))
        out_shape.append(jax.ShapeDtypeStruct((k, n), BF16))
    outs = pl.pallas_call(
        functools.partial(_mm_body, norm=norm, residual=residual, small=has_small, emit=emit),
        grid=(m // tm, n // tn),
        in_specs=in_specs,
        out_specs=out_specs,
        out_shape=out_shape,
        scratch_shapes=[pltpu.VMEM((tm, k), BF16)] if norm else [],
        compiler_params=_cparams(2),
        name="matmul",
    )(*args)
    return outs if len(outs) > 1 else outs[0]


def _ffn_body(*refs, tm, shift, halo, tiles_per_seq, nj, final_norm, emit):
    x_ref, gn_ref, wu_ref, wg_ref, cw_ref, cb_ref, wo_ref, buf0_ref, gf_ref, o_ref, bufo_ref = refs[:11]
    wub_ref, wgb_ref, wob_ref = refs[11:14] if emit else (None, None, None)
    xn_ref, gs_ref, carry_ref = refs[-3:]
    i = pl.program_id(0)
    j = pl.program_id(1)

    @pl.when(j == 0)
    def _():
        x = x_ref[...]
        xn_ref[...] = _rmsnorm(x, gn_ref[...]).astype(BF16)
        o_ref[...] = x

    xn = xn_ref[...]
    first = (i % tiles_per_seq) == 0
    gs_ref[0:halo, :] = jnp.where(first, buf0_ref[0], carry_ref[j])
    base = halo - (FFN_CONV - 1) * shift
    tf = wu_ref.shape[1]
    sub = min(tf, FFN_SUB_TILE)
    h_parts = []
    for s0 in range(0, tf, sub):
        cols = slice(s0, s0 + sub)
        wu, wg = wu_ref[:, cols], wg_ref[:, cols]
        if emit:
            wu, wg = wu.astype(BF16), wg.astype(BF16)
            wub_ref[:, cols] = wu
            wgb_ref[:, cols] = wg
        u = jnp.dot(xn, wu, preferred_element_type=F32)
        gs_ref[halo:halo + tm, cols] = jnp.dot(xn, wg, preferred_element_type=F32)
        c = gs_ref[base:base + tm, cols] * cw_ref[0:1, cols]
        for t in range(1, FFN_CONV):
            c = c + gs_ref[base + t * shift:base + t * shift + tm, cols] * cw_ref[t:t + 1, cols]
        c = c + cb_ref[:, cols]
        h_parts.append((_silu(c) * u).astype(BF16))
    tail = gs_ref[tm:tm + halo, :]
    carry_ref[j] = tail
    bufo_ref[0] = tail
    h = h_parts[0] if len(h_parts) == 1 else jnp.concatenate(h_parts, axis=1)
    wo = wo_ref[...]
    if emit:
        wo = wo.astype(BF16)
        wob_ref[...] = wo
    o_ref[...] += jnp.dot(h, wo, preferred_element_type=F32)

    if final_norm:
        @pl.when(j == nj - 1)
        def _():
            o_ref[...] = _rmsnorm(o_ref[...], gf_ref[...])


def _ffn(x, gain, weights, conv_w, conv_b, buf0, shift, rows_per_seq, final_gain):
    m, d = x.shape
    emit = weights[0].ndim == 3
    d_ff = weights[1].shape[1] if emit else weights[2].shape[0]
    tm = _largest_divisor(rows_per_seq, (FFN_ROW_TILE, 512, 256, 128, 64, 32, 16, 8))
    tf = _largest_divisor(d_ff, (FFN_COL_TILE, 256, 128))
    nj = d_ff // tf
    if emit:
        w_in, w_out, layer = weights
        assert m == tm
        w_args = [w_in, w_in, w_out]
        w_specs = [pl.BlockSpec((None, d, tf), lambda i, j: (layer, 0, j)),
                   pl.BlockSpec((None, d, tf), lambda i, j: (layer, 0, nj + j)),
                   pl.BlockSpec((None, tf, d), lambda i, j: (layer, j, 0))]
    else:
        w_args = list(weights)
        w_specs = [pl.BlockSpec((d, tf), lambda i, j: (0, j)),
                   pl.BlockSpec((d, tf), lambda i, j: (0, j)),
                   pl.BlockSpec((tf, d), lambda i, j: (j, 0))]
    tiles_per_seq = rows_per_seq // tm
    pre = (FFN_CONV - 1) * shift
    halo = _round_up(pre, SUBLANES)
    assert tm >= halo and tm % SUBLANES == 0
    buf0p = jnp.pad(buf0, ((0, 0), (halo - pre, 0), (0, 0)))
    final_norm = final_gain is not None
    gf = (final_gain if final_norm else gain).reshape(1, d)
    body = functools.partial(_ffn_body, tm=tm, shift=shift, halo=halo, tiles_per_seq=tiles_per_seq,
                             nj=nj, final_norm=final_norm, emit=emit)
    out_specs = [pl.BlockSpec((tm, d), lambda i, j: (i, 0)),
                 pl.BlockSpec((1, halo, tf), lambda i, j: (i, 0, j))]
    out_shape = [jax.ShapeDtypeStruct((m, d), F32),
                 jax.ShapeDtypeStruct((m // tm, halo, d_ff), F32)]
    if emit:
        out_specs += [pl.BlockSpec((d, tf), lambda i, j: (0, j)),
                      pl.BlockSpec((d, tf), lambda i, j: (0, j)),
                      pl.BlockSpec((tf, d), lambda i, j: (j, 0))]
        out_shape += [jax.ShapeDtypeStruct((d, d_ff), BF16), jax.ShapeDtypeStruct((d, d_ff), BF16),
                      jax.ShapeDtypeStruct((d_ff, d), BF16)]
    outs = pl.pallas_call(
        body,
        grid=(m // tm, nj),
        in_specs=[
            pl.BlockSpec((tm, d), lambda i, j: (i, 0)),
            pl.BlockSpec((1, d), lambda i, j: (0, 0)),
            w_specs[0],
            w_specs[1],
            pl.BlockSpec((FFN_CONV, tf), lambda i, j: (0, j)),
            pl.BlockSpec((1, tf), lambda i, j: (0, j)),
            w_specs[2],
            pl.BlockSpec((1, halo, tf), lambda i, j: (i // tiles_per_seq, 0, j)),
            pl.BlockSpec((1, d), lambda i, j: (0, 0)),
        ],
        out_specs=out_specs,
        out_shape=out_shape,
        scratch_shapes=[pltpu.VMEM((tm, d), BF16),
                        pltpu.VMEM((halo + tm, tf), F32),
                        pltpu.VMEM((nj, halo, tf), F32)],
        compiler_params=_cparams(2),
        name="conv_ffn",
    )(x, gain.reshape(1, d), w_args[0], w_args[1], conv_w, conv_b.reshape(1, d_ff), w_args[2], buf0p, gf)
    result = (outs[0], outs[1][tiles_per_seq - 1::tiles_per_seq, halo - pre:, :])
    return result + ((tuple(outs[2:]),) if emit else ())


def _row_from_col(col, eye):
    return jnp.sum(jnp.where(eye, col, 0.0), axis=0, keepdims=True)


def _cumsum_rows(x, tril):
    return jnp.dot(tril, x, preferred_element_type=F32, precision=lax.Precision.HIGHEST)


def _mlstm_body(*refs, chunk, dk, dv, has_init):
    q_ref, k_ref, v_ref, o_ref, gate_ref, bias_ref, gh_ref = refs[:7]
    c0_ref, n0_ref, m0_ref = refs[7:10] if has_init else (None, None, None)
    hg_ref, c_ref, n_ref, m_ref = refs[-4:]
    L = chunk
    H = MLSTM_HEADS

    @pl.when(pl.program_id(1) == 0)
    def _():
        if has_init:
            c_ref[...] = c0_ref[...]
            n_ref[...] = n0_ref[...]
            m_ref[...] = m0_ref[...]
        else:
            c_ref[...] = jnp.zeros_like(c_ref)
            n_ref[...] = jnp.zeros_like(n_ref)
            m_ref[...] = jnp.zeros_like(m_ref)

    ri = lax.broadcasted_iota(jnp.int32, (L, L), 0)
    ci = lax.broadcasted_iota(jnp.int32, (L, L), 1)
    causal = ri >= ci
    eye = ri == ci
    tril = causal.astype(F32)

    gates = gate_ref[...] + bias_ref[...]
    x = gates
    log_sig = jnp.minimum(x, 0.0) - jnp.log1p(jnp.exp(-jnp.abs(x)))
    bcum = _cumsum_rows(log_sig, tril)

    n_all = n_ref[0]
    m_all = m_ref[0]
    if L % LANES == 0:
        gates_t, bcum_t = gates.T, bcum.T
        row_of = lambda arr_t, col, lane: arr_t[lane:lane + 1, :]
    else:
        gates_t = bcum_t = None
        row_of = lambda arr_t, col, lane: _row_from_col(col, eye)

    stack = lambda parts: jnp.concatenate(parts, axis=0)
    ig_cols = [gates[:, h:h + 1] for h in range(H)]
    b_cols = [bcum[:, H + h:H + h + 1] for h in range(H)]
    m_prevs = [m_all[h:h + 1, 0:1] for h in range(H)]
    ig_col, b_col = stack(ig_cols), stack(b_cols)
    m_prev = stack([jnp.broadcast_to(mp, (L, 1)) for mp in m_prevs])
    ig_row = stack([jnp.broadcast_to(row_of(gates_t, ig_cols[h], h), (L, L)) for h in range(H)])
    b_row = stack([jnp.broadcast_to(row_of(bcum_t, b_cols[h], H + h), (L, L)) for h in range(H)])
    causal_all = stack([causal] * H)

    log_d = jnp.where(causal_all, b_col - b_row + ig_row, -jnp.inf)
    log_inter = b_col + m_prev
    m_t = jnp.maximum(log_inter, jnp.max(log_d, axis=-1, keepdims=True))
    d = jnp.exp(log_d - m_t)
    inter = jnp.exp(log_inter - m_t)

    qk, qc, qn, kfs, vbs, c_prevs = [], [], [], [], [], []
    for h in range(H):
        qf = q_ref[:, h * dk:(h + 1) * dk] * (dk ** -0.5)
        kf = k_ref[:, h * dk:(h + 1) * dk]
        qb = qf.astype(BF16)
        c_prev = c_ref[0, h]
        qk.append(lax.dot_general(qb, kf.astype(BF16), (((1,), (1,)), ((), ())), preferred_element_type=F32))
        qc.append(jnp.dot(qb, c_prev.astype(BF16), preferred_element_type=F32))
        qn.append(jnp.sum(qf * n_all[h:h + 1, :], axis=-1, keepdims=True))
        kfs.append(kf)
        vbs.append(v_ref[:, h * dv:(h + 1) * dv].astype(BF16))
        c_prevs.append(c_prev)

    s = stack(qk) * d
    sb = s.astype(BF16)
    sv = stack([jnp.dot(sb[h * L:(h + 1) * L], vbs[h], preferred_element_type=F32) for h in range(H)])
    num = inter * stack(qc) + sv
    den = inter * stack(qn) + jnp.sum(s, axis=-1, keepdims=True)
    hh = num / jnp.maximum(jnp.abs(den), jnp.exp(-m_t))
    hn = hh * lax.rsqrt(jnp.mean(hh * hh, axis=-1, keepdims=True) + RMS_EPS)
    for h in range(H):
        lanes = slice(h * dv, (h + 1) * dv)
        out = jax.nn.sigmoid(o_ref[:, lanes]) * (hn[h * L:(h + 1) * L] * gh_ref[:, lanes])
        hg_ref[:, lanes] = out.astype(hg_ref.dtype)

    b_lasts = [bc[L - 1:L, :] for bc in b_cols]
    log_w = stack([jnp.broadcast_to(bl, (L, 1)) for bl in b_lasts]) - b_col + ig_col
    m_news = [jnp.maximum(b_lasts[h] + m_prevs[h], jnp.max(log_w[h * L:(h + 1) * L], axis=0, keepdims=True))
              for h in range(H)]
    w = jnp.exp(log_w - stack([jnp.broadcast_to(mn, (L, 1)) for mn in m_news]))

    head_row = lax.broadcasted_iota(jnp.int32, (H, LANES), 0)
    n_new, m_new_all = n_all, m_all
    for h in range(H):
        decay = jnp.exp(b_lasts[h] + m_prevs[h] - m_news[h])
        kw = kfs[h] * w[h * L:(h + 1) * L]
        upd = lax.dot_general(kw.astype(BF16), vbs[h], (((0,), (0,)), ((), ())), preferred_element_type=F32)
        c_ref[0, h] = decay * c_prevs[h] + upd
        n_row = decay * n_all[h:h + 1, :] + jnp.sum(kw, axis=0, keepdims=True)
        n_new = jnp.where(head_row == h, n_row, n_new)
        m_new_all = jnp.where(head_row == h, m_news[h], m_new_all)
    n_ref[0] = n_new
    m_ref[0] = m_new_all


def _mlstm_mix(proj, gates, bias, g_head, init, layer, B, T, dk, dv):
    H = MLSTM_HEADS
    L = math.gcd(T, MLSTM_CHUNK)
    nc = T // L
    qw, vw = H * dk, H * dv
    assert qw * 2 == vw and vw % LANES == 0
    hg_dtype = BF16 if L % 16 == 0 else F32
    row = lambda b, c: b * nc + c
    c_spec = pl.BlockSpec((None, 1, H, dk, dv), lambda b, c: (layer, b, 0, 0, 0))
    in_specs = [
        pl.BlockSpec((L, qw), lambda b, c: (row(b, c), 0)),
        pl.BlockSpec((L, qw), lambda b, c: (row(b, c), 1)),
        pl.BlockSpec((L, vw), lambda b, c: (row(b, c), 1)),
        pl.BlockSpec((L, vw), lambda b, c: (row(b, c), 2)),
        pl.BlockSpec((L, LANES), lambda b, c: (row(b, c), 0)),
        pl.BlockSpec((1, LANES), lambda b, c: (0, 0)),
        pl.BlockSpec((1, vw), lambda b, c: (0, 0)),
    ]
    args = [proj, proj, proj, proj, gates, bias, g_head.reshape(1, vw)]
    if init is not None:
        c_all, n0, m0 = init
        in_specs += [c_spec,
                     pl.BlockSpec((1, H, dk), lambda b, c: (b, 0, 0)),
                     pl.BlockSpec((1, H, LANES), lambda b, c: (b, 0, 0))]
        args += [c_all, n0, jnp.broadcast_to(m0[:, :, None], (B, H, LANES))]
    hg, c, n, m = pl.pallas_call(
        functools.partial(_mlstm_body, chunk=L, dk=dk, dv=dv, has_init=init is not None),
        grid=(B, nc),
        in_specs=in_specs,
        out_specs=[
            pl.BlockSpec((L, vw), lambda b, c: (row(b, c), 0)),
            pl.BlockSpec((1, H, dk, dv), lambda b, c: (b, 0, 0, 0)),
            pl.BlockSpec((1, H, dk), lambda b, c: (b, 0, 0)),
            pl.BlockSpec((1, H, LANES), lambda b, c: (b, 0, 0)),
        ],
        out_shape=[jax.ShapeDtypeStruct((B * T, vw), hg_dtype),
                   jax.ShapeDtypeStruct((B, H, dk, dv), F32),
                   jax.ShapeDtypeStruct((B, H, dk), F32),
                   jax.ShapeDtypeStruct((B, H, LANES), F32)],
        compiler_params=_cparams(2),
        name="mlstm",
    )(*args)
    return hg, c, n, m[:, :, 0]


def _swa_body(sink_ref, q_ref, kp_ref, kc_ref, vp_ref, vc_ref, o_ref, *, tq, hd, mask_first):
    W = WINDOW
    G = SWA_GROUP
    S = 2 * W
    kk = jnp.concatenate([kp_ref[...], kc_ref[...]], axis=0)
    vv = jnp.concatenate([vp_ref[...], vc_ref[...]], axis=0)
    qi = lax.broadcasted_iota(jnp.int32, (tq, S), 0)
    kj = lax.broadcasted_iota(jnp.int32, (tq, S), 1)
    dist = W + qi - kj
    valid = (dist >= 0) & (dist <= W)
    if mask_first:
        valid = valid & ((kj >= W) | (pl.program_id(1) > 0))
    distf = dist.astype(F32)
    lane_s = lax.broadcasted_iota(jnp.int32, (S, LANES), 1)
    lane_q = lax.broadcasted_iota(jnp.int32, (tq, LANES), 1)
    low_s = lane_s < hd
    low_q = lane_q < hd
    heads_per_slab = LANES // hd
    assert heads_per_slab == 2

    for kh in range(SWA_KV_HEADS):
        slab = (kh // heads_per_slab) * LANES
        keep_low = (kh % heads_per_slab) == 0
        k128 = kk[:, slab:slab + LANES]
        v128 = vv[:, slab:slab + LANES]
        k_rot = pltpu.roll(k128, hd, axis=1)
        v_rot = pltpu.roll(v128, hd, axis=1)
        own = low_s if keep_low else jnp.logical_not(low_s)
        k2 = jnp.where(own, k128, k_rot).astype(BF16)
        v2 = jnp.where(own, v128, v_rot).astype(BF16)

        q_parts = []
        for g in range(G):
            h = kh * G + g
            qs = q_ref[:, (h // 2) * LANES:(h // 2 + 1) * LANES]
            keep = low_q if h % 2 == 0 else jnp.logical_not(low_q)
            q_parts.append(jnp.where(keep, qs, 0.0))
        q_stack = jnp.concatenate(q_parts, axis=0).astype(BF16)
        s_all = lax.dot_general(q_stack, k2, (((1,), (1,)), ((), ())), preferred_element_type=F32)
        s_all = s_all * (hd ** -0.5)

        p_parts = []
        for g in range(G):
            h = kh * G + g
            slope = 2.0 ** (-8.0 * (h + 1) / SWA_HEADS)
            sink = sink_ref[h]
            s = s_all[g * tq:(g + 1) * tq, :] - slope * distf
            s = jnp.where(valid, s, -jnp.inf)
            mx = jnp.maximum(jnp.max(s, axis=-1, keepdims=True), sink)
            p = jnp.exp(s - mx)
            p = p / (jnp.sum(p, axis=-1, keepdims=True) + jnp.exp(sink - mx))
            p_parts.append(p)
        p_stack = jnp.concatenate(p_parts, axis=0).astype(BF16)
        o_all = jnp.dot(p_stack, v2, preferred_element_type=F32)

        for g2 in range(G // 2):
            h = kh * G + 2 * g2
            o_even = o_all[(2 * g2) * tq:(2 * g2 + 1) * tq, :]
            o_odd = o_all[(2 * g2 + 1) * tq:(2 * g2 + 2) * tq, :]
            o_ref[:, (h // 2) * LANES:(h // 2 + 1) * LANES] = jnp.where(low_q, o_even, o_odd).astype(o_ref.dtype)


def _swa_call(sinks, q_arr, q_blk, k_arr, kp_map, kc_map, v_arr, vp_map, vc_map, grid, tq, n_rows, hd,
              mask_first):
    kvw = SWA_KV_HEADS * hd
    qw = SWA_HEADS * hd
    out_dtype = BF16 if tq % 16 == 0 else F32
    return pl.pallas_call(
        functools.partial(_swa_body, tq=tq, hd=hd, mask_first=mask_first),
        grid=grid,
        in_specs=[
            pl.BlockSpec(memory_space=pltpu.SMEM),
            pl.BlockSpec((tq, qw), q_blk),
            pl.BlockSpec((WINDOW, kvw), kp_map),
            pl.BlockSpec((WINDOW, kvw), kc_map),
            pl.BlockSpec((WINDOW, kvw), vp_map),
            pl.BlockSpec((WINDOW, kvw), vc_map),
        ],
        out_specs=pl.BlockSpec((tq, qw), q_blk),
        out_shape=jax.ShapeDtypeStruct((n_rows, qw), out_dtype),
        compiler_params=_cparams(2),
        name="swa",
    )(sinks, q_arr, k_arr, k_arr, v_arr, v_arr)


def _conv_silu(u_ref, init_ref, carry_ref, gs_ref, cw_ref, cb_ref, first, L):
    halo = SUBLANES
    gs_ref[0:halo, :] = jnp.where(first, init_ref[0], carry_ref[...])
    gs_ref[halo:halo + L, :] = u_ref[...]
    carry_ref[...] = gs_ref[L:L + halo, :]
    base = halo - (SSD_CONV - 1)
    cw = cw_ref[...]
    c = gs_ref[base:base + L, :] * cw[0:1, :]
    for t in range(1, SSD_CONV):
        c = c + gs_ref[base + t:base + t + L, :] * cw[t:t + 1, :]
    c = c + cb_ref[...]
    return _silu(c)


def _split3_bf16(x):
    hi = x.astype(BF16)
    r = x - hi.astype(F32)
    mid = r.astype(BF16)
    lo = (r - mid.astype(F32)).astype(BF16)
    return hi, mid, lo


def _ssd_body(z_ref, xs_ref, bc_ref, dt_ref, dtb_ref, alog_ref, ex_ref, dsk_ref, gn_ref,
              cwx_ref, cwbc_ref, cbx_ref, cbbc_ref, ix_ref, ibc_ref, s0_ref,
              y_ref, s_ref,
              gx_ref, gbc_ref, kx_ref, kbc_ref, *, chunk, groups, hpg, p_dim):
    L = chunk
    N = SSD_STATE
    gw = hpg * p_dim
    first = pl.program_id(1) == 0

    @pl.when(first)
    def _():
        s_ref[...] = s0_ref[...]

    xs_all = _conv_silu(xs_ref, ix_ref, kx_ref, gx_ref, cwx_ref, cbx_ref, first, L)
    bc_all = _conv_silu(bc_ref, ibc_ref, kbc_ref, gbc_ref, cwbc_ref, cbbc_ref, first, L)

    ri = lax.broadcasted_iota(jnp.int32, (L, L), 0)
    ci = lax.broadcasted_iota(jnp.int32, (L, L), 1)
    causal = ri >= ci
    eye = ri == ci
    tril = causal.astype(F32)

    dt = jax.nn.softplus(dt_ref[...] + dtb_ref[...])
    acum = _cumsum_rows(dt * (-jnp.exp(alog_ref[...])), tril)
    a_last = acum[L - 1:L, :]
    eacum = jnp.exp(acum)
    dtw = dt * jnp.exp(a_last - acum)
    ea_last = jnp.exp(a_last)
    acum_t = acum.T if L % LANES == 0 else None

    per_head = jnp.concatenate([dt, eacum, dtw], axis=0)
    ex = jnp.dot(jnp.concatenate(_split3_bf16(per_head), axis=1), ex_ref[...],
                 preferred_element_type=F32)

    lane = lax.broadcasted_iota(jnp.int32, (L, LANES), 1)
    low = lane < p_dim
    srow = lax.broadcasted_iota(jnp.int32, (LANES, 1), 0)
    heads_per_slab = LANES // p_dim
    assert heads_per_slab == 2

    gated = []
    for g in range(groups):
        lanes_g = slice(g * gw, (g + 1) * gw)
        dt_e, ea_e, dtw_e = ex[0:L, lanes_g], ex[L:2 * L, lanes_g], ex[2 * L:3 * L, lanes_g]

        xs = xs_all[:, g * gw:(g + 1) * gw]
        bmb = bc_all[:, g * N:(g + 1) * N].astype(BF16)
        cmb = bc_all[:, (groups + g) * N:(groups + g + 1) * N].astype(BF16)
        cb = lax.dot_general(cmb, bmb, (((1,), (1,)), ((), ())), preferred_element_type=F32)
        xdt = (xs * dt_e).astype(BF16)
        xw = (xs * dtw_e).astype(BF16)
        s_prev = s_ref[0, g * gw:(g + 1) * gw, :]
        y_inter = lax.dot_general(cmb, s_prev.astype(BF16), (((1,), (1,)), ((), ())),
                                  preferred_element_type=F32)
        upd = lax.dot_general(xw, bmb, (((0,), (0,)), ((), ())), preferred_element_type=F32)

        y_parts = []
        for pr in range(hpg // heads_per_slab):
            h0 = g * hpg + 2 * pr
            ms = []
            for hh in (h0, h0 + 1):
                a_col = acum[:, hh:hh + 1]
                a_row = acum_t[hh:hh + 1, :] if acum_t is not None else _row_from_col(a_col, eye)
                dec = jnp.exp(jnp.where(causal, a_col - a_row, -jnp.inf))
                ms.append((cb * dec).astype(BF16))
            mcat = jnp.concatenate(ms, axis=1)
            xb = xdt[:, pr * LANES:(pr + 1) * LANES]
            zero = jnp.zeros_like(xb)
            bd = jnp.concatenate([jnp.where(low, xb, zero), jnp.where(low, zero, xb)], axis=0)
            y_parts.append(jnp.dot(mcat, bd, preferred_element_type=F32))

            scale = jnp.where(srow < p_dim, ea_last[:, h0:h0 + 1], ea_last[:, h0 + 1:h0 + 2])
            rows = slice(g * gw + pr * LANES, g * gw + (pr + 1) * LANES)
            s_ref[0, rows, :] = s_prev[pr * LANES:(pr + 1) * LANES] * scale + upd[pr * LANES:(pr + 1) * LANES]

        y = jnp.concatenate(y_parts, axis=1) + y_inter * ea_e
        y = y + dsk_ref[:, g * gw:(g + 1) * gw] * xs
        gated.append(y * _silu(z_ref[:, g * gw:(g + 1) * gw]))

    sq = gated[0] * gated[0]
    for y in gated[1:]:
        sq = sq + y * y
    inv = lax.rsqrt(jnp.sum(sq, axis=-1, keepdims=True) * (1.0 / (groups * gw)) + RMS_EPS)
    for g, y in enumerate(gated):
        y_ref[:, g * gw:(g + 1) * gw] = (y * inv * gn_ref[:, g * gw:(g + 1) * gw]).astype(y_ref.dtype)


def _ssd_mix(proj, dt, d_inner, conv_w, conv_b, dt_bias, a_log, d_skip, g_norm, s0, conv0, B, T):
    G, P, N = SSD_GROUPS, SSD_HEAD_DIM, SSD_STATE
    heads = d_inner // P
    hpg = heads // G
    L = math.gcd(T, SSD_CHUNK)
    nc = T // L
    bcw = 2 * G * N
    dtw = LANES
    conv_dim = d_inner + bcw
    assert d_inner % bcw == 0 and heads <= LANES
    bc_blk = (2 * d_inner) // bcw
    pad_heads = lambda v: jnp.pad(v, (0, LANES - heads)).reshape(1, LANES)
    expand = (jnp.arange(3 * LANES)[:, None] % LANES == jnp.arange(d_inner)[None, :] // P).astype(BF16)
    halo = SUBLANES
    conv0p = jnp.pad(conv0, ((0, 0), (halo - (SSD_CONV - 1), 0), (0, 0)))
    conv_b2 = conv_b.reshape(1, conv_dim)
    dsk = jnp.repeat(d_skip, P).reshape(1, d_inner)
    row = lambda b, c: b * nc + c
    const = lambda b, c: (0, 0)
    y, s = pl.pallas_call(
        functools.partial(_ssd_body, chunk=L, groups=G, hpg=hpg, p_dim=P),
        grid=(B, nc),
        in_specs=[
            pl.BlockSpec((L, d_inner), lambda b, c: (row(b, c), 0)),
            pl.BlockSpec((L, d_inner), lambda b, c: (row(b, c), 1)),
            pl.BlockSpec((L, bcw), lambda b, c: (row(b, c), bc_blk)),
            pl.BlockSpec((L, dtw), lambda b, c: (row(b, c), 0)),
            pl.BlockSpec((1, dtw), const),
            pl.BlockSpec((1, dtw), const),
            pl.BlockSpec((3 * LANES, d_inner), const),
            pl.BlockSpec((1, d_inner), const),
            pl.BlockSpec((1, d_inner), const),
            pl.BlockSpec((SSD_CONV, d_inner), const),
            pl.BlockSpec((SSD_CONV, bcw), lambda b, c: (0, d_inner // bcw)),
            pl.BlockSpec((1, d_inner), const),
            pl.BlockSpec((1, bcw), lambda b, c: (0, d_inner // bcw)),
            pl.BlockSpec((1, halo, d_inner), lambda b, c: (b, 0, 0)),
            pl.BlockSpec((1, halo, bcw), lambda b, c: (b, 0, d_inner // bcw)),
            pl.BlockSpec((1, d_inner, N), lambda b, c: (b, 0, 0)),
        ],
        out_specs=[
            pl.BlockSpec((L, d_inner), lambda b, c: (row(b, c), 0)),
            pl.BlockSpec((1, d_inner, N), lambda b, c: (b, 0, 0)),
        ],
        out_shape=[jax.ShapeDtypeStruct((B * T, d_inner), BF16 if L % 16 == 0 else F32),
                   jax.ShapeDtypeStruct((B, d_inner, N), F32)],
        scratch_shapes=[pltpu.VMEM((halo + L, d_inner), F32), pltpu.VMEM((halo + L, bcw), F32),
                        pltpu.VMEM((halo, d_inner), F32), pltpu.VMEM((halo, bcw), F32)],
        compiler_params=_cparams(2),
        name="ssd",
    )(proj, proj, proj, dt, pad_heads(dt_bias), pad_heads(a_log), expand, dsk, g_norm.reshape(1, d_inner),
      conv_w, conv_w, conv_b2, conv_b2, conv0p, conv0p, s0.reshape(B, d_inner, N))
    return y, s.reshape(B, heads, P, N)


def _gate_weight(w):
    gates = 2 * MLSTM_HEADS
    return jnp.pad(w[:, -gates:], ((0, 0), (0, LANES - gates))).astype(BF16)


def _dt_weight(w, d_inner):
    heads = d_inner // SSD_HEAD_DIM
    return jnp.pad(w[:, -heads:], ((0, 0), (0, LANES - heads))).astype(BF16)


def _trunk(x3, st, W, E):
    B, T, D = x3.shape
    fresh = st is None

    def mm(name, j, n, xin, **kw):
        if fresh:
            return _matmul(xin, E[name, j], None, n, **kw)
        outs = _matmul(xin, W[name], j, n, emit=True, **kw)
        E[name, j] = outs[-1]
        return outs[0] if len(outs) == 2 else outs[:-1]

    x = x3.reshape(B * T, D)
    depth = W["norm_mix"].shape[0]
    o_c, o_n, o_m, o_k, o_v, o_s, o_cv, o_f = [], [], [], [], [], [], [], []
    for i in range(depth):
        j = i // N_MIXERS
        kind = i % N_MIXERS
        if kind == 0:
            dv = D // MLSTM_HEADS
            dk = dv // 2
            init = None if fresh else (st["c"], st["n"][j], st["m"][j])
            proj, gates = mm("w_in_a", j, 2 * MLSTM_HEADS * (dk + dv), x, gain=W["norm_mix"][i],
                             small=_gate_weight(W["w_in_a"][j]))
            bias = jnp.pad(jnp.concatenate([W["b_ig_a"][j], W["b_fg_a"][j]]), (0, LANES - 2 * MLSTM_HEADS))
            hg, c, n, m = _mlstm_mix(proj, gates, bias.reshape(1, LANES), W["g_head_a"][j], init, j,
                                     B, T, dk, dv)
            x = mm("w_out_a", j, D, hg, res=x)
            o_c.append(c); o_n.append(n); o_m.append(m)
        elif kind == 1:
            hd = D // SWA_HEADS
            kvw = SWA_KV_HEADS * hd
            qw = SWA_HEADS * hd
            proj = mm("w_in_b", j, qw + 2 * kvw, x, gain=W["norm_mix"][i])
            p3 = proj.reshape(B, T, qw + 2 * kvw)
            k_new, v_new = p3[:, :, qw:qw + kvw], p3[:, :, qw + kvw:]
            kb, vb = qw // kvw, qw // kvw + 1
            if fresh:
                nb = T // WINDOW
                prev = lambda b, n: (b * nb + jnp.maximum(n - 1, 0), kb)
                cur = lambda b, n: (b * nb + n, kb)
                prev_v = lambda b, n: (b * nb + jnp.maximum(n - 1, 0), vb)
                cur_v = lambda b, n: (b * nb + n, vb)
                o = _swa_call(W["sinks_b"][j], proj, lambda b, n: (b * nb + n, 0), proj, prev, cur,
                              proj, prev_v, cur_v, (B, nb), WINDOW, B * T, hd, True)
                kc, vc = k_new[:, -WINDOW:], v_new[:, -WINDOW:]
            else:
                kbuf = st["k"][j].reshape(B, WINDOW, kvw)
                vbuf = st["v"][j].reshape(B, WINDOW, kvw)
                zpad = jnp.zeros((B, WINDOW - T, kvw), F32)
                kk = jnp.concatenate([kbuf, k_new, zpad], axis=1).reshape(B * 2 * WINDOW, kvw)
                vv = jnp.concatenate([vbuf, v_new, zpad], axis=1).reshape(B * 2 * WINDOW, kvw)
                o = _swa_call(W["sinks_b"][j], proj, lambda b, n: (b, 0), kk, lambda b, n: (2 * b, 0),
                              lambda b, n: (2 * b + 1, 0), vv, lambda b, n: (2 * b, 0),
                              lambda b, n: (2 * b + 1, 0), (B, 1), T, B * T, hd, False)
                kc = jnp.concatenate([kbuf[:, T:], k_new], axis=1)
                vc = jnp.concatenate([vbuf[:, T:], v_new], axis=1)
            x = mm("w_out_b", j, D, o, res=x)
            o_k.append(kc.reshape(B, WINDOW, SWA_KV_HEADS, hd))
            o_v.append(vc.reshape(B, WINDOW, SWA_KV_HEADS, hd))
        else:
            d_inner = W["w_out_c"].shape[1]
            heads = d_inner // SSD_HEAD_DIM
            conv_dim = d_inner + 2 * SSD_GROUPS * SSD_STATE
            if fresh:
                s0 = jnp.zeros((B, heads, SSD_HEAD_DIM, SSD_STATE), F32)
                cb0 = jnp.zeros((B, SSD_CONV - 1, conv_dim), F32)
            else:
                s0, cb0 = st["ssm"][j], st["conv"][j]
            proj, dt = mm("w_in_c", j, d_inner + conv_dim, x, gain=W["norm_mix"][i],
                          small=_dt_weight(W["w_in_c"][j], d_inner))
            y, s = _ssd_mix(proj, dt, d_inner, W["conv_w_c"][j], W["conv_b_c"][j], W["dt_bias_c"][j],
                            W["a_log_c"][j], W["d_skip_c"][j], W["g_norm_c"][j], s0, cb0, B, T)
            x = mm("w_out_c", j, D, y, res=x)
            xbc = proj.reshape(B, T, -1)[:, -(SSD_CONV - 1):, d_inner:d_inner + conv_dim]
            o_s.append(s); o_cv.append(xbc)

        final_gain = W["norm_final"] if i == depth - 1 else None
        d_ff = W["w_ffn_out"].shape[1]
        if fresh:
            fb0 = jnp.zeros((B, FFN_CONV - 1, d_ff), F32)
            x, fb = _ffn(x, W["norm_ffn"][i], E["w_ffn", i], W["ffn_conv_w"][i], W["ffn_conv_b"][i],
                         fb0, 1, T, final_gain)
        else:
            xt = x.reshape(B, T, D).transpose(1, 0, 2).reshape(T * B, D)
            fb0 = st["ffn"][i].transpose(1, 0, 2).reshape(1, (FFN_CONV - 1) * B, d_ff)
            xt, fb, E["w_ffn", i] = _ffn(xt, W["norm_ffn"][i], (W["w_ffn_in"], W["w_ffn_out"], i),
                                         W["ffn_conv_w"][i], W["ffn_conv_b"][i], fb0, B, T * B, final_gain)
            x = xt.reshape(T, B, D).transpose(1, 0, 2).reshape(B * T, D)
            fb = fb.reshape(FFN_CONV - 1, B, d_ff).transpose(1, 0, 2)
        o_f.append(fb)
    return (x.reshape(B, T, D), jnp.stack(o_c), jnp.stack(o_n), jnp.stack(o_m), jnp.stack(o_k),
            jnp.stack(o_v), jnp.stack(o_s), jnp.stack(o_cv), jnp.stack(o_f))


def kernel(x_prompt, x_sample, state_mlstm_c, state_mlstm_n, state_mlstm_m, cache_swa_k, cache_swa_v, state_ssm, state_ssm_conv, state_ffn_conv, norm_mix, norm_ffn, norm_final, w_in_a, b_ig_a, b_fg_a, g_head_a, w_out_a, w_in_b, sinks_b, w_out_b, w_in_c, conv_w_c, conv_b_c, dt_bias_c, a_log_c, d_skip_c, g_norm_c, w_out_c, w_ffn_in, ffn_conv_w, ffn_conv_b, w_ffn_out):
    W = dict(norm_mix=norm_mix, norm_ffn=norm_ffn, norm_final=norm_final,
             w_in_a=w_in_a, b_ig_a=b_ig_a, b_fg_a=b_fg_a, g_head_a=g_head_a, w_out_a=w_out_a,
             w_in_b=w_in_b, sinks_b=sinks_b, w_out_b=w_out_b,
             w_in_c=w_in_c, conv_w_c=conv_w_c, conv_b_c=conv_b_c, dt_bias_c=dt_bias_c, a_log_c=a_log_c,
             d_skip_c=d_skip_c, g_norm_c=g_norm_c, w_out_c=w_out_c,
             w_ffn_in=w_ffn_in, ffn_conv_w=ffn_conv_w, ffn_conv_b=ffn_conv_b, w_ffn_out=w_ffn_out)
    st = dict(c=state_mlstm_c, n=state_mlstm_n, m=state_mlstm_m, k=cache_swa_k, v=cache_swa_v,
              ssm=state_ssm, conv=state_ssm_conv, ffn=state_ffn_conv)
    E = {}
    ys, sc, sn, sm, sk, sv, ss, scv, sf = _trunk(x_sample, st, W, E)
    yp, pc, pn, pm, pk, pv, ps, pcv, pf = _trunk(x_prompt, None, W, E)
    return (yp, ys, pc, sc, pn, sn, pm, sm, pk, sk, pv, sv, ps, ss, pcv, scv, pf, sf)
```

```python
import functools
import math

import jax
import jax.numpy as jnp
from jax import lax
from jax.experimental import pallas as pl
from jax.experimental.pallas import tpu as pltpu

F32 = jnp.float32
BF16 = jnp.bfloat16

RMS_EPS = 1e-6
N_MIXERS = 3

MLSTM_HEADS = 8
SWA_HEADS = 32
SWA_KV_HEADS = 4
SWA_GROUP = SWA_HEADS // SWA_KV_HEADS
WINDOW = 128
SSD_HEAD_DIM = 64
SSD_GROUPS = 8
SSD_STATE = 128
SSD_CONV = 4
FFN_CONV = 3

LANES = 128
SUBLANES = 8
VMEM_LIMIT_BYTES = 56 << 20
VMEM_TILE_BUDGET = 44 << 20

FFN_ROW_TILE = 1024
FFN_COL_TILE = 512
FFN_SUB_TILE = 256
MLSTM_CHUNK = 128
SSD_CHUNK = 128


def _cparams(n_axes):
    return pltpu.CompilerParams(dimension_semantics=("arbitrary",) * n_axes,
                                vmem_limit_bytes=VMEM_LIMIT_BYTES)


def _round_up(a, b):
    return (a + b - 1) // b * b


def _largest_divisor(n, candidates):
    for c in candidates:
        if n % c == 0:
            return c
    raise ValueError(f"no tile in {candidates} divides {n}")


def _rmsnorm(x, g):
    return x * lax.rsqrt(jnp.mean(x * x, axis=-1, keepdims=True) + RMS_EPS) * g


def _silu(x):
    return x * jax.nn.sigmoid(x)


def _mm_body(*refs, norm, residual, small, emit):
    it = iter(refs)
    x_ref = next(it)
    g_ref = next(it) if norm else None
    w_ref = next(it)
    w2_ref = next(it) if small else None
    r_ref = next(it) if residual else None
    o_ref = next(it)
    o2_ref = next(it) if small else None
    wb_ref = next(it) if emit else None
    xn_ref = next(it) if norm else None
    first = pl.program_id(1) == 0
    if norm:
        @pl.when(first)
        def _():
            xn_ref[...] = _rmsnorm(x_ref[...], g_ref[...]).astype(BF16)
        lhs = xn_ref[...]
    else:
        lhs = x_ref[...].astype(BF16)
    if small:
        @pl.when(first)
        def _():
            o2_ref[...] = jnp.dot(lhs, w2_ref[...].astype(BF16), preferred_element_type=F32)
    w = w_ref[...]
    if emit:
        w = w.astype(BF16)
        wb_ref[...] = w
    acc = jnp.dot(lhs, w, preferred_element_type=F32)
    if residual:
        acc = acc + r_ref[...]
    o_ref[...] = acc.astype(o_ref.dtype)


def _mm_tiles(m, k, n, ns, x_bytes, w_bytes, norm, residual, emit):
    for tm in (1024, 512, 256, 128, 64, 32, 16, 8):
        if m % tm:
            continue
        for tn in (2048, 1280, 1024, 896, 512, 256, 128):
            if n % tn:
                continue
            need = 2 * tm * k * x_bytes + 2 * k * tn * w_bytes + 2 * tm * tn * 4
            need += tm * k * 2 if norm else 0
            need += 2 * tm * tn * 4 if residual else 0
            need += 3 * k * tn * 2 if emit else 0
            need += 2 * k * ns * 4 + 2 * tm * ns * 4
            if need <= VMEM_TILE_BUDGET:
                return tm, tn
    raise ValueError("no matmul tile fits VMEM")


def _w_spec(w, layer, block, index):
    if layer is None:
        return pl.BlockSpec(block, index)
    return pl.BlockSpec((None,) + block, lambda *g: (layer,) + index(*g))


def _matmul(x, w, layer, n, gain=None, res=None, small=None, emit=False):
    m, k = x.shape
    norm, residual, has_small = gain is not None, res is not None, small is not None
    ns = small.shape[1] if has_small else 0
    tm, tn = _mm_tiles(m, k, n, ns, x.dtype.itemsize, w.dtype.itemsize, norm, residual, emit)
    assert not emit or tm == m
    in_specs = [pl.BlockSpec((tm, k), lambda i, j: (i, 0))]
    args = [x]
    if norm:
        in_specs.append(pl.BlockSpec((1, k), lambda i, j: (0, 0)))
        args.append(gain.reshape(1, k))
    in_specs.append(_w_spec(w, layer, (k, tn), lambda i, j: (0, j)))
    args.append(w)
    if has_small:
        in_specs.append(pl.BlockSpec((k, ns), lambda i, j: (0, 0)))
        args.append(small)
    if residual:
        in_specs.append(pl.BlockSpec((tm, tn), lambda i, j: (i, j)))
        args.append(res)
    out_specs = [pl.BlockSpec((tm, tn), lambda i, j: (i, j))]
    out_shape = [jax.ShapeDtypeStruct((m, n), F32)]
    if has_small:
        out_specs.append(pl.BlockSpec((tm, ns), lambda i, j: (i, 0)))
        out_shape.append(jax.ShapeDtypeStruct((m, ns), F32))
    if emit:
        out_specs.append(pl.BlockSpec((k, tn), lambda i, j: (0, j)))
        out_shape.append(jax.ShapeDtypeStruct((k, n), BF16))
    outs = pl.pallas_call(
        functools.partial(_mm_body, norm=norm, residual=residual, small=has_small, emit=emit),
        grid=(m // tm, n // tn),
        in_specs=in_specs,
        out_specs=out_specs,
        out_shape=out_shape,
        scratch_shapes=[pltpu.VMEM((tm, k), BF16)] if norm else [],
        compiler_params=_cparams(2),
        name="matmul",
    )(*args)
    return outs if len(outs) > 1 else outs[0]


def _ffn_body(*refs, tm, shift, halo, tiles_per_seq, nj, final_norm, emit):
    x_ref, gn_ref, wu_ref, wg_ref, cw_ref, cb_ref, wo_ref, buf0_ref, gf_ref, o_ref, bufo_ref = refs[:11]
    wub_ref, wgb_ref, wob_ref = refs[11:14] if emit else (None, None, None)
    xn_ref, gs_ref, carry_ref = refs[-3:]
    i = pl.program_id(0)
    j = pl.program_id(1)

    @pl.when(j == 0)
    def _():
        x = x_ref[...]
        xn_ref[...] = _rmsnorm(x, gn_ref[...]).astype(BF16)
        o_ref[...] = x

    first = (i % tiles_per_seq) == 0
    gs_ref[0:halo, :] = jnp.where(first, buf0_ref[0], carry_ref[j])
    base = halo - (FFN_CONV - 1) * shift
    xn = xn_ref[...]
    tf = wu_ref.shape[1]
    sub = min(tf, FFN_SUB_TILE)
    h_parts = []
    for s0 in range(0, tf, sub):
        cols = slice(s0, s0 + sub)
        wu, wg = wu_ref[:, cols], wg_ref[:, cols]
        if emit:
            wu, wg = wu.astype(BF16), wg.astype(BF16)
            wub_ref[:, cols] = wu
            wgb_ref[:, cols] = wg
        u = jnp.dot(xn, wu, preferred_element_type=F32)
        gs_ref[halo:halo + tm, cols] = jnp.dot(xn, wg, preferred_element_type=F32)
        c = gs_ref[base:base + tm, cols] * cw_ref[0:1, cols]
        for t in range(1, FFN_CONV):
            c = c + gs_ref[base + t * shift:base + t * shift + tm, cols] * cw_ref[t:t + 1, cols]
        c = c + cb_ref[:, cols]
        h_parts.append((_silu(c) * u).astype(BF16))
    tail = gs_ref[tm:tm + halo, :]
    carry_ref[j] = tail
    bufo_ref[0] = tail
    h = h_parts[0] if len(h_parts) == 1 else jnp.concatenate(h_parts, axis=1)
    wo = wo_ref[...]
    if emit:
        wo = wo.astype(BF16)
        wob_ref[...] = wo
    o_ref[...] += jnp.dot(h, wo, preferred_element_type=F32)

    if final_norm:
        @pl.when(j == nj - 1)
        def _():
            o_ref[...] = _rmsnorm(o_ref[...], gf_ref[...])


def _ffn(x, gain, weights, conv_w, conv_b, buf0, shift, rows_per_seq, final_gain):
    m, d = x.shape
    emit = weights[0].ndim == 3
    d_ff = weights[1].shape[1] if emit else weights[2].shape[0]
    tm = _largest_divisor(rows_per_seq, (FFN_ROW_TILE, 512, 256, 128, 64, 32, 16, 8))
    tf = _largest_divisor(d_ff, (FFN_COL_TILE, 256, 128))
    nj = d_ff // tf
    if emit:
        w_in, w_out, layer = weights
        assert m == tm
        w_args = [w_in, w_in, w_out]
        w_specs = [pl.BlockSpec((None, d, tf), lambda i, j: (layer, 0, j)),
                   pl.BlockSpec((None, d, tf), lambda i, j: (layer, 0, nj + j)),
                   pl.BlockSpec((None, tf, d), lambda i, j: (layer, j, 0))]
    else:
        w_args = list(weights)
        w_specs = [pl.BlockSpec((d, tf), lambda i, j: (0, j)),
                   pl.BlockSpec((d, tf), lambda i, j: (0, j)),
                   pl.BlockSpec((tf, d), lambda i, j: (j, 0))]
    tiles_per_seq = rows_per_seq // tm
    pre = (FFN_CONV - 1) * shift
    halo = _round_up(pre, SUBLANES)
    assert tm >= halo and tm % SUBLANES == 0
    buf0p = jnp.pad(buf0, ((0, 0), (halo - pre, 0), (0, 0)))
    final_norm = final_gain is not None
    gf = (final_gain if final_norm else gain).reshape(1, d)
    body = functools.partial(_ffn_body, tm=tm, shift=shift, halo=halo, tiles_per_seq=tiles_per_seq,
                             nj=nj, final_norm=final_norm, emit=emit)
    out_specs = [pl.BlockSpec((tm, d), lambda i, j: (i, 0)),
                 pl.BlockSpec((1, halo, tf), lambda i, j: (i, 0, j))]
    out_shape = [jax.ShapeDtypeStruct((m, d), F32),
                 jax.ShapeDtypeStruct((m // tm, halo, d_ff), F32)]
    if emit:
        out_specs += [pl.BlockSpec((d, tf), lambda i, j: (0, j)),
                      pl.BlockSpec((d, tf), lambda i, j: (0, j)),
                      pl.BlockSpec((tf, d), lambda i, j: (j, 0))]
        out_shape += [jax.ShapeDtypeStruct((d, d_ff), BF16), jax.ShapeDtypeStruct((d, d_ff), BF16),
                      jax.ShapeDtypeStruct((d_ff, d), BF16)]
    outs = pl.pallas_call(
        body,
        grid=(m // tm, nj),
        in_specs=[
            pl.BlockSpec((tm, d), lambda i, j: (i, 0)),
            pl.BlockSpec((1, d), lambda i, j: (0, 0)),
            w_specs[0],
            w_specs[1],
            pl.BlockSpec((FFN_CONV, tf), lambda i, j: (0, j)),
            pl.BlockSpec((1, tf), lambda i, j: (0, j)),
            w_specs[2],
            pl.BlockSpec((1, halo, tf), lambda i, j: (i // tiles_per_seq, 0, j)),
            pl.BlockSpec((1, d), lambda i, j: (0, 0)),
        ],
        out_specs=out_specs,
        out_shape=out_shape,
        scratch_shapes=[pltpu.VMEM((tm, d), BF16),
                        pltpu.VMEM((halo + tm, tf), F32),
                        pltpu.VMEM((nj, halo, tf), F32)],
        compiler_params=_cparams(2),
        name="conv_ffn",
    )(x, gain.reshape(1, d), w_args[0], w_args[1], conv_w, conv_b.reshape(1, d_ff), w_args[2], buf0p, gf)
    result = (outs[0], outs[1][tiles_per_seq - 1::tiles_per_seq, halo - pre:, :])
    return result + ((tuple(outs[2:]),) if emit else ())


def _row_from_col(col, eye):
    return jnp.sum(jnp.where(eye, col, 0.0), axis=0, keepdims=True)


def _cumsum_rows(x, tril):
    return jnp.dot(tril, x, preferred_element_type=F32, precision=lax.Precision.HIGHEST)


def _mlstm_body(*refs, chunk, dk, dv, has_init):
    q_ref, k_ref, v_ref, o_ref, gate_ref, bias_ref, gh_ref = refs[:7]
    c0_ref, n0_ref, m0_ref = refs[7:10] if has_init else (None, None, None)
    hg_ref, c_ref, n_ref, m_ref = refs[-4:]
    L = chunk
    H = MLSTM_HEADS

    @pl.when(pl.program_id(1) == 0)
    def _():
        if has_init:
            c_ref[...] = c0_ref[...]
            n_ref[...] = n0_ref[...]
            m_ref[...] = m0_ref[...]
        else:
            c_ref[...] = jnp.zeros_like(c_ref)
            n_ref[...] = jnp.zeros_like(n_ref)
            m_ref[...] = jnp.zeros_like(m_ref)

    ri = lax.broadcasted_iota(jnp.int32, (L, L), 0)
    ci = lax.broadcasted_iota(jnp.int32, (L, L), 1)
    causal = ri >= ci
    eye = ri == ci
    tril = causal.astype(F32)

    gates = gate_ref[...] + bias_ref[...]
    x = gates
    log_sig = jnp.minimum(x, 0.0) - jnp.log1p(jnp.exp(-jnp.abs(x)))
    bcum = _cumsum_rows(log_sig, tril)

    n_all = n_ref[0]
    m_all = m_ref[0]
    if L % LANES == 0:
        gates_t, bcum_t = gates.T, bcum.T
        row_of = lambda arr_t, col, lane: arr_t[lane:lane + 1, :]
    else:
        gates_t = bcum_t = None
        row_of = lambda arr_t, col, lane: _row_from_col(col, eye)

    stack = lambda parts: jnp.concatenate(parts, axis=0)
    ig_cols = [gates[:, h:h + 1] for h in range(H)]
    b_cols = [bcum[:, H + h:H + h + 1] for h in range(H)]
    m_prevs = [m_all[h:h + 1, 0:1] for h in range(H)]
    ig_col, b_col = stack(ig_cols), stack(b_cols)
    m_prev = stack([jnp.broadcast_to(mp, (L, 1)) for mp in m_prevs])
    ig_row = stack([jnp.broadcast_to(row_of(gates_t, ig_cols[h], h), (L, L)) for h in range(H)])
    b_row = stack([jnp.broadcast_to(row_of(bcum_t, b_cols[h], H + h), (L, L)) for h in range(H)])
    causal_all = stack([causal] * H)

    log_d = jnp.where(causal_all, b_col - b_row + ig_row, -jnp.inf)
    log_inter = b_col + m_prev
    m_t = jnp.maximum(log_inter, jnp.max(log_d, axis=-1, keepdims=True))
    d = jnp.exp(log_d - m_t)
    inter = jnp.exp(log_inter - m_t)

    qk, qc, qn, kfs, vbs, c_prevs = [], [], [], [], [], []
    for h in range(H):
        qf = q_ref[:, h * dk:(h + 1) * dk] * (dk ** -0.5)
        kf = k_ref[:, h * dk:(h + 1) * dk]
        qb = qf.astype(BF16)
        c_prev = c_ref[0, h]
        qk.append(lax.dot_general(qb, kf.astype(BF16), (((1,), (1,)), ((), ())), preferred_element_type=F32))
        qc.append(jnp.dot(qb, c_prev.astype(BF16), preferred_element_type=F32))
        qn.append(jnp.sum(qf * n_all[h:h + 1, :], axis=-1, keepdims=True))
        kfs.append(kf)
        vbs.append(v_ref[:, h * dv:(h + 1) * dv].astype(BF16))
        c_prevs.append(c_prev)

    s = stack(qk) * d
    sb = s.astype(BF16)
    sv = stack([jnp.dot(sb[h * L:(h + 1) * L], vbs[h], preferred_element_type=F32) for h in range(H)])
    num = inter * stack(qc) + sv
    den = inter * stack(qn) + jnp.sum(s, axis=-1, keepdims=True)
    hh = num / jnp.maximum(jnp.abs(den), jnp.exp(-m_t))
    hn = hh * lax.rsqrt(jnp.mean(hh * hh, axis=-1, keepdims=True) + RMS_EPS)
    for h in range(H):
        lanes = slice(h * dv, (h + 1) * dv)
        out = jax.nn.sigmoid(o_ref[:, lanes]) * (hn[h * L:(h + 1) * L] * gh_ref[:, lanes])
        hg_ref[:, lanes] = out.astype(hg_ref.dtype)

    b_lasts = [bc[L - 1:L, :] for bc in b_cols]
    log_w = stack([jnp.broadcast_to(bl, (L, 1)) for bl in b_lasts]) - b_col + ig_col
    m_news = [jnp.maximum(b_lasts[h] + m_prevs[h], jnp.max(log_w[h * L:(h + 1) * L], axis=0, keepdims=True))
              for h in range(H)]
    w = jnp.exp(log_w - stack([jnp.broadcast_to(mn, (L, 1)) for mn in m_news]))

    head_row = lax.broadcasted_iota(jnp.int32, (H, LANES), 0)
    n_new, m_new_all = n_all, m_all
    for h in range(H):
        decay = jnp.exp(b_lasts[h] + m_prevs[h] - m_news[h])
        kw = kfs[h] * w[h * L:(h + 1) * L]
        upd = lax.dot_general(kw.astype(BF16), vbs[h], (((0,), (0,)), ((), ())), preferred_element_type=F32)
        c_ref[0, h] = decay * c_prevs[h] + upd
        n_row = decay * n_all[h:h + 1, :] + jnp.sum(kw, axis=0, keepdims=True)
        n_new = jnp.where(head_row == h, n_row, n_new)
        m_new_all = jnp.where(head_row == h, m_news[h], m_new_all)
    n_ref[0] = n_new
    m_ref[0] = m_new_all


def _mlstm_mix(proj, gates, bias, g_head, init, layer, B, T, dk, dv):
    H = MLSTM_HEADS
    L = math.gcd(T, MLSTM_CHUNK)
    nc = T // L
    qw, vw = H * dk, H * dv
    assert qw * 2 == vw and vw % LANES == 0
    hg_dtype = BF16 if L % 16 == 0 else F32
    row = lambda b, c: b * nc + c
    c_spec = pl.BlockSpec((None, 1, H, dk, dv), lambda b, c: (layer, b, 0, 0, 0))
    in_specs = [
        pl.BlockSpec((L, qw), lambda b, c: (row(b, c), 0)),
        pl.BlockSpec((L, qw), lambda b, c: (row(b, c), 1)),
        pl.BlockSpec((L, vw), lambda b, c: (row(b, c), 1)),
        pl.BlockSpec((L, vw), lambda b, c: (row(b, c), 2)),
        pl.BlockSpec((L, LANES), lambda b, c: (row(b, c), 0)),
        pl.BlockSpec((1, LANES), lambda b, c: (0, 0)),
        pl.BlockSpec((1, vw), lambda b, c: (0, 0)),
    ]
    args = [proj, proj, proj, proj, gates, bias, g_head.reshape(1, vw)]
    if init is not None:
        c_all, n0, m0 = init
        in_specs += [c_spec,
                     pl.BlockSpec((1, H, dk), lambda b, c: (b, 0, 0)),
                     pl.BlockSpec((1, H, LANES), lambda b, c: (b, 0, 0))]
        args += [c_all, n0, jnp.broadcast_to(m0[:, :, None], (B, H, LANES))]
    hg, c, n, m = pl.pallas_call(
        functools.partial(_mlstm_body, chunk=L, dk=dk, dv=dv, has_init=init is not None),
        grid=(B, nc),
        in_specs=in_specs,
        out_specs=[
            pl.BlockSpec((L, vw), lambda b, c: (row(b, c), 0)),
            pl.BlockSpec((1, H, dk, dv), lambda b, c: (b, 0, 0, 0)),
            pl.BlockSpec((1, H, dk), lambda b, c: (b, 0, 0)),
            pl.BlockSpec((1, H, LANES), lambda b, c: (b, 0, 0)),
        ],
        out_shape=[jax.ShapeDtypeStruct((B * T, vw), hg_dtype),
                   jax.ShapeDtypeStruct((B, H, dk, dv), F32),
                   jax.ShapeDtypeStruct((B, H, dk), F32),
                   jax.ShapeDtypeStruct((B, H, LANES), F32)],
        compiler_params=_cparams(2),
        name="mlstm",
    )(*args)
    return hg, c, n, m[:, :, 0]


def _swa_body(sink_ref, q_ref, kp_ref, kc_ref, vp_ref, vc_ref, o_ref, *, tq, hd, mask_first):
    W = WINDOW
    G = SWA_GROUP
    S = 2 * W
    kk = jnp.concatenate([kp_ref[...], kc_ref[...]], axis=0)
    vv = jnp.concatenate([vp_ref[...], vc_ref[...]], axis=0)
    qi = lax.broadcasted_iota(jnp.int32, (tq, S), 0)
    kj = lax.broadcasted_iota(jnp.int32, (tq, S), 1)
    dist = W + qi - kj
    valid = (dist >= 0) & (dist <= W)
    if mask_first:
        valid = valid & ((kj >= W) | (pl.program_id(1) > 0))
    distf = dist.astype(F32)
    lane_s = lax.broadcasted_iota(jnp.int32, (S, LANES), 1)
    lane_q = lax.broadcasted_iota(jnp.int32, (tq, LANES), 1)
    low_s = lane_s < hd
    low_q = lane_q < hd
    heads_per_slab = LANES // hd
    assert heads_per_slab == 2

    for kh in range(SWA_KV_HEADS):
        slab = (kh // heads_per_slab) * LANES
        keep_low = (kh % heads_per_slab) == 0
        k128 = kk[:, slab:slab + LANES]
        v128 = vv[:, slab:slab + LANES]
        k_rot = pltpu.roll(k128, hd, axis=1)
        v_rot = pltpu.roll(v128, hd, axis=1)
        own = low_s if keep_low else jnp.logical_not(low_s)
        k2 = jnp.where(own, k128, k_rot).astype(BF16)
        v2 = jnp.where(own, v128, v_rot).astype(BF16)

        q_parts = []
        for g in range(G):
            h = kh * G + g
            qs = q_ref[:, (h // 2) * LANES:(h // 2 + 1) * LANES]
            keep = low_q if h % 2 == 0 else jnp.logical_not(low_q)
            q_parts.append(jnp.where(keep, qs, 0.0))
        q_stack = jnp.concatenate(q_parts, axis=0).astype(BF16)
        s_all = lax.dot_general(q_stack, k2, (((1,), (1,)), ((), ())), preferred_element_type=F32)
        s_all = s_all * (hd ** -0.5)

        p_parts = []
        for g in range(G):
            h = kh * G + g
            slope = 2.0 ** (-8.0 * (h + 1) / SWA_HEADS)
            sink = sink_ref[h]
            s = s_all[g * tq:(g + 1) * tq, :] - slope * distf
            s = jnp.where(valid, s, -jnp.inf)
            mx = jnp.maximum(jnp.max(s, axis=-1, keepdims=True), sink)
            p = jnp.exp(s - mx)
            p = p / (jnp.sum(p, axis=-1, keepdims=True) + jnp.exp(sink - mx))
            p_parts.append(p)
        p_stack = jnp.concatenate(p_parts, axis=0).astype(BF16)
        o_all = jnp.dot(p_stack, v2, preferred_element_type=F32)

        for g2 in range(G // 2):
            h = kh * G + 2 * g2
            o_even = o_all[(2 * g2) * tq:(2 * g2 + 1) * tq, :]
            o_odd = o_all[(2 * g2 + 1) * tq:(2 * g2 + 2) * tq, :]
            o_ref[:, (h // 2) * LANES:(h // 2 + 1) * LANES] = jnp.where(low_q, o_even, o_odd).astype(o_ref.dtype)


def _swa_call(sinks, q_arr, q_blk, k_arr, kp_map, kc_map, v_arr, vp_map, vc_map, grid, tq, n_rows, hd,
              mask_first):
    kvw = SWA_KV_HEADS * hd
    qw = SWA_HEADS * hd
    out_dtype = BF16 if tq % 16 == 0 else F32
    return pl.pallas_call(
        functools.partial(_swa_body, tq=tq, hd=hd, mask_first=mask_first),
        grid=grid,
        in_specs=[
            pl.BlockSpec(memory_space=pltpu.SMEM),
            pl.BlockSpec((tq, qw), q_blk),
            pl.BlockSpec((WINDOW, kvw), kp_map),
            pl.BlockSpec((WINDOW, kvw), kc_map),
            pl.BlockSpec((WINDOW, kvw), vp_map),
            pl.BlockSpec((WINDOW, kvw), vc_map),
        ],
        out_specs=pl.BlockSpec((tq, qw), q_blk),
        out_shape=jax.ShapeDtypeStruct((n_rows, qw), out_dtype),
        compiler_params=_cparams(2),
        name="swa",
    )(sinks, q_arr, k_arr, k_arr, v_arr, v_arr)


def _conv_silu(u_ref, init_ref, carry_ref, gs_ref, cw_ref, cb_ref, first, L):
    halo = SUBLANES
    gs_ref[0:halo, :] = jnp.where(first, init_ref[0], carry_ref[...])
    gs_ref[halo:halo + L, :] = u_ref[...]
    carry_ref[...] = gs_ref[L:L + halo, :]
    base = halo - (SSD_CONV - 1)
    cw = cw_ref[...]
    c = gs_ref[base:base + L, :] * cw[0:1, :]
    for t in range(1, SSD_CONV):
        c = c + gs_ref[base + t:base + t + L, :] * cw[t:t + 1, :]
    c = c + cb_ref[...]
    return _silu(c)


def _split3_bf16(x):
    hi = x.astype(BF16)
    r = x - hi.astype(F32)
    mid = r.astype(BF16)
    lo = (r - mid.astype(F32)).astype(BF16)
    return hi, mid, lo


def _ssd_body(z_ref, xs_ref, bc_ref, dt_ref, dtb_ref, alog_ref, ex_ref, dsk_ref, gn_ref,
              cwx_ref, cwbc_ref, cbx_ref, cbbc_ref, ix_ref, ibc_ref, s0_ref,
              y_ref, s_ref,
              gx_ref, gbc_ref, kx_ref, kbc_ref, *, chunk, groups, hpg, p_dim):
    L = chunk
    N = SSD_STATE
    gw = hpg * p_dim
    first = pl.program_id(1) == 0

    @pl.when(first)
    def _():
        s_ref[...] = s0_ref[...]

    xs_all = _conv_silu(xs_ref, ix_ref, kx_ref, gx_ref, cwx_ref, cbx_ref, first, L)
    bc_all = _conv_silu(bc_ref, ibc_ref, kbc_ref, gbc_ref, cwbc_ref, cbbc_ref, first, L)

    ri = lax.broadcasted_iota(jnp.int32, (L, L), 0)
    ci = lax.broadcasted_iota(jnp.int32, (L, L), 1)
    causal = ri >= ci
    eye = ri == ci
    tril = causal.astype(F32)

    dt = jax.nn.softplus(dt_ref[...] + dtb_ref[...])
    acum = _cumsum_rows(dt * (-jnp.exp(alog_ref[...])), tril)
    a_last = acum[L - 1:L, :]
    eacum = jnp.exp(acum)
    dtw = dt * jnp.exp(a_last - acum)
    ea_last = jnp.exp(a_last)
    acum_t = acum.T if L % LANES == 0 else None

    per_head = jnp.concatenate([dt, eacum, dtw], axis=0)
    ex = jnp.dot(jnp.concatenate(_split3_bf16(per_head), axis=1), ex_ref[...],
                 preferred_element_type=F32)

    lane = lax.broadcasted_iota(jnp.int32, (L, LANES), 1)
    low = lane < p_dim
    srow = lax.broadcasted_iota(jnp.int32, (LANES, 1), 0)
    heads_per_slab = LANES // p_dim
    assert heads_per_slab == 2

    gated = []
    for g in range(groups):
        lanes_g = slice(g * gw, (g + 1) * gw)
        dt_e, ea_e, dtw_e = ex[0:L, lanes_g], ex[L:2 * L, lanes_g], ex[2 * L:3 * L, lanes_g]

        xs = xs_all[:, g * gw:(g + 1) * gw]
        bmb = bc_all[:, g * N:(g + 1) * N].astype(BF16)
        cmb = bc_all[:, (groups + g) * N:(groups + g + 1) * N].astype(BF16)
        cb = lax.dot_general(cmb, bmb, (((1,), (1,)), ((), ())), preferred_element_type=F32)
        xdt = (xs * dt_e).astype(BF16)
        xw = (xs * dtw_e).astype(BF16)
        s_prev = s_ref[0, g * gw:(g + 1) * gw, :]
        y_inter = lax.dot_general(cmb, s_prev.astype(BF16), (((1,), (1,)), ((), ())),
                                  preferred_element_type=F32)
        upd = lax.dot_general(xw, bmb, (((0,), (0,)), ((), ())), preferred_element_type=F32)

        y_parts = []
        for pr in range(hpg // heads_per_slab):
            h0 = g * hpg + 2 * pr
            ms = []
            for hh in (h0, h0 + 1):
                a_col = acum[:, hh:hh + 1]
                a_row = acum_t[hh:hh + 1, :] if acum_t is not None else _row_from_col(a_col, eye)
                dec = jnp.exp(jnp.where(causal, a_col - a_row, -jnp.inf))
                ms.append((cb * dec).astype(BF16))
            mcat = jnp.concatenate(ms, axis=1)
            xb = xdt[:, pr * LANES:(pr + 1) * LANES]
            zero = jnp.zeros_like(xb)
            bd = jnp.concatenate([jnp.where(low, xb, zero), jnp.where(low, zero, xb)], axis=0)
            y_parts.append(jnp.dot(mcat, bd, preferred_element_type=F32))

            scale = jnp.where(srow < p_dim, ea_last[:, h0:h0 + 1], ea_last[:, h0 + 1:h0 + 2])
            rows = slice(g * gw + pr * LANES, g * gw + (pr + 1) * LANES)
            s_ref[0, rows, :] = s_prev[pr * LANES:(pr + 1) * LANES] * scale + upd[pr * LANES:(pr + 1) * LANES]

        y = jnp.concatenate(y_parts, axis=1) + y_inter * ea_e
        y = y + dsk_ref[:, g * gw:(g + 1) * gw] * xs
        gated.append(y * _silu(z_ref[:, g * gw:(g + 1) * gw]))

    sq = gated[0] * gated[0]
    for y in gated[1:]:
        sq = sq + y * y
    inv = lax.rsqrt(jnp.sum(sq, axis=-1, keepdims=True) * (1.0 / (groups * gw)) + RMS_EPS)
    for g, y in enumerate(gated):
        y_ref[:, g * gw:(g + 1) * gw] = (y * inv * gn_ref[:, g * gw:(g + 1) * gw]).astype(y_ref.dtype)


def _ssd_mix(proj, dt, d_inner, conv_w, conv_b, dt_bias, a_log, d_skip, g_norm, s0, conv0, B, T):
    G, P, N = SSD_GROUPS, SSD_HEAD_DIM, SSD_STATE
    heads = d_inner // P
    hpg = heads // G
    L = math.gcd(T, SSD_CHUNK)
    nc = T // L
    bcw = 2 * G * N
    dtw = LANES
    conv_dim = d_inner + bcw
    assert d_inner % bcw == 0 and heads <= LANES
    bc_blk = (2 * d_inner) // bcw
    pad_heads = lambda v: jnp.pad(v, (0, LANES - heads)).reshape(1, LANES)
    expand = (jnp.arange(3 * LANES)[:, None] % LANES == jnp.arange(d_inner)[None, :] // P).astype(BF16)
    halo = SUBLANES
    conv0p = jnp.pad(conv0, ((0, 0), (halo - (SSD_CONV - 1), 0), (0, 0)))
    conv_b2 = conv_b.reshape(1, conv_dim)
    dsk = jnp.repeat(d_skip, P).reshape(1, d_inner)
    row = lambda b, c: b * nc + c
    const = lambda b, c: (0, 0)
    y, s = pl.pallas_call(
        functools.partial(_ssd_body, chunk=L, groups=G, hpg=hpg, p_dim=P),
        grid=(B, nc),
        in_specs=[
            pl.BlockSpec((L, d_inner), lambda b, c: (row(b, c), 0)),
            pl.BlockSpec((L, d_inner), lambda b, c: (row(b, c), 1)),
            pl.BlockSpec((L, bcw), lambda b, c: (row(b, c), bc_blk)),
            pl.BlockSpec((L, dtw), lambda b, c: (row(b, c), 0)),
            pl.BlockSpec((1, dtw), const),
            pl.BlockSpec((1, dtw), const),
            pl.BlockSpec((3 * LANES, d_inner), const),
            pl.BlockSpec((1, d_inner), const),
            pl.BlockSpec((1, d_inner), const),
            pl.BlockSpec((SSD_CONV, d_inner), const),
            pl.BlockSpec((SSD_CONV, bcw), lambda b, c: (0, d_inner // bcw)),
            pl.BlockSpec((1, d_inner), const),
            pl.BlockSpec((1, bcw), lambda b, c: (0, d_inner // bcw)),
            pl.BlockSpec((1, halo, d_inner), lambda b, c: (b, 0, 0)),
            pl.BlockSpec((1, halo, bcw), lambda b, c: (b, 0, d_inner // bcw)),
            pl.BlockSpec((1, d_inner, N), lambda b, c: (b, 0, 0)),
        ],
        out_specs=[
            pl.BlockSpec((L, d_inner), lambda b, c: (row(b, c), 0)),
            pl.BlockSpec((1, d_inner, N), lambda b, c: (b, 0, 0)),
        ],
        out_shape=[jax.ShapeDtypeStruct((B * T, d_inner), BF16 if L % 16 == 0 else F32),
                   jax.ShapeDtypeStruct((B, d_inner, N), F32)],
        scratch_shapes=[pltpu.VMEM((halo + L, d_inner), F32), pltpu.VMEM((halo + L, bcw), F32),
                        pltpu.VMEM((halo, d_inner), F32), pltpu.VMEM((halo, bcw), F32)],
        compiler_params=_cparams(2),
        name="ssd",
    )(proj, proj, proj, dt, pad_heads(dt_bias), pad_heads(a_log), expand, dsk, g_norm.reshape(1, d_inner),
      conv_w, conv_w, conv_b2, conv_b2, conv0p, conv0p, s0.reshape(B, d_inner, N))
    return y, s.reshape(B, heads, P, N)


def _gate_weight(w):
    gates = 2 * MLSTM_HEADS
    return jnp.pad(w[:, -gates:], ((0, 0), (0, LANES - gates)))


def _dt_weight(w, d_inner):
    heads = d_inner // SSD_HEAD_DIM
    return jnp.pad(w[:, -heads:], ((0, 0), (0, LANES - heads)))


def _trunk(x3, st, W, E):
    B, T, D = x3.shape
    fresh = st is None

    def mm(name, j, n, xin, **kw):
        if fresh:
            return _matmul(xin, E[name, j], None, n, **kw)
        w, layer = W[name], j
        if w.shape[2] % LANES:
            w, layer = w[j, :, :n], None
        outs = _matmul(xin, w, layer, n, emit=True, **kw)
        E[name, j] = outs[-1]
        return outs[0] if len(outs) == 2 else outs[:-1]

    x = x3.reshape(B * T, D)
    depth = W["norm_mix"].shape[0]
    o_c, o_n, o_m, o_k, o_v, o_s, o_cv, o_f = [], [], [], [], [], [], [], []
    for i in range(depth):
        j = i // N_MIXERS
        kind = i % N_MIXERS
        if kind == 0:
            dv = D // MLSTM_HEADS
            dk = dv // 2
            init = None if fresh else (st["c"], st["n"][j], st["m"][j])
            proj, gates = mm("w_in_a", j, 2 * MLSTM_HEADS * (dk + dv), x, gain=W["norm_mix"][i],
                             small=_gate_weight(W["w_in_a"][j]))
            bias = jnp.pad(jnp.concatenate([W["b_ig_a"][j], W["b_fg_a"][j]]), (0, LANES - 2 * MLSTM_HEADS))
            hg, c, n, m = _mlstm_mix(proj, gates, bias.reshape(1, LANES), W["g_head_a"][j], init, j,
                                     B, T, dk, dv)
            x = mm("w_out_a", j, D, hg, res=x)
            o_c.append(c); o_n.append(n); o_m.append(m)
        elif kind == 1:
            hd = D // SWA_HEADS
            kvw = SWA_KV_HEADS * hd
            qw = SWA_HEADS * hd
            proj = mm("w_in_b", j, qw + 2 * kvw, x, gain=W["norm_mix"][i])
            p3 = proj.reshape(B, T, qw + 2 * kvw)
            k_new, v_new = p3[:, :, qw:qw + kvw], p3[:, :, qw + kvw:]
            kb, vb = qw // kvw, qw // kvw + 1
            if fresh:
                nb = T // WINDOW
                prev = lambda b, n: (b * nb + jnp.maximum(n - 1, 0), kb)
                cur = lambda b, n: (b * nb + n, kb)
                prev_v = lambda b, n: (b * nb + jnp.maximum(n - 1, 0), vb)
                cur_v = lambda b, n: (b * nb + n, vb)
                o = _swa_call(W["sinks_b"][j], proj, lambda b, n: (b * nb + n, 0), proj, prev, cur,
                              proj, prev_v, cur_v, (B, nb), WINDOW, B * T, hd, True)
                kc, vc = k_new[:, -WINDOW:], v_new[:, -WINDOW:]
            else:
                kbuf = st["k"][j].reshape(B, WINDOW, kvw)
                vbuf = st["v"][j].reshape(B, WINDOW, kvw)
                zpad = jnp.zeros((B, WINDOW - T, kvw), F32)
                kk = jnp.concatenate([kbuf, k_new, zpad], axis=1).reshape(B * 2 * WINDOW, kvw)
                vv = jnp.concatenate([vbuf, v_new, zpad], axis=1).reshape(B * 2 * WINDOW, kvw)
                o = _swa_call(W["sinks_b"][j], proj, lambda b, n: (b, 0), kk, lambda b, n: (2 * b, 0),
                              lambda b, n: (2 * b + 1, 0), vv, lambda b, n: (2 * b, 0),
                              lambda b, n: (2 * b + 1, 0), (B, 1), T, B * T, hd, False)
                kc = jnp.concatenate([kbuf[:, T:], k_new], axis=1)
                vc = jnp.concatenate([vbuf[:, T:], v_new], axis=1)
            x = mm("w_out_b", j, D, o, res=x)
            o_k.append(kc.reshape(B, WINDOW, SWA_KV_HEADS, hd))
            o_v.append(vc.reshape(B, WINDOW, SWA_KV_HEADS, hd))
        else:
            d_inner = W["w_out_c"].shape[1]
            heads = d_inner // SSD_HEAD_DIM
            conv_dim = d_inner + 2 * SSD_GROUPS * SSD_STATE
            if fresh:
                s0 = jnp.zeros((B, heads, SSD_HEAD_DIM, SSD_STATE), F32)
                cb0 = jnp.zeros((B, SSD_CONV - 1, conv_dim), F32)
            else:
                s0, cb0 = st["ssm"][j], st["conv"][j]
            proj, dt = mm("w_in_c", j, d_inner + conv_dim, x, gain=W["norm_mix"][i],
                          small=_dt_weight(W["w_in_c"][j], d_inner))
            y, s = _ssd_mix(proj, dt, d_inner, W["conv_w_c"][j], W["conv_b_c"][j], W["dt_bias_c"][j],
                            W["a_log_c"][j], W["d_skip_c"][j], W["g_norm_c"][j], s0, cb0, B, T)
            x = mm("w_out_c", j, D, y, res=x)
            xbc = proj.reshape(B, T, -1)[:, -(SSD_CONV - 1):, d_inner:d_inner + conv_dim]
            o_s.append(s); o_cv.append(xbc)

        final_gain = W["norm_final"] if i == depth - 1 else None
        d_ff = W["w_ffn_out"].shape[1]
        if fresh:
            fb0 = jnp.zeros((B, FFN_CONV - 1, d_ff), F32)
            x, fb = _ffn(x, W["norm_ffn"][i], E["w_ffn", i], W["ffn_conv_w"][i], W["ffn_conv_b"][i],
                         fb0, 1, T, final_gain)
        else:
            xt = x.reshape(B, T, D).transpose(1, 0, 2).reshape(T * B, D)
            fb0 = st["ffn"][i].transpose(1, 0, 2).reshape(1, (FFN_CONV - 1) * B, d_ff)
            xt, fb, E["w_ffn", i] = _ffn(xt, W["norm_ffn"][i], (W["w_ffn_in"], W["w_ffn_out"], i),
                                         W["ffn_conv_w"][i], W["ffn_conv_b"][i], fb0, B, T * B, final_gain)
            x = xt.reshape(T, B, D).transpose(1, 0, 2).reshape(B * T, D)
            fb = fb.reshape(FFN_CONV - 1, B, d_ff).transpose(1, 0, 2)
        o_f.append(fb)
    return (x.reshape(B, T, D), jnp.stack(o_c), jnp.stack(o_n), jnp.stack(o_m), jnp.stack(o_k),
            jnp.stack(o_v), jnp.stack(o_s), jnp.stack(o_cv), jnp.stack(o_f))


def kernel(x_prompt, x_sample, state_mlstm_c, state_mlstm_n, state_mlstm_m, cache_swa_k, cache_swa_v, state_ssm, state_ssm_conv, state_ffn_conv, norm_mix, norm_ffn, norm_final, w_in_a, b_ig_a, b_fg_a, g_head_a, w_out_a, w_in_b, sinks_b, w_out_b, w_in_c, conv_w_c, conv_b_c, dt_bias_c, a_log_c, d_skip_c, g_norm_c, w_out_c, w_ffn_in, ffn_conv_w, ffn_conv_b, w_ffn_out):
    W = dict(norm_mix=norm_mix, norm_ffn=norm_ffn, norm_final=norm_final,
             w_in_a=w_in_a, b_ig_a=b_ig_a, b_fg_a=b_fg_a, g_head_a=g_head_a, w_out_a=w_out_a,
             w_in_b=w_in_b, sinks_b=sinks_b, w_out_b=w_out_b,
             w_in_c=w_in_c, conv_w_c=conv_w_c, conv_b_c=conv_b_c, dt_bias_c=dt_bias_c, a_log_c=a_log_c,
             d_skip_c=d_skip_c, g_norm_c=g_norm_c, w_out_c=w_out_c,
             w_ffn_in=w_ffn_in, ffn_conv_w=ffn_conv_w, ffn_conv_b=ffn_conv_b, w_ffn_out=w_ffn_out)
    st = dict(c=state_mlstm_c, n=state_mlstm_n, m=state_mlstm_m, k=cache_swa_k, v=cache_swa_v,
              ssm=state_ssm, conv=state_ssm_conv, ffn=state_ffn_conv)
    E = {}
    ys, sc, sn, sm, sk, sv, ss, scv, sf = _trunk(x_sample, st, W, E)
    yp, pc, pn, pm, pk, pv, ps, pcv, pf = _trunk(x_prompt, None, W, E)
    return (yp, ys, pc, sc, pn, sn, pm, sm, pk, sk, pv, sv, ps, ss, pcv, scv, pf, sf)
```

```python
import functools
import math

import jax
import jax.numpy as jnp
from jax import lax
from jax.experimental import pallas as pl
from jax.experimental.pallas import tpu as pltpu

F32 = jnp.float32
BF16 = jnp.bfloat16

RMS_EPS = 1e-6
N_MIXERS = 3

MLSTM_HEADS = 8
SWA_HEADS = 32
SWA_KV_HEADS = 4
SWA_GROUP = SWA_HEADS // SWA_KV_HEADS
WINDOW = 128
SSD_HEAD_DIM = 64
SSD_GROUPS = 8
SSD_STATE = 128
SSD_CONV = 4
FFN_CONV = 3

LANES = 128
SUBLANES = 8
VMEM_LIMIT_BYTES = 56 << 20
VMEM_TILE_BUDGET = 44 << 20

FFN_ROW_TILE = 1024
FFN_COL_TILE = 512
FFN_SUB_TILE = 256
MLSTM_CHUNK = 128
SSD_CHUNK = 128


def _cparams(n_axes):
    return pltpu.CompilerParams(dimension_semantics=("arbitrary",) * n_axes,
                                vmem_limit_bytes=VMEM_LIMIT_BYTES)


def _round_up(a, b):
    return (a + b - 1) // b * b


def _largest_divisor(n, candidates):
    for c in candidates:
        if n % c == 0:
            return c
    raise ValueError(f"no tile in {candidates} divides {n}")


def _rmsnorm(x, g):
    return x * lax.rsqrt(jnp.mean(x * x, axis=-1, keepdims=True) + RMS_EPS) * g


def _silu(x):
    return x * jax.nn.sigmoid(x)


def _mm_body(*refs, norm, residual, small, emit):
    it = iter(refs)
    x_ref = next(it)
    g_ref = next(it) if norm else None
    w_ref = next(it)
    w2_ref = next(it) if small else None
    r_ref = next(it) if residual else None
    o_ref = next(it)
    o2_ref = next(it) if small else None
    wb_ref = next(it) if emit else None
    xn_ref = next(it) if norm else None
    first = pl.program_id(1) == 0
    if norm:
        @pl.when(first)
        def _():
            xn_ref[...] = _rmsnorm(x_ref[...], g_ref[...]).astype(BF16)
        lhs = xn_ref[...]
    else:
        lhs = x_ref[...].astype(BF16)
    if small:
        @pl.when(first)
        def _():
            o2_ref[...] = jnp.dot(lhs, w2_ref[...].astype(BF16), preferred_element_type=F32)
    w = w_ref[...]
    if emit:
        w = w.astype(BF16)
        wb_ref[...] = w
    acc = jnp.dot(lhs, w, preferred_element_type=F32)
    if residual:
        acc = acc + r_ref[...]
    o_ref[...] = acc.astype(o_ref.dtype)


def _mm_tiles(m, k, n, ns, x_bytes, w_bytes, norm, residual, emit):
    for tm in (1024, 512, 256, 128, 64, 32, 16, 8):
        if m % tm:
            continue
        for tn in (2048, 1280, 1024, 896, 512, 256, 128):
            if n % tn:
                continue
            need = 2 * tm * k * x_bytes + 2 * k * tn * w_bytes + 2 * tm * tn * 4
            need += tm * k * 2 if norm else 0
            need += 2 * tm * tn * 4 if residual else 0
            need += 3 * k * tn * 2 if emit else 0
            need += 2 * k * ns * 4 + 2 * tm * ns * 4
            if need <= VMEM_TILE_BUDGET:
                return tm, tn
    raise ValueError("no matmul tile fits VMEM")


def _w_spec(w, layer, block, index):
    if layer is None:
        return pl.BlockSpec(block, index)
    return pl.BlockSpec((None,) + block, lambda *g: (layer,) + index(*g))


def _matmul(x, w, layer, n, gain=None, res=None, small=None, emit=False):
    m, k = x.shape
    norm, residual, has_small = gain is not None, res is not None, small is not None
    ns = small.shape[1] if has_small else 0
    tm, tn = _mm_tiles(m, k, n, ns, x.dtype.itemsize, w.dtype.itemsize, norm, residual, emit)
    assert not emit or tm == m
    in_specs = [pl.BlockSpec((tm, k), lambda i, j: (i, 0))]
    args = [x]
    if norm:
        in_specs.append(pl.BlockSpec((1, k), lambda i, j: (0, 0)))
        args.append(gain.reshape(1, k))
    in_specs.append(_w_spec(w, layer, (k, tn), lambda i, j: (0, j)))
    args.append(w)
    if has_small:
        in_specs.append(pl.BlockSpec((k, ns), lambda i, j: (0, 0)))
        args.append(small)
    if residual:
        in_specs.append(pl.BlockSpec((tm, tn), lambda i, j: (i, j)))
        args.append(res)
    out_specs = [pl.BlockSpec((tm, tn), lambda i, j: (i, j))]
    out_shape = [jax.ShapeDtypeStruct((m, n), F32)]
    if has_small:
        out_specs.append(pl.BlockSpec((tm, ns), lambda i, j: (i, 0)))
        out_shape.append(jax.ShapeDtypeStruct((m, ns), F32))
    if emit:
        out_specs.append(pl.BlockSpec((k, tn), lambda i, j: (0, j)))
        out_shape.append(jax.ShapeDtypeStruct((k, n), BF16))
    outs = pl.pallas_call(
        functools.partial(_mm_body, norm=norm, residual=residual, small=has_small, emit=emit),
        grid=(m // tm, n // tn),
        in_specs=in_specs,
        out_specs=out_specs,
        out_shape=out_shape,
        scratch_shapes=[pltpu.VMEM((tm, k), BF16)] if norm else [],
        compiler_params=_cparams(2),
        name="matmul",
    )(*args)
    return outs if len(outs) > 1 else outs[0]


def _ffn_body(*refs, tm, shift, halo, tiles_per_seq, nj, final_norm, emit):
    x_ref, gn_ref, wu_ref, wg_ref, cw_ref, cb_ref, wo_ref, buf0_ref, gf_ref, o_ref, bufo_ref = refs[:11]
    wub_ref, wgb_ref, wob_ref = refs[11:14] if emit else (None, None, None)
    xn_ref, gs_ref, carry_ref = refs[-3:]
    i = pl.program_id(0)
    j = pl.program_id(1)

    @pl.when(j == 0)
    def _():
        x = x_ref[...]
        xn_ref[...] = _rmsnorm(x, gn_ref[...]).astype(BF16)
        o_ref[...] = x

    first = (i % tiles_per_seq) == 0
    gs_ref[0:halo, :] = jnp.where(first, buf0_ref[0], carry_ref[j])
    base = halo - (FFN_CONV - 1) * shift
    xn = xn_ref[...]
    tf = wu_ref.shape[1]
    sub = min(tf, FFN_SUB_TILE)
    h_parts = []
    for s0 in range(0, tf, sub):
        cols = slice(s0, s0 + sub)
        wu, wg = wu_ref[:, cols], wg_ref[:, cols]
        if emit:
            wu, wg = wu.astype(BF16), wg.astype(BF16)
            wub_ref[:, cols] = wu
            wgb_ref[:, cols] = wg
        u = jnp.dot(xn, wu, preferred_element_type=F32)
        gs_ref[halo:halo + tm, cols] = jnp.dot(xn, wg, preferred_element_type=F32)
        c = gs_ref[base:base + tm, cols] * cw_ref[0:1, cols]
        for t in range(1, FFN_CONV):
            c = c + gs_ref[base + t * shift:base + t * shift + tm, cols] * cw_ref[t:t + 1, cols]
        c = c + cb_ref[:, cols]
        h_parts.append((_silu(c) * u).astype(BF16))
    tail = gs_ref[tm:tm + halo, :]
    carry_ref[j] = tail
    bufo_ref[0] = tail
    h = h_parts[0] if len(h_parts) == 1 else jnp.concatenate(h_parts, axis=1)
    wo = wo_ref[...]
    if emit:
        wo = wo.astype(BF16)
        wob_ref[...] = wo
    o_ref[...] += jnp.dot(h, wo, preferred_element_type=F32)

    if final_norm:
        @pl.when(j == nj - 1)
        def _():
            o_ref[...] = _rmsnorm(o_ref[...], gf_ref[...])


def _ffn(x, gain, weights, conv_w, conv_b, buf0, shift, rows_per_seq, final_gain):
    m, d = x.shape
    emit = weights[0].ndim == 3
    d_ff = weights[1].shape[1] if emit else weights[2].shape[0]
    tm = _largest_divisor(rows_per_seq, (FFN_ROW_TILE, 512, 256, 128, 64, 32, 16, 8))
    tf = _largest_divisor(d_ff, (FFN_COL_TILE, 256, 128))
    nj = d_ff // tf
    if emit:
        w_in, w_out, layer = weights
        assert m == tm
        w_args = [w_in, w_in, w_out]
        w_specs = [pl.BlockSpec((None, d, tf), lambda i, j: (layer, 0, j)),
                   pl.BlockSpec((None, d, tf), lambda i, j: (layer, 0, nj + j)),
                   pl.BlockSpec((None, tf, d), lambda i, j: (layer, j, 0))]
    else:
        w_args = list(weights)
        w_specs = [pl.BlockSpec((d, tf), lambda i, j: (0, j)),
                   pl.BlockSpec((d, tf), lambda i, j: (0, j)),
                   pl.BlockSpec((tf, d), lambda i, j: (j, 0))]
    tiles_per_seq = rows_per_seq // tm
    pre = (FFN_CONV - 1) * shift
    halo = _round_up(pre, SUBLANES)
    assert tm >= halo and tm % SUBLANES == 0
    buf0p = jnp.pad(buf0, ((0, 0), (halo - pre, 0), (0, 0)))
    final_norm = final_gain is not None
    gf = (final_gain if final_norm else gain).reshape(1, d)
    body = functools.partial(_ffn_body, tm=tm, shift=shift, halo=halo, tiles_per_seq=tiles_per_seq,
                             nj=nj, final_norm=final_norm, emit=emit)
    out_specs = [pl.BlockSpec((tm, d), lambda i, j: (i, 0)),
                 pl.BlockSpec((1, halo, tf), lambda i, j: (i, 0, j))]
    out_shape = [jax.ShapeDtypeStruct((m, d), F32),
                 jax.ShapeDtypeStruct((m // tm, halo, d_ff), F32)]
    if emit:
        out_specs += [pl.BlockSpec((d, tf), lambda i, j: (0, j)),
                      pl.BlockSpec((d, tf), lambda i, j: (0, j)),
                      pl.BlockSpec((tf, d), lambda i, j: (j, 0))]
        out_shape += [jax.ShapeDtypeStruct((d, d_ff), BF16), jax.ShapeDtypeStruct((d, d_ff), BF16),
                      jax.ShapeDtypeStruct((d_ff, d), BF16)]
    outs = pl.pallas_call(
        body,
        grid=(m // tm, nj),
        in_specs=[
            pl.BlockSpec((tm, d), lambda i, j: (i, 0)),
            pl.BlockSpec((1, d), lambda i, j: (0, 0)),
            w_specs[0],
            w_specs[1],
            pl.BlockSpec((FFN_CONV, tf), lambda i, j: (0, j)),
            pl.BlockSpec((1, tf), lambda i, j: (0, j)),
            w_specs[2],
            pl.BlockSpec((1, halo, tf), lambda i, j: (i // tiles_per_seq, 0, j)),
            pl.BlockSpec((1, d), lambda i, j: (0, 0)),
        ],
        out_specs=out_specs,
        out_shape=out_shape,
        scratch_shapes=[pltpu.VMEM((tm, d), BF16),
                        pltpu.VMEM((halo + tm, tf), F32),
                        pltpu.VMEM((nj, halo, tf), F32)],
        compiler_params=_cparams(2),
        name="conv_ffn",
    )(x, gain.reshape(1, d), w_args[0], w_args[1], conv_w, conv_b.reshape(1, d_ff), w_args[2], buf0p, gf)
    result = (outs[0], outs[1][tiles_per_seq - 1::tiles_per_seq, halo - pre:, :])
    return result + ((tuple(outs[2:]),) if emit else ())


def _row_from_col(col, eye):
    return jnp.sum(jnp.where(eye, col, 0.0), axis=0, keepdims=True)


def _cumsum_rows(x, tril):
    return jnp.dot(tril, x, preferred_element_type=F32, precision=lax.Precision.HIGHEST)


def _mlstm_body(*refs, chunk, dk, dv, has_init):
    q_ref, k_ref, v_ref, o_ref, gate_ref, bias_ref, gh_ref = refs[:7]
    c0_ref, n0_ref, m0_ref = refs[7:10] if has_init else (None, None, None)
    hg_ref, c_ref, n_ref, m_ref = refs[-4:]
    L = chunk
    H = MLSTM_HEADS

    @pl.when(pl.program_id(1) == 0)
    def _():
        if has_init:
            c_ref[...] = c0_ref[...]
            n_ref[...] = n0_ref[...]
            m_ref[...] = m0_ref[...]
        else:
            c_ref[...] = jnp.zeros_like(c_ref)
            n_ref[...] = jnp.zeros_like(n_ref)
            m_ref[...] = jnp.zeros_like(m_ref)

    ri = lax.broadcasted_iota(jnp.int32, (L, L), 0)
    ci = lax.broadcasted_iota(jnp.int32, (L, L), 1)
    causal = ri >= ci
    eye = ri == ci
    tril = causal.astype(F32)

    gates = gate_ref[...] + bias_ref[...]
    x = gates
    log_sig = jnp.minimum(x, 0.0) - jnp.log1p(jnp.exp(-jnp.abs(x)))
    bcum = _cumsum_rows(log_sig, tril)

    n_all = n_ref[0]
    m_all = m_ref[0]
    if L % LANES == 0:
        gates_t, bcum_t = gates.T, bcum.T
        row_of = lambda arr_t, col, lane: arr_t[lane:lane + 1, :]
    else:
        gates_t = bcum_t = None
        row_of = lambda arr_t, col, lane: _row_from_col(col, eye)

    stack = lambda parts: jnp.concatenate(parts, axis=0)
    ig_cols = [gates[:, h:h + 1] for h in range(H)]
    b_cols = [bcum[:, H + h:H + h + 1] for h in range(H)]
    m_prevs = [m_all[h:h + 1, 0:1] for h in range(H)]
    ig_col, b_col = stack(ig_cols), stack(b_cols)
    m_prev = stack([jnp.broadcast_to(mp, (L, 1)) for mp in m_prevs])
    ig_row = stack([jnp.broadcast_to(row_of(gates_t, ig_cols[h], h), (L, L)) for h in range(H)])
    b_row = stack([jnp.broadcast_to(row_of(bcum_t, b_cols[h], H + h), (L, L)) for h in range(H)])
    causal_all = stack([causal] * H)

    log_d = jnp.where(causal_all, b_col - b_row + ig_row, -jnp.inf)
    log_inter = b_col + m_prev
    m_t = jnp.maximum(log_inter, jnp.max(log_d, axis=-1, keepdims=True))
    d = jnp.exp(log_d - m_t)
    inter = jnp.exp(log_inter - m_t)

    qk, qc, qn, kfs, vbs, c_prevs = [], [], [], [], [], []
    for h in range(H):
        qf = q_ref[:, h * dk:(h + 1) * dk] * (dk ** -0.5)
        kf = k_ref[:, h * dk:(h + 1) * dk]
        qb = qf.astype(BF16)
        c_prev = c_ref[0, h]
        qk.append(lax.dot_general(qb, kf.astype(BF16), (((1,), (1,)), ((), ())), preferred_element_type=F32))
        qc.append(jnp.dot(qb, c_prev.astype(BF16), preferred_element_type=F32))
        qn.append(jnp.sum(qf * n_all[h:h + 1, :], axis=-1, keepdims=True))
        kfs.append(kf)
        vbs.append(v_ref[:, h * dv:(h + 1) * dv].astype(BF16))
        c_prevs.append(c_prev)

    s = stack(qk) * d
    sb = s.astype(BF16)
    sv = stack([jnp.dot(sb[h * L:(h + 1) * L], vbs[h], preferred_element_type=F32) for h in range(H)])
    num = inter * stack(qc) + sv
    den = inter * stack(qn) + jnp.sum(s, axis=-1, keepdims=True)
    hh = num / jnp.maximum(jnp.abs(den), jnp.exp(-m_t))
    hn = hh * lax.rsqrt(jnp.mean(hh * hh, axis=-1, keepdims=True) + RMS_EPS)
    for h in range(H):
        lanes = slice(h * dv, (h + 1) * dv)
        out = jax.nn.sigmoid(o_ref[:, lanes]) * (hn[h * L:(h + 1) * L] * gh_ref[:, lanes])
        hg_ref[:, lanes] = out.astype(hg_ref.dtype)

    b_lasts = [bc[L - 1:L, :] for bc in b_cols]
    log_w = stack([jnp.broadcast_to(bl, (L, 1)) for bl in b_lasts]) - b_col + ig_col
    m_news = [jnp.maximum(b_lasts[h] + m_prevs[h], jnp.max(log_w[h * L:(h + 1) * L], axis=0, keepdims=True))
              for h in range(H)]
    w = jnp.exp(log_w - stack([jnp.broadcast_to(mn, (L, 1)) for mn in m_news]))

    head_row = lax.broadcasted_iota(jnp.int32, (H, LANES), 0)
    n_new, m_new_all = n_all, m_all
    for h in range(H):
        decay = jnp.exp(b_lasts[h] + m_prevs[h] - m_news[h])
        kw = kfs[h] * w[h * L:(h + 1) * L]
        upd = lax.dot_general(kw.astype(BF16), vbs[h], (((0,), (0,)), ((), ())), preferred_element_type=F32)
        c_ref[0, h] = decay * c_prevs[h] + upd
        n_row = decay * n_all[h:h + 1, :] + jnp.sum(kw, axis=0, keepdims=True)
        n_new = jnp.where(head_row == h, n_row, n_new)
        m_new_all = jnp.where(head_row == h, m_news[h], m_new_all)
    n_ref[0] = n_new
    m_ref[0] = m_new_all


def _mlstm_mix(proj, gates, bias, g_head, init, layer, B, T, dk, dv):
    H = MLSTM_HEADS
    L = math.gcd(T, MLSTM_CHUNK)
    nc = T // L
    qw, vw = H * dk, H * dv
    assert qw * 2 == vw and vw % LANES == 0
    hg_dtype = BF16 if L % 16 == 0 else F32
    row = lambda b, c: b * nc + c
    c_spec = pl.BlockSpec((None, 1, H, dk, dv), lambda b, c: (layer, b, 0, 0, 0))
    in_specs = [
        pl.BlockSpec((L, qw), lambda b, c: (row(b, c), 0)),
        pl.BlockSpec((L, qw), lambda b, c: (row(b, c), 1)),
        pl.BlockSpec((L, vw), lambda b, c: (row(b, c), 1)),
        pl.BlockSpec((L, vw), lambda b, c: (row(b, c), 2)),
        pl.BlockSpec((L, LANES), lambda b, c: (row(b, c), 0)),
        pl.BlockSpec((1, LANES), lambda b, c: (0, 0)),
        pl.BlockSpec((1, vw), lambda b, c: (0, 0)),
    ]
    args = [proj, proj, proj, proj, gates, bias, g_head.reshape(1, vw)]
    if init is not None:
        c_all, n0, m0 = init
        in_specs += [c_spec,
                     pl.BlockSpec((1, H, dk), lambda b, c: (b, 0, 0)),
                     pl.BlockSpec((1, H, LANES), lambda b, c: (b, 0, 0))]
        args += [c_all, n0, jnp.broadcast_to(m0[:, :, None], (B, H, LANES))]
    hg, c, n, m = pl.pallas_call(
        functools.partial(_mlstm_body, chunk=L, dk=dk, dv=dv, has_init=init is not None),
        grid=(B, nc),
        in_specs=in_specs,
        out_specs=[
            pl.BlockSpec((L, vw), lambda b, c: (row(b, c), 0)),
            pl.BlockSpec((1, H, dk, dv), lambda b, c: (b, 0, 0, 0)),
            pl.BlockSpec((1, H, dk), lambda b, c: (b, 0, 0)),
            pl.BlockSpec((1, H, LANES), lambda b, c: (b, 0, 0)),
        ],
        out_shape=[jax.ShapeDtypeStruct((B * T, vw), hg_dtype),
                   jax.ShapeDtypeStruct((B, H, dk, dv), F32),
                   jax.ShapeDtypeStruct((B, H, dk), F32),
                   jax.ShapeDtypeStruct((B, H, LANES), F32)],
        compiler_params=_cparams(2),
        name="mlstm",
    )(*args)
    return hg, c, n, m[:, :, 0]


def _swa_body(sink_ref, q_ref, kp_ref, kc_ref, vp_ref, vc_ref, o_ref, *, tq, hd, mask_first):
    W = WINDOW
    G = SWA_GROUP
    S = 2 * W
    kk = jnp.concatenate([kp_ref[...], kc_ref[...]], axis=0)
    vv = jnp.concatenate([vp_ref[...], vc_ref[...]], axis=0)
    qi = lax.broadcasted_iota(jnp.int32, (tq, S), 0)
    kj = lax.broadcasted_iota(jnp.int32, (tq, S), 1)
    dist = W + qi - kj
    valid = (dist >= 0) & (dist <= W)
    if mask_first:
        valid = valid & ((kj >= W) | (pl.program_id(1) > 0))
    distf = dist.astype(F32)
    lane_s = lax.broadcasted_iota(jnp.int32, (S, LANES), 1)
    lane_q = lax.broadcasted_iota(jnp.int32, (tq, LANES), 1)
    low_s = lane_s < hd
    low_q = lane_q < hd
    heads_per_slab = LANES // hd
    assert heads_per_slab == 2

    for kh in range(SWA_KV_HEADS):
        slab = (kh // heads_per_slab) * LANES
        keep_low = (kh % heads_per_slab) == 0
        k128 = kk[:, slab:slab + LANES]
        v128 = vv[:, slab:slab + LANES]
        k_rot = pltpu.roll(k128, hd, axis=1)
        v_rot = pltpu.roll(v128, hd, axis=1)
        own = low_s if keep_low else jnp.logical_not(low_s)
        k2 = jnp.where(own, k128, k_rot).astype(BF16)
        v2 = jnp.where(own, v128, v_rot).astype(BF16)

        q_parts = []
        for g in range(G):
            h = kh * G + g
            qs = q_ref[:, (h // 2) * LANES:(h // 2 + 1) * LANES]
            keep = low_q if h % 2 == 0 else jnp.logical_not(low_q)
            q_parts.append(jnp.where(keep, qs, 0.0))
        q_stack = jnp.concatenate(q_parts, axis=0).astype(BF16)
        s_all = lax.dot_general(q_stack, k2, (((1,), (1,)), ((), ())), preferred_element_type=F32)
        s_all = s_all * (hd ** -0.5)

        p_parts = []
        for g in range(G):
            h = kh * G + g
            slope = 2.0 ** (-8.0 * (h + 1) / SWA_HEADS)
            sink = sink_ref[h]
            s = s_all[g * tq:(g + 1) * tq, :] - slope * distf
            s = jnp.where(valid, s, -jnp.inf)
            mx = jnp.maximum(jnp.max(s, axis=-1, keepdims=True), sink)
            p = jnp.exp(s - mx)
            p = p / (jnp.sum(p, axis=-1, keepdims=True) + jnp.exp(sink - mx))
            p_parts.append(p)
        p_stack = jnp.concatenate(p_parts, axis=0).astype(BF16)
        o_all = jnp.dot(p_stack, v2, preferred_element_type=F32)

        for g2 in range(G // 2):
            h = kh * G + 2 * g2
            o_even = o_all[(2 * g2) * tq:(2 * g2 + 1) * tq, :]
            o_odd = o_all[(2 * g2 + 1) * tq:(2 * g2 + 2) * tq, :]
            o_ref[:, (h // 2) * LANES:(h // 2 + 1) * LANES] = jnp.where(low_q, o_even, o_odd).astype(o_ref.dtype)


def _swa_call(sinks, q_arr, q_blk, k_arr, kp_map, kc_map, v_arr, vp_map, vc_map, grid, tq, n_rows, hd,
              mask_first):
    kvw = SWA_KV_HEADS * hd
    qw = SWA_HEADS * hd
    out_dtype = BF16 if tq % 16 == 0 else F32
    return pl.pallas_call(
        functools.partial(_swa_body, tq=tq, hd=hd, mask_first=mask_first),
        grid=grid,
        in_specs=[
            pl.BlockSpec(memory_space=pltpu.SMEM),
            pl.BlockSpec((tq, qw), q_blk),
            pl.BlockSpec((WINDOW, kvw), kp_map),
            pl.BlockSpec((WINDOW, kvw), kc_map),
            pl.BlockSpec((WINDOW, kvw), vp_map),
            pl.BlockSpec((WINDOW, kvw), vc_map),
        ],
        out_specs=pl.BlockSpec((tq, qw), q_blk),
        out_shape=jax.ShapeDtypeStruct((n_rows, qw), out_dtype),
        compiler_params=_cparams(2),
        name="swa",
    )(sinks, q_arr, k_arr, k_arr, v_arr, v_arr)


def _conv_silu(u_ref, init_ref, carry_ref, gs_ref, cw_ref, cb_ref, first, L):
    halo = SUBLANES
    gs_ref[0:halo, :] = jnp.where(first, init_ref[0], carry_ref[...])
    gs_ref[halo:halo + L, :] = u_ref[...]
    carry_ref[...] = gs_ref[L:L + halo, :]
    base = halo - (SSD_CONV - 1)
    cw = cw_ref[...]
    c = gs_ref[base:base + L, :] * cw[0:1, :]
    for t in range(1, SSD_CONV):
        c = c + gs_ref[base + t:base + t + L, :] * cw[t:t + 1, :]
    c = c + cb_ref[...]
    return _silu(c)


def _split3_bf16(x):
    hi = x.astype(BF16)
    r = x - hi.astype(F32)
    mid = r.astype(BF16)
    lo = (r - mid.astype(F32)).astype(BF16)
    return hi, mid, lo


def _ssd_body(z_ref, xs_ref, bc_ref, dt_ref, dtb_ref, alog_ref, ex_ref, dsk_ref, gn_ref,
              cwx_ref, cwbc_ref, cbx_ref, cbbc_ref, ix_ref, ibc_ref, s0_ref,
              y_ref, s_ref,
              gx_ref, gbc_ref, kx_ref, kbc_ref, *, chunk, groups, hpg, p_dim):
    L = chunk
    N = SSD_STATE
    gw = hpg * p_dim
    first = pl.program_id(1) == 0

    @pl.when(first)
    def _():
        s_ref[...] = s0_ref[...]

    xs_all = _conv_silu(xs_ref, ix_ref, kx_ref, gx_ref, cwx_ref, cbx_ref, first, L)
    bc_all = _conv_silu(bc_ref, ibc_ref, kbc_ref, gbc_ref, cwbc_ref, cbbc_ref, first, L)

    ri = lax.broadcasted_iota(jnp.int32, (L, L), 0)
    ci = lax.broadcasted_iota(jnp.int32, (L, L), 1)
    causal = ri >= ci
    eye = ri == ci
    tril = causal.astype(F32)

    dt = jax.nn.softplus(dt_ref[...] + dtb_ref[...])
    acum = _cumsum_rows(dt * (-jnp.exp(alog_ref[...])), tril)
    a_last = acum[L - 1:L, :]
    eacum = jnp.exp(acum)
    dtw = dt * jnp.exp(a_last - acum)
    ea_last = jnp.exp(a_last)
    acum_t = acum.T if L % LANES == 0 else None

    per_head = jnp.concatenate([dt, eacum, dtw], axis=0)
    ex = jnp.dot(jnp.concatenate(_split3_bf16(per_head), axis=1), ex_ref[...],
                 preferred_element_type=F32)

    lane = lax.broadcasted_iota(jnp.int32, (L, LANES), 1)
    low = lane < p_dim
    srow = lax.broadcasted_iota(jnp.int32, (LANES, 1), 0)
    heads_per_slab = LANES // p_dim
    assert heads_per_slab == 2

    gated = []
    for g in range(groups):
        lanes_g = slice(g * gw, (g + 1) * gw)
        dt_e, ea_e, dtw_e = ex[0:L, lanes_g], ex[L:2 * L, lanes_g], ex[2 * L:3 * L, lanes_g]

        xs = xs_all[:, g * gw:(g + 1) * gw]
        bmb = bc_all[:, g * N:(g + 1) * N].astype(BF16)
        cmb = bc_all[:, (groups + g) * N:(groups + g + 1) * N].astype(BF16)
        cb = lax.dot_general(cmb, bmb, (((1,), (1,)), ((), ())), preferred_element_type=F32)
        xdt = (xs * dt_e).astype(BF16)
        xw = (xs * dtw_e).astype(BF16)
        s_prev = s_ref[0, g * gw:(g + 1) * gw, :]
        y_inter = lax.dot_general(cmb, s_prev.astype(BF16), (((1,), (1,)), ((), ())),
                                  preferred_element_type=F32)
        upd = lax.dot_general(xw, bmb, (((0,), (0,)), ((), ())), preferred_element_type=F32)

        y_parts = []
        for pr in range(hpg // heads_per_slab):
            h0 = g * hpg + 2 * pr
            ms = []
            for hh in (h0, h0 + 1):
                a_col = acum[:, hh:hh + 1]
                a_row = acum_t[hh:hh + 1, :] if acum_t is not None else _row_from_col(a_col, eye)
                dec = jnp.exp(jnp.where(causal, a_col - a_row, -jnp.inf))
                ms.append((cb * dec).astype(BF16))
            mcat = jnp.concatenate(ms, axis=1)
            xb = xdt[:, pr * LANES:(pr + 1) * LANES]
            zero = jnp.zeros_like(xb)
            bd = jnp.concatenate([jnp.where(low, xb, zero), jnp.where(low, zero, xb)], axis=0)
            y_parts.append(jnp.dot(mcat, bd, preferred_element_type=F32))

            scale = jnp.where(srow < p_dim, ea_last[:, h0:h0 + 1], ea_last[:, h0 + 1:h0 + 2])
            rows = slice(g * gw + pr * LANES, g * gw + (pr + 1) * LANES)
            s_ref[0, rows, :] = s_prev[pr * LANES:(pr + 1) * LANES] * scale + upd[pr * LANES:(pr + 1) * LANES]

        y = jnp.concatenate(y_parts, axis=1) + y_inter * ea_e
        y = y + dsk_ref[:, g * gw:(g + 1) * gw] * xs
        gated.append(y * _silu(z_ref[:, g * gw:(g + 1) * gw]))

    sq = gated[0] * gated[0]
    for y in gated[1:]:
        sq = sq + y * y
    inv = lax.rsqrt(jnp.sum(sq, axis=-1, keepdims=True) * (1.0 / (groups * gw)) + RMS_EPS)
    for g, y in enumerate(gated):
        y_ref[:, g * gw:(g + 1) * gw] = (y * inv * gn_ref[:, g * gw:(g + 1) * gw]).astype(y_ref.dtype)


def _ssd_mix(proj, dt, d_inner, conv_w, conv_b, dt_bias, a_log, d_skip, g_norm, s0, conv0, B, T):
    G, P, N = SSD_GROUPS, SSD_HEAD_DIM, SSD_STATE
    heads = d_inner // P
    hpg = heads // G
    L = math.gcd(T, SSD_CHUNK)
    nc = T // L
    bcw = 2 * G * N
    dtw = LANES
    conv_dim = d_inner + bcw
    assert d_inner % bcw == 0 and heads <= LANES
    bc_blk = (2 * d_inner) // bcw
    pad_heads = lambda v: jnp.pad(v, (0, LANES - heads)).reshape(1, LANES)
    expand = (jnp.arange(3 * LANES)[:, None] % LANES == jnp.arange(d_inner)[None, :] // P).astype(BF16)
    halo = SUBLANES
    conv0p = jnp.pad(conv0, ((0, 0), (halo - (SSD_CONV - 1), 0), (0, 0)))
    conv_b2 = conv_b.reshape(1, conv_dim)
    dsk = jnp.repeat(d_skip, P).reshape(1, d_inner)
    row = lambda b, c: b * nc + c
    const = lambda b, c: (0, 0)
    y, s = pl.pallas_call(
        functools.partial(_ssd_body, chunk=L, groups=G, hpg=hpg, p_dim=P),
        grid=(B, nc),
        in_specs=[
            pl.BlockSpec((L, d_inner), lambda b, c: (row(b, c), 0)),
            pl.BlockSpec((L, d_inner), lambda b, c: (row(b, c), 1)),
            pl.BlockSpec((L, bcw), lambda b, c: (row(b, c), bc_blk)),
            pl.BlockSpec((L, dtw), lambda b, c: (row(b, c), 0)),
            pl.BlockSpec((1, dtw), const),
            pl.BlockSpec((1, dtw), const),
            pl.BlockSpec((3 * LANES, d_inner), const),
            pl.BlockSpec((1, d_inner), const),
            pl.BlockSpec((1, d_inner), const),
            pl.BlockSpec((SSD_CONV, d_inner), const),
            pl.BlockSpec((SSD_CONV, bcw), lambda b, c: (0, d_inner // bcw)),
            pl.BlockSpec((1, d_inner), const),
            pl.BlockSpec((1, bcw), lambda b, c: (0, d_inner // bcw)),
            pl.BlockSpec((1, halo, d_inner), lambda b, c: (b, 0, 0)),
            pl.BlockSpec((1, halo, bcw), lambda b, c: (b, 0, d_inner // bcw)),
            pl.BlockSpec((1, d_inner, N), lambda b, c: (b, 0, 0)),
        ],
        out_specs=[
            pl.BlockSpec((L, d_inner), lambda b, c: (row(b, c), 0)),
            pl.BlockSpec((1, d_inner, N), lambda b, c: (b, 0, 0)),
        ],
        out_shape=[jax.ShapeDtypeStruct((B * T, d_inner), BF16 if L % 16 == 0 else F32),
                   jax.ShapeDtypeStruct((B, d_inner, N), F32)],
        scratch_shapes=[pltpu.VMEM((halo + L, d_inner), F32), pltpu.VMEM((halo + L, bcw), F32),
                        pltpu.VMEM((halo, d_inner), F32), pltpu.VMEM((halo, bcw), F32)],
        compiler_params=_cparams(2),
        name="ssd",
    )(proj, proj, proj, dt, pad_heads(dt_bias), pad_heads(a_log), expand, dsk, g_norm.reshape(1, d_inner),
      conv_w, conv_w, conv_b2, conv_b2, conv0p, conv0p, s0.reshape(B, d_inner, N))
    return y, s.reshape(B, heads, P, N)


def _gate_weight(w, j):
    gates = 2 * MLSTM_HEADS
    return jnp.pad(w[j, :, -gates:], ((0, 0), (0, LANES - gates)))


def _dt_weight(w, j, d_inner):
    heads = d_inner // SSD_HEAD_DIM
    return jnp.pad(w[j, :, -heads:], ((0, 0), (0, LANES - heads)))


def _main_columns_bf16(w, n):
    return [w[j, :, :n].astype(BF16) for j in range(w.shape[0])]


def _trunk(x3, st, W, E):
    B, T, D = x3.shape
    fresh = st is None

    def mm(name, j, n, xin, **kw):
        if (name, j) in E:
            return _matmul(xin, E[name, j], None, n, **kw)
        outs = _matmul(xin, W[name], j, n, emit=True, **kw)
        E[name, j] = outs[-1]
        return outs[0] if len(outs) == 2 else outs[:-1]

    x = x3.reshape(B * T, D)
    depth = W["norm_mix"].shape[0]
    o_c, o_n, o_m, o_k, o_v, o_s, o_cv, o_f = [], [], [], [], [], [], [], []
    for i in range(depth):
        j = i // N_MIXERS
        kind = i % N_MIXERS
        if kind == 0:
            dv = D // MLSTM_HEADS
            dk = dv // 2
            init = None if fresh else (st["c"], st["n"][j], st["m"][j])
            proj, gates = mm("w_in_a", j, 2 * MLSTM_HEADS * (dk + dv), x, gain=W["norm_mix"][i],
                             small=_gate_weight(W["w_in_a"], j))
            bias = jnp.pad(jnp.concatenate([W["b_ig_a"][j], W["b_fg_a"][j]]), (0, LANES - 2 * MLSTM_HEADS))
            hg, c, n, m = _mlstm_mix(proj, gates, bias.reshape(1, LANES), W["g_head_a"][j], init, j,
                                     B, T, dk, dv)
            x = mm("w_out_a", j, D, hg, res=x)
            o_c.append(c); o_n.append(n); o_m.append(m)
        elif kind == 1:
            hd = D // SWA_HEADS
            kvw = SWA_KV_HEADS * hd
            qw = SWA_HEADS * hd
            proj = mm("w_in_b", j, qw + 2 * kvw, x, gain=W["norm_mix"][i])
            p3 = proj.reshape(B, T, qw + 2 * kvw)
            k_new, v_new = p3[:, :, qw:qw + kvw], p3[:, :, qw + kvw:]
            kb, vb = qw // kvw, qw // kvw + 1
            if fresh:
                nb = T // WINDOW
                prev = lambda b, n: (b * nb + jnp.maximum(n - 1, 0), kb)
                cur = lambda b, n: (b * nb + n, kb)
                prev_v = lambda b, n: (b * nb + jnp.maximum(n - 1, 0), vb)
                cur_v = lambda b, n: (b * nb + n, vb)
                o = _swa_call(W["sinks_b"][j], proj, lambda b, n: (b * nb + n, 0), proj, prev, cur,
                              proj, prev_v, cur_v, (B, nb), WINDOW, B * T, hd, True)
                kc, vc = k_new[:, -WINDOW:], v_new[:, -WINDOW:]
            else:
                kbuf = st["k"][j].reshape(B, WINDOW, kvw)
                vbuf = st["v"][j].reshape(B, WINDOW, kvw)
                zpad = jnp.zeros((B, WINDOW - T, kvw), F32)
                kk = jnp.concatenate([kbuf, k_new, zpad], axis=1).reshape(B * 2 * WINDOW, kvw)
                vv = jnp.concatenate([vbuf, v_new, zpad], axis=1).reshape(B * 2 * WINDOW, kvw)
                o = _swa_call(W["sinks_b"][j], proj, lambda b, n: (b, 0), kk, lambda b, n: (2 * b, 0),
                              lambda b, n: (2 * b + 1, 0), vv, lambda b, n: (2 * b, 0),
                              lambda b, n: (2 * b + 1, 0), (B, 1), T, B * T, hd, False)
                kc = jnp.concatenate([kbuf[:, T:], k_new], axis=1)
                vc = jnp.concatenate([vbuf[:, T:], v_new], axis=1)
            x = mm("w_out_b", j, D, o, res=x)
            o_k.append(kc.reshape(B, WINDOW, SWA_KV_HEADS, hd))
            o_v.append(vc.reshape(B, WINDOW, SWA_KV_HEADS, hd))
        else:
            d_inner = W["w_out_c"].shape[1]
            heads = d_inner // SSD_HEAD_DIM
            conv_dim = d_inner + 2 * SSD_GROUPS * SSD_STATE
            if fresh:
                s0 = jnp.zeros((B, heads, SSD_HEAD_DIM, SSD_STATE), F32)
                cb0 = jnp.zeros((B, SSD_CONV - 1, conv_dim), F32)
            else:
                s0, cb0 = st["ssm"][j], st["conv"][j]
            proj, dt = mm("w_in_c", j, d_inner + conv_dim, x, gain=W["norm_mix"][i],
                          small=_dt_weight(W["w_in_c"], j, d_inner))
            y, s = _ssd_mix(proj, dt, d_inner, W["conv_w_c"][j], W["conv_b_c"][j], W["dt_bias_c"][j],
                            W["a_log_c"][j], W["d_skip_c"][j], W["g_norm_c"][j], s0, cb0, B, T)
            x = mm("w_out_c", j, D, y, res=x)
            xbc = proj.reshape(B, T, -1)[:, -(SSD_CONV - 1):, d_inner:d_inner + conv_dim]
            o_s.append(s); o_cv.append(xbc)

        final_gain = W["norm_final"] if i == depth - 1 else None
        d_ff = W["w_ffn_out"].shape[1]
        if fresh:
            fb0 = jnp.zeros((B, FFN_CONV - 1, d_ff), F32)
            x, fb = _ffn(x, W["norm_ffn"][i], E["w_ffn", i], W["ffn_conv_w"][i], W["ffn_conv_b"][i],
                         fb0, 1, T, final_gain)
        else:
            xt = x.reshape(B, T, D).transpose(1, 0, 2).reshape(T * B, D)
            fb0 = st["ffn"][i].transpose(1, 0, 2).reshape(1, (FFN_CONV - 1) * B, d_ff)
            xt, fb, E["w_ffn", i] = _ffn(xt, W["norm_ffn"][i], (W["w_ffn_in"], W["w_ffn_out"], i),
                                         W["ffn_conv_w"][i], W["ffn_conv_b"][i], fb0, B, T * B, final_gain)
            x = xt.reshape(T, B, D).transpose(1, 0, 2).reshape(B * T, D)
            fb = fb.reshape(FFN_CONV - 1, B, d_ff).transpose(1, 0, 2)
        o_f.append(fb)
    return (x.reshape(B, T, D), jnp.stack(o_c), jnp.stack(o_n), jnp.stack(o_m), jnp.stack(o_k),
            jnp.stack(o_v), jnp.stack(o_s), jnp.stack(o_cv), jnp.stack(o_f))


def kernel(x_prompt, x_sample, state_mlstm_c, state_mlstm_n, state_mlstm_m, cache_swa_k, cache_swa_v, state_ssm, state_ssm_conv, state_ffn_conv, norm_mix, norm_ffn, norm_final, w_in_a, b_ig_a, b_fg_a, g_head_a, w_out_a, w_in_b, sinks_b, w_out_b, w_in_c, conv_w_c, conv_b_c, dt_bias_c, a_log_c, d_skip_c, g_norm_c, w_out_c, w_ffn_in, ffn_conv_w, ffn_conv_b, w_ffn_out):
    W = dict(norm_mix=norm_mix, norm_ffn=norm_ffn, norm_final=norm_final,
             w_in_a=w_in_a, b_ig_a=b_ig_a, b_fg_a=b_fg_a, g_head_a=g_head_a, w_out_a=w_out_a,
             w_in_b=w_in_b, sinks_b=sinks_b, w_out_b=w_out_b,
             w_in_c=w_in_c, conv_w_c=conv_w_c, conv_b_c=conv_b_c, dt_bias_c=dt_bias_c, a_log_c=a_log_c,
             d_skip_c=d_skip_c, g_norm_c=g_norm_c, w_out_c=w_out_c,
             w_ffn_in=w_ffn_in, ffn_conv_w=ffn_conv_w, ffn_conv_b=ffn_conv_b, w_ffn_out=w_ffn_out)
    st = dict(c=state_mlstm_c, n=state_mlstm_n, m=state_mlstm_m, k=cache_swa_k, v=cache_swa_v,
              ssm=state_ssm, conv=state_ssm_conv, ffn=state_ffn_conv)
    E = {}
    for name, tail in (("w_in_a", 2 * MLSTM_HEADS), ("w_in_c", w_out_c.shape[1] // SSD_HEAD_DIM)):
        for j, wj in enumerate(_main_columns_bf16(W[name], W[name].shape[2] - tail)):
            E[name, j] = wj
    ys, sc, sn, sm, sk, sv, ss, scv, sf = _trunk(x_sample, st, W, E)
    yp, pc, pn, pm, pk, pv, ps, pcv, pf = _trunk(x_prompt, None, W, E)
    return (yp, ys, pc, sc, pn, sn, pm, sm, pk, sk, pv, sv, ps, ss, pcv, scv, pf, sf)
```

```python
import functools
import math

import jax
import jax.numpy as jnp
from jax import lax
from jax.experimental import pallas as pl
from jax.experimental.pallas import tpu as pltpu

F32 = jnp.float32
BF16 = jnp.bfloat16

RMS_EPS = 1e-6
N_MIXERS = 3

MLSTM_HEADS = 8
SWA_HEADS = 32
SWA_KV_HEADS = 4
SWA_GROUP = SWA_HEADS // SWA_KV_HEADS
WINDOW = 128
SSD_HEAD_DIM = 64
SSD_GROUPS = 8
SSD_STATE = 128
SSD_CONV = 4
FFN_CONV = 3

LANES = 128
SUBLANES = 8
VMEM_LIMIT_BYTES = 56 << 20
VMEM_TILE_BUDGET = 44 << 20

FFN_ROW_TILE = 1024
FFN_COL_TILE = 512
FFN_SUB_TILE = 256
MLSTM_CHUNK = 128
SSD_CHUNK = 128


def _cparams(n_axes):
    return pltpu.CompilerParams(dimension_semantics=("arbitrary",) * n_axes,
                                vmem_limit_bytes=VMEM_LIMIT_BYTES)


def _round_up(a, b):
    return (a + b - 1) // b * b


def _largest_divisor(n, candidates):
    for c in candidates:
        if n % c == 0:
            return c
    raise ValueError(f"no tile in {candidates} divides {n}")


def _rmsnorm(x, g):
    return x * lax.rsqrt(jnp.mean(x * x, axis=-1, keepdims=True) + RMS_EPS) * g


def _silu(x):
    return x * jax.nn.sigmoid(x)


def _mm_body(*refs, norm, residual, small, emit):
    it = iter(refs)
    x_ref = next(it)
    g_ref = next(it) if norm else None
    w_ref = next(it)
    w2_ref = next(it) if small else None
    r_ref = next(it) if residual else None
    o_ref = next(it)
    o2_ref = next(it) if small else None
    wb_ref = next(it) if emit else None
    xn_ref = next(it) if norm else None
    first = pl.program_id(1) == 0
    if norm:
        @pl.when(first)
        def _():
            xn_ref[...] = _rmsnorm(x_ref[...], g_ref[...]).astype(BF16)
        lhs = xn_ref[...]
    else:
        lhs = x_ref[...].astype(BF16)
    if small:
        @pl.when(first)
        def _():
            o2_ref[...] = jnp.dot(lhs, w2_ref[...].astype(BF16), preferred_element_type=F32)
    w = w_ref[...]
    if emit:
        w = w.astype(BF16)
        wb_ref[...] = w
    acc = jnp.dot(lhs, w, preferred_element_type=F32)
    if residual:
        acc = acc + r_ref[...]
    o_ref[...] = acc.astype(o_ref.dtype)


def _mm_tiles(m, k, n, ns, x_bytes, w_bytes, norm, residual, emit):
    def need(tm, tn, w_bufs):
        b = 2 * tm * k * x_bytes + w_bufs * k * tn * w_bytes + 2 * tm * tn * 4
        b += tm * k * 2 if norm else 0
        b += 2 * tm * tn * 4 if residual else 0
        b += 3 * k * tn * 2 if emit else 0
        return b + 2 * k * ns * 4 + 2 * tm * ns * 4

    row_tiles = [tm for tm in (1024, 512, 256, 128, 64, 32, 16, 8) if m % tm == 0]
    for tm in row_tiles[:2]:
        if need(tm, n, 1) <= VMEM_TILE_BUDGET:
            return tm, n, 1
    for tm in row_tiles:
        for tn in (2048, 1280, 1024, 896, 512, 256, 128):
            if n % tn == 0 and need(tm, tn, 2) <= VMEM_TILE_BUDGET:
                return tm, tn, 2
    raise ValueError("no matmul tile fits VMEM")


def _w_spec(w, layer, block, index, buffers):
    mode = pl.Buffered(buffers)
    if layer is None:
        return pl.BlockSpec(block, index, pipeline_mode=mode)
    return pl.BlockSpec((None,) + block, lambda *g: (layer,) + index(*g), pipeline_mode=mode)


def _matmul(x, w, layer, n, gain=None, res=None, small=None, emit=False):
    m, k = x.shape
    norm, residual, has_small = gain is not None, res is not None, small is not None
    ns = small.shape[1] if has_small else 0
    tm, tn, w_bufs = _mm_tiles(m, k, n, ns, x.dtype.itemsize, w.dtype.itemsize, norm, residual, emit)
    assert not emit or tm == m
    in_specs = [pl.BlockSpec((tm, k), lambda i, j: (i, 0))]
    args = [x]
    if norm:
        in_specs.append(pl.BlockSpec((1, k), lambda i, j: (0, 0)))
        args.append(gain.reshape(1, k))
    in_specs.append(_w_spec(w, layer, (k, tn), lambda i, j: (0, j), w_bufs))
    args.append(w)
    if has_small:
        in_specs.append(pl.BlockSpec((k, ns), lambda i, j: (0, 0)))
        args.append(small)
    if residual:
        in_specs.append(pl.BlockSpec((tm, tn), lambda i, j: (i, j)))
        args.append(res)
    out_specs = [pl.BlockSpec((tm, tn), lambda i, j: (i, j))]
    out_shape = [jax.ShapeDtypeStruct((m, n), F32)]
    if has_small:
        out_specs.append(pl.BlockSpec((tm, ns), lambda i, j: (i, 0)))
        out_shape.append(jax.ShapeDtypeStruct((m, ns), F32))
    if emit:
        out_specs.append(pl.BlockSpec((k, tn), lambda i, j: (0, j)))
        out_shape.append(jax.ShapeDtypeStruct((k, n), BF16))
    outs = pl.pallas_call(
        functools.partial(_mm_body, norm=norm, residual=residual, small=has_small, emit=emit),
        grid=(m // tm, n // tn),
        in_specs=in_specs,
        out_specs=out_specs,
        out_shape=out_shape,
        scratch_shapes=[pltpu.VMEM((tm, k), BF16)] if norm else [],
        compiler_params=_cparams(2),
        name="matmul",
    )(*args)
    return outs if len(outs) > 1 else outs[0]


def _ffn_body(*refs, tm, shift, halo, tiles_per_seq, nj, final_norm, emit):
    x_ref, gn_ref, wu_ref, wg_ref, cw_ref, cb_ref, wo_ref, buf0_ref, gf_ref, o_ref, bufo_ref = refs[:11]
    wub_ref, wgb_ref, wob_ref = refs[11:14] if emit else (None, None, None)
    xn_ref, gs_ref, carry_ref = refs[-3:]
    i = pl.program_id(0)
    j = pl.program_id(1)

    @pl.when(j == 0)
    def _():
        x = x_ref[...]
        xn_ref[...] = _rmsnorm(x, gn_ref[...]).astype(BF16)
        o_ref[...] = x

    first = (i % tiles_per_seq) == 0
    gs_ref[0:halo, :] = jnp.where(first, buf0_ref[0], carry_ref[j])
    base = halo - (FFN_CONV - 1) * shift
    xn = xn_ref[...]
    tf = wu_ref.shape[1]
    sub = min(tf, FFN_SUB_TILE)
    h_parts = []
    for s0 in range(0, tf, sub):
        cols = slice(s0, s0 + sub)
        wu, wg = wu_ref[:, cols], wg_ref[:, cols]
        if emit:
            wu, wg = wu.astype(BF16), wg.astype(BF16)
            wub_ref[:, cols] = wu
            wgb_ref[:, cols] = wg
        u = jnp.dot(xn, wu, preferred_element_type=F32)
        gs_ref[halo:halo + tm, cols] = jnp.dot(xn, wg, preferred_element_type=F32)
        c = gs_ref[base:base + tm, cols] * cw_ref[0:1, cols]
        for t in range(1, FFN_CONV):
            c = c + gs_ref[base + t * shift:base + t * shift + tm, cols] * cw_ref[t:t + 1, cols]
        c = c + cb_ref[:, cols]
        h_parts.append((_silu(c) * u).astype(BF16))
    tail = gs_ref[tm:tm + halo, :]
    carry_ref[j] = tail
    bufo_ref[0] = tail
    h = h_parts[0] if len(h_parts) == 1 else jnp.concatenate(h_parts, axis=1)
    wo = wo_ref[...]
    if emit:
        wo = wo.astype(BF16)
        wob_ref[...] = wo
    o_ref[...] += jnp.dot(h, wo, preferred_element_type=F32)

    if final_norm:
        @pl.when(j == nj - 1)
        def _():
            o_ref[...] = _rmsnorm(o_ref[...], gf_ref[...])


def _ffn(x, gain, weights, conv_w, conv_b, buf0, shift, rows_per_seq, final_gain):
    m, d = x.shape
    emit = weights[0].ndim == 3
    d_ff = weights[1].shape[1] if emit else weights[2].shape[0]
    tm = _largest_divisor(rows_per_seq, (FFN_ROW_TILE, 512, 256, 128, 64, 32, 16, 8))
    tf = _largest_divisor(d_ff, (FFN_COL_TILE, 256, 128))
    nj = d_ff // tf
    if emit:
        w_in, w_out, layer = weights
        assert m == tm
        w_args = [w_in, w_in, w_out]
        w_specs = [pl.BlockSpec((None, d, tf), lambda i, j: (layer, 0, j)),
                   pl.BlockSpec((None, d, tf), lambda i, j: (layer, 0, nj + j)),
                   pl.BlockSpec((None, tf, d), lambda i, j: (layer, j, 0))]
    else:
        w_args = list(weights)
        w_specs = [pl.BlockSpec((d, tf), lambda i, j: (0, j)),
                   pl.BlockSpec((d, tf), lambda i, j: (0, j)),
                   pl.BlockSpec((tf, d), lambda i, j: (j, 0))]
    tiles_per_seq = rows_per_seq // tm
    pre = (FFN_CONV - 1) * shift
    halo = _round_up(pre, SUBLANES)
    assert tm >= halo and tm % SUBLANES == 0
    buf0p = jnp.pad(buf0, ((0, 0), (halo - pre, 0), (0, 0)))
    final_norm = final_gain is not None
    gf = (final_gain if final_norm else gain).reshape(1, d)
    body = functools.partial(_ffn_body, tm=tm, shift=shift, halo=halo, tiles_per_seq=tiles_per_seq,
                             nj=nj, final_norm=final_norm, emit=emit)
    out_specs = [pl.BlockSpec((tm, d), lambda i, j: (i, 0)),
                 pl.BlockSpec((1, halo, tf), lambda i, j: (i, 0, j))]
    out_shape = [jax.ShapeDtypeStruct((m, d), F32),
                 jax.ShapeDtypeStruct((m // tm, halo, d_ff), F32)]
    if emit:
        out_specs += [pl.BlockSpec((d, tf), lambda i, j: (0, j)),
                      pl.BlockSpec((d, tf), lambda i, j: (0, j)),
                      pl.BlockSpec((tf, d), lambda i, j: (j, 0))]
        out_shape += [jax.ShapeDtypeStruct((d, d_ff), BF16), jax.ShapeDtypeStruct((d, d_ff), BF16),
                      jax.ShapeDtypeStruct((d_ff, d), BF16)]
    outs = pl.pallas_call(
        body,
        grid=(m // tm, nj),
        in_specs=[
            pl.BlockSpec((tm, d), lambda i, j: (i, 0)),
            pl.BlockSpec((1, d), lambda i, j: (0, 0)),
            w_specs[0],
            w_specs[1],
            pl.BlockSpec((FFN_CONV, tf), lambda i, j: (0, j)),
            pl.BlockSpec((1, tf), lambda i, j: (0, j)),
            w_specs[2],
            pl.BlockSpec((1, halo, tf), lambda i, j: (i // tiles_per_seq, 0, j)),
            pl.BlockSpec((1, d), lambda i, j: (0, 0)),
        ],
        out_specs=out_specs,
        out_shape=out_shape,
        scratch_shapes=[pltpu.VMEM((tm, d), BF16),
                        pltpu.VMEM((halo + tm, tf), F32),
                        pltpu.VMEM((nj, halo, tf), F32)],
        compiler_params=_cparams(2),
        name="conv_ffn",
    )(x, gain.reshape(1, d), w_args[0], w_args[1], conv_w, conv_b.reshape(1, d_ff), w_args[2], buf0p, gf)
    result = (outs[0], outs[1][tiles_per_seq - 1::tiles_per_seq, halo - pre:, :])
    return result + ((tuple(outs[2:]),) if emit else ())


def _row_from_col(col, eye):
    return jnp.sum(jnp.where(eye, col, 0.0), axis=0, keepdims=True)


def _rowsum_lanes(x):
    parts = jnp.concatenate(_split3_bf16(x), axis=1)
    return jnp.dot(parts, jnp.ones((parts.shape[1], LANES), BF16), preferred_element_type=F32)


def _cumsum_rows(x, tril):
    return jnp.dot(tril, x, preferred_element_type=F32, precision=lax.Precision.HIGHEST)


def _mlstm_body(*refs, chunk, dk, dv, has_init):
    q_ref, k_ref, v_ref, o_ref, gate_ref, bias_ref, gh_ref = refs[:7]
    c0_ref, n0_ref, m0_ref = refs[7:10] if has_init else (None, None, None)
    hg_ref, c_ref, n_ref, m_ref = refs[-4:]
    L = chunk
    H = MLSTM_HEADS

    @pl.when(pl.program_id(1) == 0)
    def _():
        if has_init:
            c_ref[...] = c0_ref[...]
            n_ref[...] = n0_ref[...]
            m_ref[...] = m0_ref[...]
        else:
            c_ref[...] = jnp.zeros_like(c_ref)
            n_ref[...] = jnp.zeros_like(n_ref)
            m_ref[...] = jnp.zeros_like(m_ref)

    ri = lax.broadcasted_iota(jnp.int32, (L, L), 0)
    ci = lax.broadcasted_iota(jnp.int32, (L, L), 1)
    causal = ri >= ci
    eye = ri == ci
    tril = causal.astype(F32)

    gates = gate_ref[...] + bias_ref[...]
    x = gates
    log_sig = jnp.minimum(x, 0.0) - jnp.log1p(jnp.exp(-jnp.abs(x)))
    bcum = _cumsum_rows(log_sig, tril)

    n_all = n_ref[0]
    m_all = m_ref[0]
    stack = lambda parts: jnp.concatenate(parts, axis=0)
    if L % LANES == 0:
        gates_t, bcum_t = gates.T, bcum.T
        row_of = lambda arr_t, col, r: arr_t[r:r + 1, :]
    else:
        gates_t = bcum_t = None
        row_of = lambda arr_t, col, r: _row_from_col(col, eye)
    ig_cols = [gates[:, h:h + 1] for h in range(H)]
    b_cols = [bcum[:, H + h:H + h + 1] for h in range(H)]
    m_prevs = [m_all[h:h + 1, 0:1] for h in range(H)]
    ig_col, b_col = stack(ig_cols), stack(b_cols)
    m_prev = stack([jnp.broadcast_to(mp, (L, 1)) for mp in m_prevs])
    ig_row = stack([jnp.broadcast_to(row_of(gates_t, ig_cols[h], h), (L, L)) for h in range(H)])
    b_row = stack([jnp.broadcast_to(row_of(bcum_t, b_cols[h], H + h), (L, L)) for h in range(H)])
    causal_all = stack([causal] * H)

    log_d = jnp.where(causal_all, b_col - b_row + ig_row, -jnp.inf)
    log_inter = b_col + m_prev
    m_t = jnp.maximum(log_inter, jnp.max(log_d, axis=-1, keepdims=True))
    d = jnp.exp(log_d - m_t)
    inter = jnp.exp(log_inter - m_t)

    qk, qc, qn, kfs, vbs, c_prevs = [], [], [], [], [], []
    for h in range(H):
        qf = q_ref[:, h * dk:(h + 1) * dk] * (dk ** -0.5)
        kf = k_ref[:, h * dk:(h + 1) * dk]
        qb = qf.astype(BF16)
        c_prev = c_ref[0, h]
        qk.append(lax.dot_general(qb, kf.astype(BF16), (((1,), (1,)), ((), ())), preferred_element_type=F32))
        qc.append(jnp.dot(qb, c_prev.astype(BF16), preferred_element_type=F32))
        qn.append(qf * n_all[h:h + 1, :])
        kfs.append(kf)
        vbs.append(v_ref[:, h * dv:(h + 1) * dv].astype(BF16))
        c_prevs.append(c_prev)

    s = stack(qk) * d
    sb = s.astype(BF16)
    sv = stack([jnp.dot(sb[h * L:(h + 1) * L], vbs[h], preferred_element_type=F32) for h in range(H)])
    num = inter * stack(qc) + sv
    if L % LANES == 0:
        rep = lambda col: jnp.broadcast_to(col, (H * L, LANES))
        wide = lambda a: jnp.concatenate([a] * (dv // LANES), axis=1)
        rowsum = _rowsum_lanes
    else:
        rep = wide = lambda a: a
        rowsum = lambda a: jnp.sum(a, axis=-1, keepdims=True)
    den = rep(inter) * rowsum(stack(qn)) + rowsum(s)
    hh = num / wide(jnp.maximum(jnp.abs(den), rep(jnp.exp(-m_t))))
    hn = hh * wide(lax.rsqrt(rowsum(hh * hh) * (1.0 / dv) + RMS_EPS))
    for h in range(H):
        lanes = slice(h * dv, (h + 1) * dv)
        out = jax.nn.sigmoid(o_ref[:, lanes]) * (hn[h * L:(h + 1) * L] * gh_ref[:, lanes])
        hg_ref[:, lanes] = out.astype(hg_ref.dtype)

    b_lasts = [bc[L - 1:L, :] for bc in b_cols]
    log_w = stack([jnp.broadcast_to(bl, (L, 1)) for bl in b_lasts]) - b_col + ig_col
    m_news = [jnp.maximum(b_lasts[h] + m_prevs[h], jnp.max(log_w[h * L:(h + 1) * L], axis=0, keepdims=True))
              for h in range(H)]
    w = jnp.exp(log_w - stack([jnp.broadcast_to(mn, (L, 1)) for mn in m_news]))

    head_row = lax.broadcasted_iota(jnp.int32, (H, LANES), 0)
    n_new, m_new_all = n_all, m_all
    for h in range(H):
        decay = jnp.exp(b_lasts[h] + m_prevs[h] - m_news[h])
        kw = kfs[h] * w[h * L:(h + 1) * L]
        upd = lax.dot_general(kw.astype(BF16), vbs[h], (((0,), (0,)), ((), ())), preferred_element_type=F32)
        c_ref[0, h] = decay * c_prevs[h] + upd
        n_row = decay * n_all[h:h + 1, :] + jnp.sum(kw, axis=0, keepdims=True)
        n_new = jnp.where(head_row == h, n_row, n_new)
        m_new_all = jnp.where(head_row == h, m_news[h], m_new_all)
    n_ref[0] = n_new
    m_ref[0] = m_new_all


def _mlstm_mix(proj, gates, bias, g_head, init, layer, B, T, dk, dv):
    H = MLSTM_HEADS
    L = math.gcd(T, MLSTM_CHUNK)
    nc = T // L
    qw, vw = H * dk, H * dv
    assert qw * 2 == vw and vw % LANES == 0
    hg_dtype = BF16 if L % 16 == 0 else F32
    row = lambda b, c: b * nc + c
    c_spec = pl.BlockSpec((None, 1, H, dk, dv), lambda b, c: (layer, b, 0, 0, 0))
    in_specs = [
        pl.BlockSpec((L, qw), lambda b, c: (row(b, c), 0)),
        pl.BlockSpec((L, qw), lambda b, c: (row(b, c), 1)),
        pl.BlockSpec((L, vw), lambda b, c: (row(b, c), 1)),
        pl.BlockSpec((L, vw), lambda b, c: (row(b, c), 2)),
        pl.BlockSpec((L, LANES), lambda b, c: (row(b, c), 0)),
        pl.BlockSpec((1, LANES), lambda b, c: (0, 0)),
        pl.BlockSpec((1, vw), lambda b, c: (0, 0)),
    ]
    args = [proj, proj, proj, proj, gates, bias, g_head.reshape(1, vw)]
    if init is not None:
        c_all, n0, m0 = init
        in_specs += [c_spec,
                     pl.BlockSpec((1, H, dk), lambda b, c: (b, 0, 0)),
                     pl.BlockSpec((1, H, LANES), lambda b, c: (b, 0, 0))]
        args += [c_all, n0, jnp.broadcast_to(m0[:, :, None], (B, H, LANES))]
    hg, c, n, m = pl.pallas_call(
        functools.partial(_mlstm_body, chunk=L, dk=dk, dv=dv, has_init=init is not None),
        grid=(B, nc),
        in_specs=in_specs,
        out_specs=[
            pl.BlockSpec((L, vw), lambda b, c: (row(b, c), 0)),
            pl.BlockSpec((1, H, dk, dv), lambda b, c: (b, 0, 0, 0)),
            pl.BlockSpec((1, H, dk), lambda b, c: (b, 0, 0)),
            pl.BlockSpec((1, H, LANES), lambda b, c: (b, 0, 0)),
        ],
        out_shape=[jax.ShapeDtypeStruct((B * T, vw), hg_dtype),
                   jax.ShapeDtypeStruct((B, H, dk, dv), F32),
                   jax.ShapeDtypeStruct((B, H, dk), F32),
                   jax.ShapeDtypeStruct((B, H, LANES), F32)],
        compiler_params=_cparams(2),
        name="mlstm",
    )(*args)
    return hg, c, n, m[:, :, 0]


def _swa_body(sink_ref, q_ref, kp_ref, kc_ref, vp_ref, vc_ref, o_ref, *, tq, hd, mask_first):
    W = WINDOW
    G = SWA_GROUP
    S = 2 * W
    kk = jnp.concatenate([kp_ref[...], kc_ref[...]], axis=0)
    vv = jnp.concatenate([vp_ref[...], vc_ref[...]], axis=0)
    qi = lax.broadcasted_iota(jnp.int32, (tq, S), 0)
    kj = lax.broadcasted_iota(jnp.int32, (tq, S), 1)
    dist = W + qi - kj
    valid = (dist >= 0) & (dist <= W)
    if mask_first:
        valid = valid & ((kj >= W) | (pl.program_id(1) > 0))
    distf = dist.astype(F32)
    lane_s = lax.broadcasted_iota(jnp.int32, (S, LANES), 1)
    lane_q = lax.broadcasted_iota(jnp.int32, (tq, LANES), 1)
    low_s = lane_s < hd
    low_q = lane_q < hd
    heads_per_slab = LANES // hd
    assert heads_per_slab == 2

    for kh in range(SWA_KV_HEADS):
        slab = (kh // heads_per_slab) * LANES
        keep_low = (kh % heads_per_slab) == 0
        k128 = kk[:, slab:slab + LANES]
        v128 = vv[:, slab:slab + LANES]
        k_rot = pltpu.roll(k128, hd, axis=1)
        v_rot = pltpu.roll(v128, hd, axis=1)
        own = low_s if keep_low else jnp.logical_not(low_s)
        k2 = jnp.where(own, k128, k_rot).astype(BF16)
        v2 = jnp.where(own, v128, v_rot).astype(BF16)

        q_parts = []
        for g in range(G):
            h = kh * G + g
            qs = q_ref[:, (h // 2) * LANES:(h // 2 + 1) * LANES]
            keep = low_q if h % 2 == 0 else jnp.logical_not(low_q)
            q_parts.append(jnp.where(keep, qs, 0.0))
        q_stack = jnp.concatenate(q_parts, axis=0).astype(BF16)
        s_all = lax.dot_general(q_stack, k2, (((1,), (1,)), ((), ())), preferred_element_type=F32)
        s_all = s_all * (hd ** -0.5)

        p_parts = []
        for g in range(G):
            h = kh * G + g
            slope = 2.0 ** (-8.0 * (h + 1) / SWA_HEADS)
            sink = sink_ref[h]
            s = s_all[g * tq:(g + 1) * tq, :] - slope * distf
            s = jnp.where(valid, s, -jnp.inf)
            mx = jnp.maximum(jnp.max(s, axis=-1, keepdims=True), sink)
            p = jnp.exp(s - mx)
            p = p / (jnp.sum(p, axis=-1, keepdims=True) + jnp.exp(sink - mx))
            p_parts.append(p)
        p_stack = jnp.concatenate(p_parts, axis=0).astype(BF16)
        o_all = jnp.dot(p_stack, v2, preferred_element_type=F32)

        for g2 in range(G // 2):
            h = kh * G + 2 * g2
            o_even = o_all[(2 * g2) * tq:(2 * g2 + 1) * tq, :]
            o_odd = o_all[(2 * g2 + 1) * tq:(2 * g2 + 2) * tq, :]
            o_ref[:, (h // 2) * LANES:(h // 2 + 1) * LANES] = jnp.where(low_q, o_even, o_odd).astype(o_ref.dtype)


def _swa_call(sinks, q_arr, q_blk, k_arr, kp_map, kc_map, v_arr, vp_map, vc_map, grid, tq, n_rows, hd,
              mask_first):
    kvw = SWA_KV_HEADS * hd
    qw = SWA_HEADS * hd
    out_dtype = BF16 if tq % 16 == 0 else F32
    return pl.pallas_call(
        functools.partial(_swa_body, tq=tq, hd=hd, mask_first=mask_first),
        grid=grid,
        in_specs=[
            pl.BlockSpec(memory_space=pltpu.SMEM),
            pl.BlockSpec((tq, qw), q_blk),
            pl.BlockSpec((WINDOW, kvw), kp_map),
            pl.BlockSpec((WINDOW, kvw), kc_map),
            pl.BlockSpec((WINDOW, kvw), vp_map),
            pl.BlockSpec((WINDOW, kvw), vc_map),
        ],
        out_specs=pl.BlockSpec((tq, qw), q_blk),
        out_shape=jax.ShapeDtypeStruct((n_rows, qw), out_dtype),
        compiler_params=_cparams(2),
        name="swa",
    )(sinks, q_arr, k_arr, k_arr, v_arr, v_arr)


def _conv_silu(u_ref, init_ref, carry_ref, gs_ref, cw_ref, cb_ref, first, L):
    halo = SUBLANES
    gs_ref[0:halo, :] = jnp.where(first, init_ref[0], carry_ref[...])
    gs_ref[halo:halo + L, :] = u_ref[...]
    carry_ref[...] = gs_ref[L:L + halo, :]
    base = halo - (SSD_CONV - 1)
    cw = cw_ref[...]
    c = gs_ref[base:base + L, :] * cw[0:1, :]
    for t in range(1, SSD_CONV):
        c = c + gs_ref[base + t:base + t + L, :] * cw[t:t + 1, :]
    c = c + cb_ref[...]
    return _silu(c)


def _split3_bf16(x):
    hi = x.astype(BF16)
    r = x - hi.astype(F32)
    mid = r.astype(BF16)
    lo = (r - mid.astype(F32)).astype(BF16)
    return hi, mid, lo


def _ssd_body(z_ref, xs_ref, bc_ref, dt_ref, dtb_ref, alog_ref, ex_ref, dsk_ref, gn_ref,
              cwx_ref, cwbc_ref, cbx_ref, cbbc_ref, ix_ref, ibc_ref, s0_ref,
              y_ref, s_ref,
              gx_ref, gbc_ref, kx_ref, kbc_ref, *, chunk, groups, hpg, p_dim):
    L = chunk
    N = SSD_STATE
    gw = hpg * p_dim
    first = pl.program_id(1) == 0

    @pl.when(first)
    def _():
        s_ref[...] = s0_ref[...]

    xs_all = _conv_silu(xs_ref, ix_ref, kx_ref, gx_ref, cwx_ref, cbx_ref, first, L)
    bc_all = _conv_silu(bc_ref, ibc_ref, kbc_ref, gbc_ref, cwbc_ref, cbbc_ref, first, L)

    ri = lax.broadcasted_iota(jnp.int32, (L, L), 0)
    ci = lax.broadcasted_iota(jnp.int32, (L, L), 1)
    causal = ri >= ci
    eye = ri == ci
    tril = causal.astype(F32)

    dt = jax.nn.softplus(dt_ref[...] + dtb_ref[...])
    acum = _cumsum_rows(dt * (-jnp.exp(alog_ref[...])), tril)
    a_last = acum[L - 1:L, :]
    eacum = jnp.exp(acum)
    dtw = dt * jnp.exp(a_last - acum)
    ea_last = jnp.exp(a_last)
    acum_t = acum.T if L % LANES == 0 else None

    per_head = jnp.concatenate([dt, eacum, dtw], axis=0)
    ex = jnp.dot(jnp.concatenate(_split3_bf16(per_head), axis=1), ex_ref[...],
                 preferred_element_type=F32)

    lane = lax.broadcasted_iota(jnp.int32, (L, LANES), 1)
    low = lane < p_dim
    srow = lax.broadcasted_iota(jnp.int32, (LANES, 1), 0)
    heads_per_slab = LANES // p_dim
    assert heads_per_slab == 2

    gated = []
    for g in range(groups):
        lanes_g = slice(g * gw, (g + 1) * gw)
        dt_e, ea_e, dtw_e = ex[0:L, lanes_g], ex[L:2 * L, lanes_g], ex[2 * L:3 * L, lanes_g]

        xs = xs_all[:, g * gw:(g + 1) * gw]
        bmb = bc_all[:, g * N:(g + 1) * N].astype(BF16)
        cmb = bc_all[:, (groups + g) * N:(groups + g + 1) * N].astype(BF16)
        cb = lax.dot_general(cmb, bmb, (((1,), (1,)), ((), ())), preferred_element_type=F32)
        xdt = (xs * dt_e).astype(BF16)
        xw = (xs * dtw_e).astype(BF16)
        s_prev = s_ref[0, g * gw:(g + 1) * gw, :]
        y_inter = lax.dot_general(cmb, s_prev.astype(BF16), (((1,), (1,)), ((), ())),
                                  preferred_element_type=F32)
        upd = lax.dot_general(xw, bmb, (((0,), (0,)), ((), ())), preferred_element_type=F32)

        y_parts = []
        for pr in range(hpg // heads_per_slab):
            h0 = g * hpg + 2 * pr
            ms = []
            for hh in (h0, h0 + 1):
                a_col = acum[:, hh:hh + 1]
                a_row = acum_t[hh:hh + 1, :] if acum_t is not None else _row_from_col(a_col, eye)
                dec = jnp.exp(jnp.where(causal, a_col - a_row, -jnp.inf))
                ms.append((cb * dec).astype(BF16))
            mcat = jnp.concatenate(ms, axis=1)
            xb = xdt[:, pr * LANES:(pr + 1) * LANES]
            zero = jnp.zeros_like(xb)
            bd = jnp.concatenate([jnp.where(low, xb, zero), jnp.where(low, zero, xb)], axis=0)
            y_parts.append(jnp.dot(mcat, bd, preferred_element_type=F32))

            scale = jnp.where(srow < p_dim, ea_last[:, h0:h0 + 1], ea_last[:, h0 + 1:h0 + 2])
            rows = slice(g * gw + pr * LANES, g * gw + (pr + 1) * LANES)
            s_ref[0, rows, :] = s_prev[pr * LANES:(pr + 1) * LANES] * scale + upd[pr * LANES:(pr + 1) * LANES]

        y = jnp.concatenate(y_parts, axis=1) + y_inter * ea_e
        y = y + dsk_ref[:, g * gw:(g + 1) * gw] * xs
        gated.append(y * _silu(z_ref[:, g * gw:(g + 1) * gw]))

    sq = gated[0] * gated[0]
    for y in gated[1:]:
        sq = sq + y * y
    inv = lax.rsqrt(jnp.sum(sq, axis=-1, keepdims=True) * (1.0 / (groups * gw)) + RMS_EPS)
    for g, y in enumerate(gated):
        y_ref[:, g * gw:(g + 1) * gw] = (y * inv * gn_ref[:, g * gw:(g + 1) * gw]).astype(y_ref.dtype)


def _ssd_mix(proj, dt, d_inner, conv_w, conv_b, dt_bias, a_log, d_skip, g_norm, s0, conv0, B, T):
    G, P, N = SSD_GROUPS, SSD_HEAD_DIM, SSD_STATE
    heads = d_inner // P
    hpg = heads // G
    L = math.gcd(T, SSD_CHUNK)
    nc = T // L
    bcw = 2 * G * N
    dtw = LANES
    conv_dim = d_inner + bcw
    assert d_inner % bcw == 0 and heads <= LANES
    bc_blk = (2 * d_inner) // bcw
    pad_heads = lambda v: jnp.pad(v, (0, LANES - heads)).reshape(1, LANES)
    expand = (jnp.arange(3 * LANES)[:, None] % LANES == jnp.arange(d_inner)[None, :] // P).astype(BF16)
    halo = SUBLANES
    conv0p = jnp.pad(conv0, ((0, 0), (halo - (SSD_CONV - 1), 0), (0, 0)))
    conv_b2 = conv_b.reshape(1, conv_dim)
    dsk = jnp.repeat(d_skip, P).reshape(1, d_inner)
    row = lambda b, c: b * nc + c
    const = lambda b, c: (0, 0)
    y, s = pl.pallas_call(
        functools.partial(_ssd_body, chunk=L, groups=G, hpg=hpg, p_dim=P),
        grid=(B, nc),
        in_specs=[
            pl.BlockSpec((L, d_inner), lambda b, c: (row(b, c), 0)),
            pl.BlockSpec((L, d_inner), lambda b, c: (row(b, c), 1)),
            pl.BlockSpec((L, bcw), lambda b, c: (row(b, c), bc_blk)),
            pl.BlockSpec((L, dtw), lambda b, c: (row(b, c), 0)),
            pl.BlockSpec((1, dtw), const),
            pl.BlockSpec((1, dtw), const),
            pl.BlockSpec((3 * LANES, d_inner), const),
            pl.BlockSpec((1, d_inner), const),
            pl.BlockSpec((1, d_inner), const),
            pl.BlockSpec((SSD_CONV, d_inner), const),
            pl.BlockSpec((SSD_CONV, bcw), lambda b, c: (0, d_inner // bcw)),
            pl.BlockSpec((1, d_inner), const),
            pl.BlockSpec((1, bcw), lambda b, c: (0, d_inner // bcw)),
            pl.BlockSpec((1, halo, d_inner), lambda b, c: (b, 0, 0)),
            pl.BlockSpec((1, halo, bcw), lambda b, c: (b, 0, d_inner // bcw)),
            pl.BlockSpec((1, d_inner, N), lambda b, c: (b, 0, 0)),
        ],
        out_specs=[
            pl.BlockSpec((L, d_inner), lambda b, c: (row(b, c), 0)),
            pl.BlockSpec((1, d_inner, N), lambda b, c: (b, 0, 0)),
        ],
        out_shape=[jax.ShapeDtypeStruct((B * T, d_inner), BF16 if L % 16 == 0 else F32),
                   jax.ShapeDtypeStruct((B, d_inner, N), F32)],
        scratch_shapes=[pltpu.VMEM((halo + L, d_inner), F32), pltpu.VMEM((halo + L, bcw), F32),
                        pltpu.VMEM((halo, d_inner), F32), pltpu.VMEM((halo, bcw), F32)],
        compiler_params=_cparams(2),
        name="ssd",
    )(proj, proj, proj, dt, pad_heads(dt_bias), pad_heads(a_log), expand, dsk, g_norm.reshape(1, d_inner),
      conv_w, conv_w, conv_b2, conv_b2, conv0p, conv0p, s0.reshape(B, d_inner, N))
    return y, s.reshape(B, heads, P, N)


def _gate_weight(w, j):
    gates = 2 * MLSTM_HEADS
    return jnp.pad(w[j, :, -gates:], ((0, 0), (0, LANES - gates)))


def _dt_weight(w, j, d_inner):
    heads = d_inner // SSD_HEAD_DIM
    return jnp.pad(w[j, :, -heads:], ((0, 0), (0, LANES - heads)))


def _main_columns_bf16(w, n):
    return [w[j, :, :n].astype(BF16) for j in range(w.shape[0])]


def _trunk(x3, st, W, E):
    B, T, D = x3.shape
    fresh = st is None

    def mm(name, j, n, xin, **kw):
        if (name, j) in E:
            return _matmul(xin, E[name, j], None, n, **kw)
        outs = _matmul(xin, W[name], j, n, emit=True, **kw)
        E[name, j] = outs[-1]
        return outs[0] if len(outs) == 2 else outs[:-1]

    x = x3.reshape(B * T, D)
    depth = W["norm_mix"].shape[0]
    o_c, o_n, o_m, o_k, o_v, o_s, o_cv, o_f = [], [], [], [], [], [], [], []
    for i in range(depth):
        j = i // N_MIXERS
        kind = i % N_MIXERS
        if kind == 0:
            dv = D // MLSTM_HEADS
            dk = dv // 2
            init = None if fresh else (st["c"], st["n"][j], st["m"][j])
            proj, gates = mm("w_in_a", j, 2 * MLSTM_HEADS * (dk + dv), x, gain=W["norm_mix"][i],
                             small=_gate_weight(W["w_in_a"], j))
            bias = jnp.pad(jnp.concatenate([W["b_ig_a"][j], W["b_fg_a"][j]]), (0, LANES - 2 * MLSTM_HEADS))
            hg, c, n, m = _mlstm_mix(proj, gates, bias.reshape(1, LANES), W["g_head_a"][j], init, j,
                                     B, T, dk, dv)
            x = mm("w_out_a", j, D, hg, res=x)
            o_c.append(c); o_n.append(n); o_m.append(m)
        elif kind == 1:
            hd = D // SWA_HEADS
            kvw = SWA_KV_HEADS * hd
            qw = SWA_HEADS * hd
            proj = mm("w_in_b", j, qw + 2 * kvw, x, gain=W["norm_mix"][i])
            p3 = proj.reshape(B, T, qw + 2 * kvw)
            k_new, v_new = p3[:, :, qw:qw + kvw], p3[:, :, qw + kvw:]
            kb, vb = qw // kvw, qw // kvw + 1
            if fresh:
                nb = T // WINDOW
                prev = lambda b, n: (b * nb + jnp.maximum(n - 1, 0), kb)
                cur = lambda b, n: (b * nb + n, kb)
                prev_v = lambda b, n: (b * nb + jnp.maximum(n - 1, 0), vb)
                cur_v = lambda b, n: (b * nb + n, vb)
                o = _swa_call(W["sinks_b"][j], proj, lambda b, n: (b * nb + n, 0), proj, prev, cur,
                              proj, prev_v, cur_v, (B, nb), WINDOW, B * T, hd, True)
                kc, vc = k_new[:, -WINDOW:], v_new[:, -WINDOW:]
            else:
                kbuf = st["k"][j].reshape(B, WINDOW, kvw)
                vbuf = st["v"][j].reshape(B, WINDOW, kvw)
                zpad = jnp.zeros((B, WINDOW - T, kvw), F32)
                kk = jnp.concatenate([kbuf, k_new, zpad], axis=1).reshape(B * 2 * WINDOW, kvw)
                vv = jnp.concatenate([vbuf, v_new, zpad], axis=1).reshape(B * 2 * WINDOW, kvw)
                o = _swa_call(W["sinks_b"][j], proj, lambda b, n: (b, 0), kk, lambda b, n: (2 * b, 0),
                              lambda b, n: (2 * b + 1, 0), vv, lambda b, n: (2 * b, 0),
                              lambda b, n: (2 * b + 1, 0), (B, 1), T, B * T, hd, False)
                kc = jnp.concatenate([kbuf[:, T:], k_new], axis=1)
                vc = jnp.concatenate([vbuf[:, T:], v_new], axis=1)
            x = mm("w_out_b", j, D, o, res=x)
            o_k.append(kc.reshape(B, WINDOW, SWA_KV_HEADS, hd))
            o_v.append(vc.reshape(B, WINDOW, SWA_KV_HEADS, hd))
        else:
            d_inner = W["w_out_c"].shape[1]
            heads = d_inner // SSD_HEAD_DIM
            conv_dim = d_inner + 2 * SSD_GROUPS * SSD_STATE
            if fresh:
                s0 = jnp.zeros((B, heads, SSD_HEAD_DIM, SSD_STATE), F32)
                cb0 = jnp.zeros((B, SSD_CONV - 1, conv_dim), F32)
            else:
                s0, cb0 = st["ssm"][j], st["conv"][j]
            proj, dt = mm("w_in_c", j, d_inner + conv_dim, x, gain=W["norm_mix"][i],
                          small=_dt_weight(W["w_in_c"], j, d_inner))
            y, s = _ssd_mix(proj, dt, d_inner, W["conv_w_c"][j], W["conv_b_c"][j], W["dt_bias_c"][j],
                            W["a_log_c"][j], W["d_skip_c"][j], W["g_norm_c"][j], s0, cb0, B, T)
            x = mm("w_out_c", j, D, y, res=x)
            xbc = proj.reshape(B, T, -1)[:, -(SSD_CONV - 1):, d_inner:d_inner + conv_dim]
            o_s.append(s); o_cv.append(xbc)

        final_gain = W["norm_final"] if i == depth - 1 else None
        d_ff = W["w_ffn_out"].shape[1]
        if fresh:
            fb0 = jnp.zeros((B, FFN_CONV - 1, d_ff), F32)
            x, fb = _ffn(x, W["norm_ffn"][i], E["w_ffn", i], W["ffn_conv_w"][i], W["ffn_conv_b"][i],
                         fb0, 1, T, final_gain)
        else:
            xt = x.reshape(B, T, D).transpose(1, 0, 2).reshape(T * B, D)
            fb0 = st["ffn"][i].transpose(1, 0, 2).reshape(1, (FFN_CONV - 1) * B, d_ff)
            xt, fb, E["w_ffn", i] = _ffn(xt, W["norm_ffn"][i], (W["w_ffn_in"], W["w_ffn_out"], i),
                                         W["ffn_conv_w"][i], W["ffn_conv_b"][i], fb0, B, T * B, final_gain)
            x = xt.reshape(T, B, D).transpose(1, 0, 2).reshape(B * T, D)
            fb = fb.reshape(FFN_CONV - 1, B, d_ff).transpose(1, 0, 2)
        o_f.append(fb)
    return (x.reshape(B, T, D), jnp.stack(o_c), jnp.stack(o_n), jnp.stack(o_m), jnp.stack(o_k),
            jnp.stack(o_v), jnp.stack(o_s), jnp.stack(o_cv), jnp.stack(o_f))


def kernel(x_prompt, x_sample, state_mlstm_c, state_mlstm_n, state_mlstm_m, cache_swa_k, cache_swa_v, state_ssm, state_ssm_conv, state_ffn_conv, norm_mix, norm_ffn, norm_final, w_in_a, b_ig_a, b_fg_a, g_head_a, w_out_a, w_in_b, sinks_b, w_out_b, w_in_c, conv_w_c, conv_b_c, dt_bias_c, a_log_c, d_skip_c, g_norm_c, w_out_c, w_ffn_in, ffn_conv_w, ffn_conv_b, w_ffn_out):
    W = dict(norm_mix=norm_mix, norm_ffn=norm_ffn, norm_final=norm_final,
             w_in_a=w_in_a, b_ig_a=b_ig_a, b_fg_a=b_fg_a, g_head_a=g_head_a, w_out_a=w_out_a,
             w_in_b=w_in_b, sinks_b=sinks_b, w_out_b=w_out_b,
             w_in_c=w_in_c, conv_w_c=conv_w_c, conv_b_c=conv_b_c, dt_bias_c=dt_bias_c, a_log_c=a_log_c,
             d_skip_c=d_skip_c, g_norm_c=g_norm_c, w_out_c=w_out_c,
             w_ffn_in=w_ffn_in, ffn_conv_w=ffn_conv_w, ffn_conv_b=ffn_conv_b, w_ffn_out=w_ffn_out)
    st = dict(c=state_mlstm_c, n=state_mlstm_n, m=state_mlstm_m, k=cache_swa_k, v=cache_swa_v,
              ssm=state_ssm, conv=state_ssm_conv, ffn=state_ffn_conv)
    E = {}
    for name, tail in (("w_in_a", 2 * MLSTM_HEADS), ("w_in_c", w_out_c.shape[1] // SSD_HEAD_DIM)):
        for j, wj in enumerate(_main_columns_bf16(W[name], W[name].shape[2] - tail)):
            E[name, j] = wj
    ys, sc, sn, sm, sk, sv, ss, scv, sf = _trunk(x_sample, st, W, E)
    yp, pc, pn, pm, pk, pv, ps, pcv, pf = _trunk(x_prompt, None, W, E)
    return (yp, ys, pc, sc, pn, sn, pm, sm, pk, sk, pv, sv, ps, ss, pcv, scv, pf, sf)
```

```python
import functools
import math

import jax
import jax.numpy as jnp
from jax import lax
from jax.experimental import pallas as pl
from jax.experimental.pallas import tpu as pltpu

F32 = jnp.float32
BF16 = jnp.bfloat16

RMS_EPS = 1e-6
N_MIXERS = 3

MLSTM_HEADS = 8
SWA_HEADS = 32
SWA_KV_HEADS = 4
SWA_GROUP = SWA_HEADS // SWA_KV_HEADS
WINDOW = 128
SSD_HEAD_DIM = 64
SSD_GROUPS = 8
SSD_STATE = 128
SSD_CONV = 4
FFN_CONV = 3

LANES = 128
SUBLANES = 8
VMEM_LIMIT_BYTES = 56 << 20
VMEM_TILE_BUDGET = 44 << 20

FFN_ROW_TILE = 1024
FFN_COL_TILE = 512
MLSTM_CHUNK = 128
SSD_CHUNK = 128


def _cparams(n_axes):
    return pltpu.CompilerParams(dimension_semantics=("arbitrary",) * n_axes,
                                vmem_limit_bytes=VMEM_LIMIT_BYTES)


def _round_up(a, b):
    return (a + b - 1) // b * b


def _largest_divisor(n, candidates):
    for c in candidates:
        if n % c == 0:
            return c
    raise ValueError(f"no tile in {candidates} divides {n}")


def _rmsnorm(x, g):
    return x * lax.rsqrt(jnp.mean(x * x, axis=-1, keepdims=True) + RMS_EPS) * g


def _silu(x):
    return x * jax.nn.sigmoid(x)


def _mm_body(*refs, norm, residual, small, emit):
    it = iter(refs)
    x_ref = next(it)
    g_ref = next(it) if norm else None
    w_ref = next(it)
    w2_ref = next(it) if small else None
    r_ref = next(it) if residual else None
    o_ref = next(it)
    o2_ref = next(it) if small else None
    wb_ref = next(it) if emit else None
    xn_ref = next(it) if norm else None
    first = pl.program_id(1) == 0
    if norm:
        @pl.when(first)
        def _():
            xn_ref[...] = _rmsnorm(x_ref[...], g_ref[...]).astype(BF16)
        lhs = xn_ref[...]
    else:
        lhs = x_ref[...].astype(BF16)
    if small:
        @pl.when(first)
        def _():
            o2_ref[...] = jnp.dot(lhs, w2_ref[...].astype(BF16), preferred_element_type=F32)
    w = w_ref[...]
    if emit:
        w = w.astype(BF16)
        wb_ref[...] = w
    acc = jnp.dot(lhs, w, preferred_element_type=F32)
    if residual:
        acc = acc + r_ref[...]
    o_ref[...] = acc.astype(o_ref.dtype)


def _mm_tiles(m, k, n, ns, x_bytes, w_bytes, norm, residual, emit):
    def need(tm, tn, w_bufs):
        b = 2 * tm * k * x_bytes + w_bufs * k * tn * w_bytes + 2 * tm * tn * 4
        b += tm * k * 2 if norm else 0
        b += 2 * tm * tn * 4 if residual else 0
        b += 3 * k * tn * 2 if emit else 0
        return b + 2 * k * ns * 4 + 2 * tm * ns * 4

    row_tiles = [tm for tm in (1024, 512, 256, 128, 64, 32, 16, 8) if m % tm == 0]
    for tm in row_tiles[:2]:
        if need(tm, n, 1) <= VMEM_TILE_BUDGET:
            return tm, n, 1
    for tm in row_tiles:
        for tn in (2048, 1280, 1024, 896, 512, 256, 128):
            if n % tn == 0 and need(tm, tn, 2) <= VMEM_TILE_BUDGET:
                return tm, tn, 2
    raise ValueError("no matmul tile fits VMEM")


def _w_spec(w, layer, block, index, buffers):
    mode = pl.Buffered(buffers)
    if layer is None:
        return pl.BlockSpec(block, index, pipeline_mode=mode)
    return pl.BlockSpec((None,) + block, lambda *g: (layer,) + index(*g), pipeline_mode=mode)


def _matmul(x, w, layer, n, gain=None, res=None, small=None, emit=False):
    m, k = x.shape
    norm, residual, has_small = gain is not None, res is not None, small is not None
    ns = small.shape[1] if has_small else 0
    tm, tn, w_bufs = _mm_tiles(m, k, n, ns, x.dtype.itemsize, w.dtype.itemsize, norm, residual, emit)
    assert not emit or tm == m
    in_specs = [pl.BlockSpec((tm, k), lambda i, j: (i, 0))]
    args = [x]
    if norm:
        in_specs.append(pl.BlockSpec((1, k), lambda i, j: (0, 0)))
        args.append(gain.reshape(1, k))
    in_specs.append(_w_spec(w, layer, (k, tn), lambda i, j: (0, j), w_bufs))
    args.append(w)
    if has_small:
        in_specs.append(pl.BlockSpec((k, ns), lambda i, j: (0, 0)))
        args.append(small)
    if residual:
        in_specs.append(pl.BlockSpec((tm, tn), lambda i, j: (i, j)))
        args.append(res)
    out_specs = [pl.BlockSpec((tm, tn), lambda i, j: (i, j))]
    out_shape = [jax.ShapeDtypeStruct((m, n), F32)]
    if has_small:
        out_specs.append(pl.BlockSpec((tm, ns), lambda i, j: (i, 0)))
        out_shape.append(jax.ShapeDtypeStruct((m, ns), F32))
    if emit:
        out_specs.append(pl.BlockSpec((k, tn), lambda i, j: (0, j)))
        out_shape.append(jax.ShapeDtypeStruct((k, n), BF16))
    outs = pl.pallas_call(
        functools.partial(_mm_body, norm=norm, residual=residual, small=has_small, emit=emit),
        grid=(m // tm, n // tn),
        in_specs=in_specs,
        out_specs=out_specs,
        out_shape=out_shape,
        scratch_shapes=[pltpu.VMEM((tm, k), BF16)] if norm else [],
        compiler_params=_cparams(2),
        name="matmul",
    )(*args)
    return outs if len(outs) > 1 else outs[0]


def _ffn_body(*refs, tm, shift, halo, tiles_per_seq, nj, final_norm, emit):
    x_ref, gn_ref, wu_ref, wg_ref, cw_ref, cb_ref, wo_ref, buf0_ref, gf_ref, o_ref, bufo_ref = refs[:11]
    wub_ref, wgb_ref, wob_ref = refs[11:14] if emit else (None, None, None)
    xn_ref, gs_ref, carry_ref = refs[-3:]
    i = pl.program_id(0)
    j = pl.program_id(1)

    @pl.when(j == 0)
    def _():
        x = x_ref[...]
        xn_ref[...] = _rmsnorm(x, gn_ref[...]).astype(BF16)
        o_ref[...] = x

    first = (i % tiles_per_seq) == 0
    gs_ref[0:halo, :] = jnp.where(first, buf0_ref[0], carry_ref[j])
    base = halo - (FFN_CONV - 1) * shift
    xn = xn_ref[...]
    wu, wg = wu_ref[...], wg_ref[...]
    if emit:
        wu, wg = wu.astype(BF16), wg.astype(BF16)
        wub_ref[...] = wu
        wgb_ref[...] = wg
    gs_ref[halo:halo + tm, :] = jnp.dot(xn, wg, preferred_element_type=F32)
    u = jnp.dot(xn, wu, preferred_element_type=F32)
    c = gs_ref[base:base + tm, :] * cw_ref[0:1, :]
    for t in range(1, FFN_CONV):
        c = c + gs_ref[base + t * shift:base + t * shift + tm, :] * cw_ref[t:t + 1, :]
    c = c + cb_ref[...]
    h = (_silu(c) * u).astype(BF16)
    tail = gs_ref[tm:tm + halo, :]
    carry_ref[j] = tail
    bufo_ref[0] = tail
    wo = wo_ref[...]
    if emit:
        wo = wo.astype(BF16)
        wob_ref[...] = wo
    o_ref[...] += jnp.dot(h, wo, preferred_element_type=F32)

    if final_norm:
        @pl.when(j == nj - 1)
        def _():
            o_ref[...] = _rmsnorm(o_ref[...], gf_ref[...])


def _ffn(x, gain, weights, conv_w, conv_b, buf0, shift, rows_per_seq, final_gain):
    m, d = x.shape
    emit = weights[0].ndim == 3
    d_ff = weights[1].shape[1] if emit else weights[2].shape[0]
    tm = _largest_divisor(rows_per_seq, (FFN_ROW_TILE, 512, 256, 128, 64, 32, 16, 8))
    tf = _largest_divisor(d_ff, (FFN_COL_TILE, 256, 128))
    nj = d_ff // tf
    if emit:
        w_in, w_out, layer = weights
        assert m == tm
        w_args = [w_in, w_in, w_out]
        w_specs = [pl.BlockSpec((None, d, tf), lambda i, j: (layer, 0, j)),
                   pl.BlockSpec((None, d, tf), lambda i, j: (layer, 0, nj + j)),
                   pl.BlockSpec((None, tf, d), lambda i, j: (layer, j, 0))]
    else:
        w_args = list(weights)
        w_specs = [pl.BlockSpec((d, tf), lambda i, j: (0, j)),
                   pl.BlockSpec((d, tf), lambda i, j: (0, j)),
                   pl.BlockSpec((tf, d), lambda i, j: (j, 0))]
    tiles_per_seq = rows_per_seq // tm
    pre = (FFN_CONV - 1) * shift
    halo = _round_up(pre, SUBLANES)
    assert tm >= halo and tm % SUBLANES == 0
    buf0p = jnp.pad(buf0, ((0, 0), (halo - pre, 0), (0, 0)))
    final_norm = final_gain is not None
    gf = (final_gain if final_norm else gain).reshape(1, d)
    body = functools.partial(_ffn_body, tm=tm, shift=shift, halo=halo, tiles_per_seq=tiles_per_seq,
                             nj=nj, final_norm=final_norm, emit=emit)
    out_specs = [pl.BlockSpec((tm, d), lambda i, j: (i, 0)),
                 pl.BlockSpec((1, halo, tf), lambda i, j: (i, 0, j))]
    out_shape = [jax.ShapeDtypeStruct((m, d), F32),
                 jax.ShapeDtypeStruct((m // tm, halo, d_ff), F32)]
    if emit:
        out_specs += [pl.BlockSpec((d, tf), lambda i, j: (0, j)),
                      pl.BlockSpec((d, tf), lambda i, j: (0, j)),
                      pl.BlockSpec((tf, d), lambda i, j: (j, 0))]
        out_shape += [jax.ShapeDtypeStruct((d, d_ff), BF16), jax.ShapeDtypeStruct((d, d_ff), BF16),
                      jax.ShapeDtypeStruct((d_ff, d), BF16)]
    outs = pl.pallas_call(
        body,
        grid=(m // tm, nj),
        in_specs=[
            pl.BlockSpec((tm, d), lambda i, j: (i, 0)),
            pl.BlockSpec((1, d), lambda i, j: (0, 0)),
            w_specs[0],
            w_specs[1],
            pl.BlockSpec((FFN_CONV, tf), lambda i, j: (0, j)),
            pl.BlockSpec((1, tf), lambda i, j: (0, j)),
            w_specs[2],
            pl.BlockSpec((1, halo, tf), lambda i, j: (i // tiles_per_seq, 0, j)),
            pl.BlockSpec((1, d), lambda i, j: (0, 0)),
        ],
        out_specs=out_specs,
        out_shape=out_shape,
        scratch_shapes=[pltpu.VMEM((tm, d), BF16),
                        pltpu.VMEM((halo + tm, tf), F32),
                        pltpu.VMEM((nj, halo, tf), F32)],
        compiler_params=_cparams(2),
        name="conv_ffn",
    )(x, gain.reshape(1, d), w_args[0], w_args[1], conv_w, conv_b.reshape(1, d_ff), w_args[2], buf0p, gf)
    result = (outs[0], outs[1][tiles_per_seq - 1::tiles_per_seq, halo - pre:, :])
    return result + ((tuple(outs[2:]),) if emit else ())


def _row_from_col(col, eye):
    return jnp.sum(jnp.where(eye, col, 0.0), axis=0, keepdims=True)


def _rowsum_lanes(x):
    parts = jnp.concatenate(_split3_bf16(x), axis=1)
    return jnp.dot(parts, jnp.ones((parts.shape[1], LANES), BF16), preferred_element_type=F32)


def _cumsum_rows(x, tril):
    return jnp.dot(tril, x, preferred_element_type=F32, precision=lax.Precision.HIGHEST)


def _mlstm_body(*refs, chunk, dk, dv, has_init):
    q_ref, k_ref, v_ref, o_ref, gate_ref, bias_ref, gh_ref = refs[:7]
    c0_ref, n0_ref, m0_ref = refs[7:10] if has_init else (None, None, None)
    hg_ref, c_ref, n_ref, m_ref = refs[-4:]
    L = chunk
    H = MLSTM_HEADS

    @pl.when(pl.program_id(1) == 0)
    def _():
        if has_init:
            c_ref[...] = c0_ref[...]
            n_ref[...] = n0_ref[...]
            m_ref[...] = m0_ref[...]
        else:
            c_ref[...] = jnp.zeros_like(c_ref)
            n_ref[...] = jnp.zeros_like(n_ref)
            m_ref[...] = jnp.zeros_like(m_ref)

    ri = lax.broadcasted_iota(jnp.int32, (L, L), 0)
    ci = lax.broadcasted_iota(jnp.int32, (L, L), 1)
    causal = ri >= ci
    eye = ri == ci
    tril = causal.astype(F32)

    gates = gate_ref[...] + bias_ref[...]
    x = gates
    log_sig = jnp.minimum(x, 0.0) - jnp.log1p(jnp.exp(-jnp.abs(x)))
    bcum = _cumsum_rows(log_sig, tril)

    n_all = n_ref[0]
    m_all = m_ref[0]
    stack = lambda parts: jnp.concatenate(parts, axis=0)
    if L % LANES == 0:
        gates_t, bcum_t = gates.T, bcum.T
        row_of = lambda arr_t, col, r: arr_t[r:r + 1, :]
    else:
        gates_t = bcum_t = None
        row_of = lambda arr_t, col, r: _row_from_col(col, eye)
    ig_cols = [gates[:, h:h + 1] for h in range(H)]
    b_cols = [bcum[:, H + h:H + h + 1] for h in range(H)]
    m_prevs = [m_all[h:h + 1, 0:1] for h in range(H)]
    ig_col, b_col = stack(ig_cols), stack(b_cols)
    m_prev = stack([jnp.broadcast_to(mp, (L, 1)) for mp in m_prevs])
    ig_row = stack([jnp.broadcast_to(row_of(gates_t, ig_cols[h], h), (L, L)) for h in range(H)])
    b_row = stack([jnp.broadcast_to(row_of(bcum_t, b_cols[h], H + h), (L, L)) for h in range(H)])
    causal_all = stack([causal] * H)

    log_d = jnp.where(causal_all, b_col - b_row + ig_row, -jnp.inf)
    log_inter = b_col + m_prev
    m_t = jnp.maximum(log_inter, jnp.max(log_d, axis=-1, keepdims=True))
    d = jnp.exp(log_d - m_t)
    inter = jnp.exp(log_inter - m_t)

    qk, qc, qn, kfs, vbs, c_prevs = [], [], [], [], [], []
    for h in range(H):
        qf = q_ref[:, h * dk:(h + 1) * dk] * (dk ** -0.5)
        kf = k_ref[:, h * dk:(h + 1) * dk]
        qb = qf.astype(BF16)
        c_prev = c_ref[0, h]
        qk.append(lax.dot_general(qb, kf.astype(BF16), (((1,), (1,)), ((), ())), preferred_element_type=F32))
        qc.append(jnp.dot(qb, c_prev.astype(BF16), preferred_element_type=F32))
        qn.append(qf * n_all[h:h + 1, :])
        kfs.append(kf)
        vbs.append(v_ref[:, h * dv:(h + 1) * dv].astype(BF16))
        c_prevs.append(c_prev)

    s = stack(qk) * d
    sb = s.astype(BF16)
    sv = stack([jnp.dot(sb[h * L:(h + 1) * L], vbs[h], preferred_element_type=F32) for h in range(H)])
    num = inter * stack(qc) + sv
    if L % LANES == 0:
        rep = lambda col: jnp.broadcast_to(col, (H * L, LANES))
        wide = lambda a: jnp.concatenate([a] * (dv // LANES), axis=1)
        rowsum = _rowsum_lanes
    else:
        rep = wide = lambda a: a
        rowsum = lambda a: jnp.sum(a, axis=-1, keepdims=True)
    den = rep(inter) * rowsum(stack(qn)) + rowsum(s)
    hh = num / wide(jnp.maximum(jnp.abs(den), rep(jnp.exp(-m_t))))
    hn = hh * wide(lax.rsqrt(rowsum(hh * hh) * (1.0 / dv) + RMS_EPS))
    for h in range(H):
        lanes = slice(h * dv, (h + 1) * dv)
        out = jax.nn.sigmoid(o_ref[:, lanes]) * (hn[h * L:(h + 1) * L] * gh_ref[:, lanes])
        hg_ref[:, lanes] = out.astype(hg_ref.dtype)

    b_lasts = [bc[L - 1:L, :] for bc in b_cols]
    log_w = stack([jnp.broadcast_to(bl, (L, 1)) for bl in b_lasts]) - b_col + ig_col
    m_news = [jnp.maximum(b_lasts[h] + m_prevs[h], jnp.max(log_w[h * L:(h + 1) * L], axis=0, keepdims=True))
              for h in range(H)]
    w = jnp.exp(log_w - stack([jnp.broadcast_to(mn, (L, 1)) for mn in m_news]))

    head_row = lax.broadcasted_iota(jnp.int32, (H, LANES), 0)
    n_new, m_new_all = n_all, m_all
    for h in range(H):
        decay = jnp.exp(b_lasts[h] + m_prevs[h] - m_news[h])
        kw = kfs[h] * w[h * L:(h + 1) * L]
        upd = lax.dot_general(kw.astype(BF16), vbs[h], (((0,), (0,)), ((), ())), preferred_element_type=F32)
        c_ref[0, h] = decay * c_prevs[h] + upd
        n_row = decay * n_all[h:h + 1, :] + jnp.sum(kw, axis=0, keepdims=True)
        n_new = jnp.where(head_row == h, n_row, n_new)
        m_new_all = jnp.where(head_row == h, m_news[h], m_new_all)
    n_ref[0] = n_new
    m_ref[0] = m_new_all


def _mlstm_mix(proj, gates, bias, g_head, init, layer, B, T, dk, dv):
    H = MLSTM_HEADS
    L = math.gcd(T, MLSTM_CHUNK)
    nc = T // L
    qw, vw = H * dk, H * dv
    assert qw * 2 == vw and vw % LANES == 0
    hg_dtype = BF16 if L % 16 == 0 else F32
    row = lambda b, c: b * nc + c
    c_spec = pl.BlockSpec((None, 1, H, dk, dv), lambda b, c: (layer, b, 0, 0, 0))
    in_specs = [
        pl.BlockSpec((L, qw), lambda b, c: (row(b, c), 0)),
        pl.BlockSpec((L, qw), lambda b, c: (row(b, c), 1)),
        pl.BlockSpec((L, vw), lambda b, c: (row(b, c), 1)),
        pl.BlockSpec((L, vw), lambda b, c: (row(b, c), 2)),
        pl.BlockSpec((L, LANES), lambda b, c: (row(b, c), 0)),
        pl.BlockSpec((1, LANES), lambda b, c: (0, 0)),
        pl.BlockSpec((1, vw), lambda b, c: (0, 0)),
    ]
    args = [proj, proj, proj, proj, gates, bias, g_head.reshape(1, vw)]
    if init is not None:
        c_all, n0, m0 = init
        in_specs += [c_spec,
                     pl.BlockSpec((1, H, dk), lambda b, c: (b, 0, 0)),
                     pl.BlockSpec((1, H, LANES), lambda b, c: (b, 0, 0))]
        args += [c_all, n0, jnp.broadcast_to(m0[:, :, None], (B, H, LANES))]
    hg, c, n, m = pl.pallas_call(
        functools.partial(_mlstm_body, chunk=L, dk=dk, dv=dv, has_init=init is not None),
        grid=(B, nc),
        in_specs=in_specs,
        out_specs=[
            pl.BlockSpec((L, vw), lambda b, c: (row(b, c), 0)),
            pl.BlockSpec((1, H, dk, dv), lambda b, c: (b, 0, 0, 0)),
            pl.BlockSpec((1, H, dk), lambda b, c: (b, 0, 0)),
            pl.BlockSpec((1, H, LANES), lambda b, c: (b, 0, 0)),
        ],
        out_shape=[jax.ShapeDtypeStruct((B * T, vw), hg_dtype),
                   jax.ShapeDtypeStruct((B, H, dk, dv), F32),
                   jax.ShapeDtypeStruct((B, H, dk), F32),
                   jax.ShapeDtypeStruct((B, H, LANES), F32)],
        compiler_params=_cparams(2),
        name="mlstm",
    )(*args)
    return hg, c, n, m[:, :, 0]


def _swa_body(sink_ref, q_ref, kp_ref, kc_ref, vp_ref, vc_ref, o_ref, *, tq, hd, mask_first):
    W = WINDOW
    G = SWA_GROUP
    S = 2 * W
    kk = jnp.concatenate([kp_ref[...], kc_ref[...]], axis=0)
    vv = jnp.concatenate([vp_ref[...], vc_ref[...]], axis=0)
    qi = lax.broadcasted_iota(jnp.int32, (tq, S), 0)
    kj = lax.broadcasted_iota(jnp.int32, (tq, S), 1)
    dist = W + qi - kj
    valid = (dist >= 0) & (dist <= W)
    if mask_first:
        valid = valid & ((kj >= W) | (pl.program_id(1) > 0))
    distf = dist.astype(F32)
    lane_s = lax.broadcasted_iota(jnp.int32, (S, LANES), 1)
    lane_q = lax.broadcasted_iota(jnp.int32, (tq, LANES), 1)
    low_s = lane_s < hd
    low_q = lane_q < hd
    heads_per_slab = LANES // hd
    assert heads_per_slab == 2

    for kh in range(SWA_KV_HEADS):
        slab = (kh // heads_per_slab) * LANES
        keep_low = (kh % heads_per_slab) == 0
        k128 = kk[:, slab:slab + LANES]
        v128 = vv[:, slab:slab + LANES]
        k_rot = pltpu.roll(k128, hd, axis=1)
        v_rot = pltpu.roll(v128, hd, axis=1)
        own = low_s if keep_low else jnp.logical_not(low_s)
        k2 = jnp.where(own, k128, k_rot).astype(BF16)
        v2 = jnp.where(own, v128, v_rot).astype(BF16)

        q_parts = []
        for g in range(G):
            h = kh * G + g
            qs = q_ref[:, (h // 2) * LANES:(h // 2 + 1) * LANES]
            keep = low_q if h % 2 == 0 else jnp.logical_not(low_q)
            q_parts.append(jnp.where(keep, qs, 0.0))
        q_stack = jnp.concatenate(q_parts, axis=0).astype(BF16)
        s_all = lax.dot_general(q_stack, k2, (((1,), (1,)), ((), ())), preferred_element_type=F32)
        s_all = s_all * (hd ** -0.5)

        p_parts = []
        for g in range(G):
            h = kh * G + g
            slope = 2.0 ** (-8.0 * (h + 1) / SWA_HEADS)
            sink = sink_ref[h]
            s = s_all[g * tq:(g + 1) * tq, :] - slope * distf
            s = jnp.where(valid, s, -jnp.inf)
            mx = jnp.maximum(jnp.max(s, axis=-1, keepdims=True), sink)
            p = jnp.exp(s - mx)
            p = p / (jnp.sum(p, axis=-1, keepdims=True) + jnp.exp(sink - mx))
            p_parts.append(p)
        p_stack = jnp.concatenate(p_parts, axis=0).astype(BF16)
        o_all = jnp.dot(p_stack, v2, preferred_element_type=F32)

        for g2 in range(G // 2):
            h = kh * G + 2 * g2
            o_even = o_all[(2 * g2) * tq:(2 * g2 + 1) * tq, :]
            o_odd = o_all[(2 * g2 + 1) * tq:(2 * g2 + 2) * tq, :]
            o_ref[:, (h // 2) * LANES:(h // 2 + 1) * LANES] = jnp.where(low_q, o_even, o_odd).astype(o_ref.dtype)


def _swa_call(sinks, q_arr, q_blk, k_arr, kp_map, kc_map, v_arr, vp_map, vc_map, grid, tq, n_rows, hd,
              mask_first):
    kvw = SWA_KV_HEADS * hd
    qw = SWA_HEADS * hd
    out_dtype = BF16 if tq % 16 == 0 else F32
    return pl.pallas_call(
        functools.partial(_swa_body, tq=tq, hd=hd, mask_first=mask_first),
        grid=grid,
        in_specs=[
            pl.BlockSpec(memory_space=pltpu.SMEM),
            pl.BlockSpec((tq, qw), q_blk),
            pl.BlockSpec((WINDOW, kvw), kp_map),
            pl.BlockSpec((WINDOW, kvw), kc_map),
            pl.BlockSpec((WINDOW, kvw), vp_map),
            pl.BlockSpec((WINDOW, kvw), vc_map),
        ],
        out_specs=pl.BlockSpec((tq, qw), q_blk),
        out_shape=jax.ShapeDtypeStruct((n_rows, qw), out_dtype),
        compiler_params=_cparams(2),
        name="swa",
    )(sinks, q_arr, k_arr, k_arr, v_arr, v_arr)


def _conv_silu(u_ref, init_ref, carry_ref, gs_ref, cw_ref, cb_ref, first, L):
    halo = SUBLANES
    gs_ref[0:halo, :] = jnp.where(first, init_ref[0], carry_ref[...])
    gs_ref[halo:halo + L, :] = u_ref[...]
    carry_ref[...] = gs_ref[L:L + halo, :]
    base = halo - (SSD_CONV - 1)
    cw = cw_ref[...]
    c = gs_ref[base:base + L, :] * cw[0:1, :]
    for t in range(1, SSD_CONV):
        c = c + gs_ref[base + t:base + t + L, :] * cw[t:t + 1, :]
    c = c + cb_ref[...]
    return _silu(c)


def _split3_bf16(x):
    hi = x.astype(BF16)
    r = x - hi.astype(F32)
    mid = r.astype(BF16)
    lo = (r - mid.astype(F32)).astype(BF16)
    return hi, mid, lo


def _ssd_body(z_ref, xs_ref, bc_ref, dt_ref, dtb_ref, alog_ref, ex_ref, dsk_ref, gn_ref,
              cwx_ref, cwbc_ref, cbx_ref, cbbc_ref, ix_ref, ibc_ref, s0_ref,
              y_ref, s_ref,
              gx_ref, gbc_ref, kx_ref, kbc_ref, *, chunk, groups, hpg, p_dim):
    L = chunk
    N = SSD_STATE
    gw = hpg * p_dim
    first = pl.program_id(1) == 0

    @pl.when(first)
    def _():
        s_ref[...] = s0_ref[...]

    xs_all = _conv_silu(xs_ref, ix_ref, kx_ref, gx_ref, cwx_ref, cbx_ref, first, L)
    bc_all = _conv_silu(bc_ref, ibc_ref, kbc_ref, gbc_ref, cwbc_ref, cbbc_ref, first, L)

    ri = lax.broadcasted_iota(jnp.int32, (L, L), 0)
    ci = lax.broadcasted_iota(jnp.int32, (L, L), 1)
    causal = ri >= ci
    eye = ri == ci
    tril = causal.astype(F32)

    dt = jax.nn.softplus(dt_ref[...] + dtb_ref[...])
    acum = _cumsum_rows(dt * (-jnp.exp(alog_ref[...])), tril)
    a_last = acum[L - 1:L, :]
    eacum = jnp.exp(acum)
    dtw = dt * jnp.exp(a_last - acum)
    ea_last = jnp.exp(a_last)
    acum_t = acum.T if L % LANES == 0 else None

    per_head = jnp.concatenate([dt, eacum, dtw], axis=0)
    ex = jnp.dot(jnp.concatenate(_split3_bf16(per_head), axis=1), ex_ref[...],
                 preferred_element_type=F32)

    lane = lax.broadcasted_iota(jnp.int32, (L, LANES), 1)
    low = lane < p_dim
    srow = lax.broadcasted_iota(jnp.int32, (LANES, 1), 0)
    heads_per_slab = LANES // p_dim
    assert heads_per_slab == 2

    gated = []
    for g in range(groups):
        lanes_g = slice(g * gw, (g + 1) * gw)
        dt_e, ea_e, dtw_e = ex[0:L, lanes_g], ex[L:2 * L, lanes_g], ex[2 * L:3 * L, lanes_g]

        xs = xs_all[:, g * gw:(g + 1) * gw]
        bmb = bc_all[:, g * N:(g + 1) * N].astype(BF16)
        cmb = bc_all[:, (groups + g) * N:(groups + g + 1) * N].astype(BF16)
        cb = lax.dot_general(cmb, bmb, (((1,), (1,)), ((), ())), preferred_element_type=F32)
        xdt = (xs * dt_e).astype(BF16)
        xw = (xs * dtw_e).astype(BF16)
        s_prev = s_ref[0, g * gw:(g + 1) * gw, :]
        y_inter = lax.dot_general(cmb, s_prev.astype(BF16), (((1,), (1,)), ((), ())),
                                  preferred_element_type=F32)
        upd = lax.dot_general(xw, bmb, (((0,), (0,)), ((), ())), preferred_element_type=F32)

        y_parts = []
        for pr in range(hpg // heads_per_slab):
            h0 = g * hpg + 2 * pr
            ms = []
            for hh in (h0, h0 + 1):
                a_col = acum[:, hh:hh + 1]
                a_row = acum_t[hh:hh + 1, :] if acum_t is not None else _row_from_col(a_col, eye)
                dec = jnp.exp(jnp.where(causal, a_col - a_row, -jnp.inf))
                ms.append((cb * dec).astype(BF16))
            mcat = jnp.concatenate(ms, axis=1)
            xb = xdt[:, pr * LANES:(pr + 1) * LANES]
            zero = jnp.zeros_like(xb)
            bd = jnp.concatenate([jnp.where(low, xb, zero), jnp.where(low, zero, xb)], axis=0)
            y_parts.append(jnp.dot(mcat, bd, preferred_element_type=F32))

            scale = jnp.where(srow < p_dim, ea_last[:, h0:h0 + 1], ea_last[:, h0 + 1:h0 + 2])
            rows = slice(g * gw + pr * LANES, g * gw + (pr + 1) * LANES)
            s_ref[0, rows, :] = s_prev[pr * LANES:(pr + 1) * LANES] * scale + upd[pr * LANES:(pr + 1) * LANES]

        y = jnp.concatenate(y_parts, axis=1) + y_inter * ea_e
        y = y + dsk_ref[:, g * gw:(g + 1) * gw] * xs
        gated.append(y * _silu(z_ref[:, g * gw:(g + 1) * gw]))

    sq = gated[0] * gated[0]
    for y in gated[1:]:
        sq = sq + y * y
    inv = lax.rsqrt(jnp.sum(sq, axis=-1, keepdims=True) * (1.0 / (groups * gw)) + RMS_EPS)
    for g, y in enumerate(gated):
        y_ref[:, g * gw:(g + 1) * gw] = (y * inv * gn_ref[:, g * gw:(g + 1) * gw]).astype(y_ref.dtype)


def _ssd_mix(proj, dt, d_inner, conv_w, conv_b, dt_bias, a_log, d_skip, g_norm, s0, conv0, B, T):
    G, P, N = SSD_GROUPS, SSD_HEAD_DIM, SSD_STATE
    heads = d_inner // P
    hpg = heads // G
    L = math.gcd(T, SSD_CHUNK)
    nc = T // L
    bcw = 2 * G * N
    dtw = LANES
    conv_dim = d_inner + bcw
    assert d_inner % bcw == 0 and heads <= LANES
    bc_blk = (2 * d_inner) // bcw
    pad_heads = lambda v: jnp.pad(v, (0, LANES - heads)).reshape(1, LANES)
    expand = (jnp.arange(3 * LANES)[:, None] % LANES == jnp.arange(d_inner)[None, :] // P).astype(BF16)
    halo = SUBLANES
    conv0p = jnp.pad(conv0, ((0, 0), (halo - (SSD_CONV - 1), 0), (0, 0)))
    conv_b2 = conv_b.reshape(1, conv_dim)
    dsk = jnp.repeat(d_skip, P).reshape(1, d_inner)
    row = lambda b, c: b * nc + c
    const = lambda b, c: (0, 0)
    y, s = pl.pallas_call(
        functools.partial(_ssd_body, chunk=L, groups=G, hpg=hpg, p_dim=P),
        grid=(B, nc),
        in_specs=[
            pl.BlockSpec((L, d_inner), lambda b, c: (row(b, c), 0)),
            pl.BlockSpec((L, d_inner), lambda b, c: (row(b, c), 1)),
            pl.BlockSpec((L, bcw), lambda b, c: (row(b, c), bc_blk)),
            pl.BlockSpec((L, dtw), lambda b, c: (row(b, c), 0)),
            pl.BlockSpec((1, dtw), const),
            pl.BlockSpec((1, dtw), const),
            pl.BlockSpec((3 * LANES, d_inner), const),
            pl.BlockSpec((1, d_inner), const),
            pl.BlockSpec((1, d_inner), const),
            pl.BlockSpec((SSD_CONV, d_inner), const),
            pl.BlockSpec((SSD_CONV, bcw), lambda b, c: (0, d_inner // bcw)),
            pl.BlockSpec((1, d_inner), const),
            pl.BlockSpec((1, bcw), lambda b, c: (0, d_inner // bcw)),
            pl.BlockSpec((1, halo, d_inner), lambda b, c: (b, 0, 0)),
            pl.BlockSpec((1, halo, bcw), lambda b, c: (b, 0, d_inner // bcw)),
            pl.BlockSpec((1, d_inner, N), lambda b, c: (b, 0, 0)),
        ],
        out_specs=[
            pl.BlockSpec((L, d_inner), lambda b, c: (row(b, c), 0)),
            pl.BlockSpec((1, d_inner, N), lambda b, c: (b, 0, 0)),
        ],
        out_shape=[jax.ShapeDtypeStruct((B * T, d_inner), BF16 if L % 16 == 0 else F32),
                   jax.ShapeDtypeStruct((B, d_inner, N), F32)],
        scratch_shapes=[pltpu.VMEM((halo + L, d_inner), F32), pltpu.VMEM((halo + L, bcw), F32),
                        pltpu.VMEM((halo, d_inner), F32), pltpu.VMEM((halo, bcw), F32)],
        compiler_params=_cparams(2),
        name="ssd",
    )(proj, proj, proj, dt, pad_heads(dt_bias), pad_heads(a_log), expand, dsk, g_norm.reshape(1, d_inner),
      conv_w, conv_w, conv_b2, conv_b2, conv0p, conv0p, s0.reshape(B, d_inner, N))
    return y, s.reshape(B, heads, P, N)


def _gate_weight(w, j):
    gates = 2 * MLSTM_HEADS
    return jnp.pad(w[j, :, -gates:], ((0, 0), (0, LANES - gates)))


def _dt_weight(w, j, d_inner):
    heads = d_inner // SSD_HEAD_DIM
    return jnp.pad(w[j, :, -heads:], ((0, 0), (0, LANES - heads)))


def _main_columns_bf16(w, n):
    return [w[j, :, :n].astype(BF16) for j in range(w.shape[0])]


def _trunk(x3, st, W, E):
    B, T, D = x3.shape
    fresh = st is None

    def mm(name, j, n, xin, **kw):
        if (name, j) in E:
            return _matmul(xin, E[name, j], None, n, **kw)
        outs = _matmul(xin, W[name], j, n, emit=True, **kw)
        E[name, j] = outs[-1]
        return outs[0] if len(outs) == 2 else outs[:-1]

    x = x3.reshape(B * T, D)
    depth = W["norm_mix"].shape[0]
    o_c, o_n, o_m, o_k, o_v, o_s, o_cv, o_f = [], [], [], [], [], [], [], []
    for i in range(depth):
        j = i // N_MIXERS
        kind = i % N_MIXERS
        if kind == 0:
            dv = D // MLSTM_HEADS
            dk = dv // 2
            init = None if fresh else (st["c"], st["n"][j], st["m"][j])
            proj, gates = mm("w_in_a", j, 2 * MLSTM_HEADS * (dk + dv), x, gain=W["norm_mix"][i],
                             small=_gate_weight(W["w_in_a"], j))
            bias = jnp.pad(jnp.concatenate([W["b_ig_a"][j], W["b_fg_a"][j]]), (0, LANES - 2 * MLSTM_HEADS))
            hg, c, n, m = _mlstm_mix(proj, gates, bias.reshape(1, LANES), W["g_head_a"][j], init, j,
                                     B, T, dk, dv)
            x = mm("w_out_a", j, D, hg, res=x)
            o_c.append(c); o_n.append(n); o_m.append(m)
        elif kind == 1:
            hd = D // SWA_HEADS
            kvw = SWA_KV_HEADS * hd
            qw = SWA_HEADS * hd
            proj = mm("w_in_b", j, qw + 2 * kvw, x, gain=W["norm_mix"][i])
            p3 = proj.reshape(B, T, qw + 2 * kvw)
            k_new, v_new = p3[:, :, qw:qw + kvw], p3[:, :, qw + kvw:]
            kb, vb = qw // kvw, qw // kvw + 1
            if fresh:
                nb = T // WINDOW
                prev = lambda b, n: (b * nb + jnp.maximum(n - 1, 0), kb)
                cur = lambda b, n: (b * nb + n, kb)
                prev_v = lambda b, n: (b * nb + jnp.maximum(n - 1, 0), vb)
                cur_v = lambda b, n: (b * nb + n, vb)
                o = _swa_call(W["sinks_b"][j], proj, lambda b, n: (b * nb + n, 0), proj, prev, cur,
                              proj, prev_v, cur_v, (B, nb), WINDOW, B * T, hd, True)
                kc, vc = k_new[:, -WINDOW:], v_new[:, -WINDOW:]
            else:
                kbuf = st["k"][j].reshape(B, WINDOW, kvw)
                vbuf = st["v"][j].reshape(B, WINDOW, kvw)
                zpad = jnp.zeros((B, WINDOW - T, kvw), F32)
                kk = jnp.concatenate([kbuf, k_new, zpad], axis=1).reshape(B * 2 * WINDOW, kvw)
                vv = jnp.concatenate([vbuf, v_new, zpad], axis=1).reshape(B * 2 * WINDOW, kvw)
                o = _swa_call(W["sinks_b"][j], proj, lambda b, n: (b, 0), kk, lambda b, n: (2 * b, 0),
                              lambda b, n: (2 * b + 1, 0), vv, lambda b, n: (2 * b, 0),
                              lambda b, n: (2 * b + 1, 0), (B, 1), T, B * T, hd, False)
                kc = jnp.concatenate([kbuf[:, T:], k_new], axis=1)
                vc = jnp.concatenate([vbuf[:, T:], v_new], axis=1)
            x = mm("w_out_b", j, D, o, res=x)
            o_k.append(kc.reshape(B, WINDOW, SWA_KV_HEADS, hd))
            o_v.append(vc.reshape(B, WINDOW, SWA_KV_HEADS, hd))
        else:
            d_inner = W["w_out_c"].shape[1]
            heads = d_inner // SSD_HEAD_DIM
            conv_dim = d_inner + 2 * SSD_GROUPS * SSD_STATE
            if fresh:
                s0 = jnp.zeros((B, heads, SSD_HEAD_DIM, SSD_STATE), F32)
                cb0 = jnp.zeros((B, SSD_CONV - 1, conv_dim), F32)
            else:
                s0, cb0 = st["ssm"][j], st["conv"][j]
            proj, dt = mm("w_in_c", j, d_inner + conv_dim, x, gain=W["norm_mix"][i],
                          small=_dt_weight(W["w_in_c"], j, d_inner))
            y, s = _ssd_mix(proj, dt, d_inner, W["conv_w_c"][j], W["conv_b_c"][j], W["dt_bias_c"][j],
                            W["a_log_c"][j], W["d_skip_c"][j], W["g_norm_c"][j], s0, cb0, B, T)
            x = mm("w_out_c", j, D, y, res=x)
            xbc = proj.reshape(B, T, -1)[:, -(SSD_CONV - 1):, d_inner:d_inner + conv_dim]
            o_s.append(s); o_cv.append(xbc)

        final_gain = W["norm_final"] if i == depth - 1 else None
        d_ff = W["w_ffn_out"].shape[1]
        if fresh:
            fb0 = jnp.zeros((B, FFN_CONV - 1, d_ff), F32)
            x, fb = _ffn(x, W["norm_ffn"][i], E["w_ffn", i], W["ffn_conv_w"][i], W["ffn_conv_b"][i],
                         fb0, 1, T, final_gain)
        else:
            xt = x.reshape(B, T, D).transpose(1, 0, 2).reshape(T * B, D)
            fb0 = st["ffn"][i].transpose(1, 0, 2).reshape(1, (FFN_CONV - 1) * B, d_ff)
            xt, fb, E["w_ffn", i] = _ffn(xt, W["norm_ffn"][i], (W["w_ffn_in"], W["w_ffn_out"], i),
                                         W["ffn_conv_w"][i], W["ffn_conv_b"][i], fb0, B, T * B, final_gain)
            x = xt.reshape(T, B, D).transpose(1, 0, 2).reshape(B * T, D)
            fb = fb.reshape(FFN_CONV - 1, B, d_ff).transpose(1, 0, 2)
        o_f.append(fb)
    return (x.reshape(B, T, D), jnp.stack(o_c), jnp.stack(o_n), jnp.stack(o_m), jnp.stack(o_k),
            jnp.stack(o_v), jnp.stack(o_s), jnp.stack(o_cv), jnp.stack(o_f))


def kernel(x_prompt, x_sample, state_mlstm_c, state_mlstm_n, state_mlstm_m, cache_swa_k, cache_swa_v, state_ssm, state_ssm_conv, state_ffn_conv, norm_mix, norm_ffn, norm_final, w_in_a, b_ig_a, b_fg_a, g_head_a, w_out_a, w_in_b, sinks_b, w_out_b, w_in_c, conv_w_c, conv_b_c, dt_bias_c, a_log_c, d_skip_c, g_norm_c, w_out_c, w_ffn_in, ffn_conv_w, ffn_conv_b, w_ffn_out):
    W = dict(norm_mix=norm_mix, norm_ffn=norm_ffn, norm_final=norm_final,
             w_in_a=w_in_a, b_ig_a=b_ig_a, b_fg_a=b_fg_a, g_head_a=g_head_a, w_out_a=w_out_a,
             w_in_b=w_in_b, sinks_b=sinks_b, w_out_b=w_out_b,
             w_in_c=w_in_c, conv_w_c=conv_w_c, conv_b_c=conv_b_c, dt_bias_c=dt_bias_c, a_log_c=a_log_c,
             d_skip_c=d_skip_c, g_norm_c=g_norm_c, w_out_c=w_out_c,
             w_ffn_in=w_ffn_in, ffn_conv_w=ffn_conv_w, ffn_conv_b=ffn_conv_b, w_ffn_out=w_ffn_out)
    st = dict(c=state_mlstm_c, n=state_mlstm_n, m=state_mlstm_m, k=cache_swa_k, v=cache_swa_v,
              ssm=state_ssm, conv=state_ssm_conv, ffn=state_ffn_conv)
    E = {}
    for name, tail in (("w_in_a", 2 * MLSTM_HEADS), ("w_in_c", w_out_c.shape[1] // SSD_HEAD_DIM)):
        for j, wj in enumerate(_main_columns_bf16(W[name], W[name].shape[2] - tail)):
            E[name, j] = wj
    ys, sc, sn, sm, sk, sv, ss, scv, sf = _trunk(x_sample, st, W, E)
    yp, pc, pn, pm, pk, pv, ps, pcv, pf = _trunk(x_prompt, None, W, E)
    return (yp, ys, pc, sc, pn, sn, pm, sm, pk, sk, pv, sv, ps, ss, pcv, scv, pf, sf)
```

```python
import functools
import math

import jax
import jax.numpy as jnp
from jax import lax
from jax.experimental import pallas as pl
from jax.experimental.pallas import tpu as pltpu

F32 = jnp.float32
BF16 = jnp.bfloat16

RMS_EPS = 1e-6
N_MIXERS = 3

MLSTM_HEADS = 8
SWA_HEADS = 32
SWA_KV_HEADS = 4
SWA_GROUP = SWA_HEADS // SWA_KV_HEADS
WINDOW = 128
SSD_HEAD_DIM = 64
SSD_GROUPS = 8
SSD_STATE = 128
SSD_CONV = 4
FFN_CONV = 3

LANES = 128
SUBLANES = 8
VMEM_LIMIT_BYTES = 56 << 20
VMEM_TILE_BUDGET = 44 << 20

FFN_ROW_TILE = 1024
FFN_COL_TILE = 512
MLSTM_CHUNK = 128
SSD_CHUNK = 128


def _cparams(n_axes):
    return pltpu.CompilerParams(dimension_semantics=("arbitrary",) * n_axes,
                                vmem_limit_bytes=VMEM_LIMIT_BYTES)


def _round_up(a, b):
    return (a + b - 1) // b * b


def _largest_divisor(n, candidates):
    for c in candidates:
        if n % c == 0:
            return c
    raise ValueError(f"no tile in {candidates} divides {n}")


def _rmsnorm(x, g):
    return x * lax.rsqrt(jnp.mean(x * x, axis=-1, keepdims=True) + RMS_EPS) * g


def _silu(x):
    return x * jax.nn.sigmoid(x)


def _mm_body(*refs, norm, residual, small, emit):
    it = iter(refs)
    x_ref = next(it)
    g_ref = next(it) if norm else None
    w_ref = next(it)
    w2_ref = next(it) if small else None
    r_ref = next(it) if residual else None
    o_ref = next(it)
    o2_ref = next(it) if small else None
    wb_ref = next(it) if emit else None
    xn_ref = next(it) if norm else None
    first = pl.program_id(1) == 0
    if norm:
        @pl.when(first)
        def _():
            xn_ref[...] = _rmsnorm(x_ref[...], g_ref[...]).astype(BF16)
        lhs = xn_ref[...]
    else:
        lhs = x_ref[...].astype(BF16)
    if small:
        @pl.when(first)
        def _():
            o2_ref[...] = jnp.dot(lhs, w2_ref[...].astype(BF16), preferred_element_type=F32)
    w = w_ref[...]
    if emit:
        w = w.astype(BF16)
        wb_ref[...] = w
    acc = jnp.dot(lhs, w, preferred_element_type=F32)
    if residual:
        acc = acc + r_ref[...]
    o_ref[...] = acc.astype(o_ref.dtype)


def _mm_tiles(m, k, n, ns, x_bytes, w_bytes, norm, residual, emit):
    def need(tm, tn, w_bufs):
        b = 2 * tm * k * x_bytes + w_bufs * k * tn * w_bytes + 2 * tm * tn * 4
        b += tm * k * 2 if norm else 0
        b += 2 * tm * tn * 4 if residual else 0
        b += 3 * k * tn * 2 if emit else 0
        return b + 2 * k * ns * 4 + 2 * tm * ns * 4

    row_tiles = [tm for tm in (1024, 512, 256, 128, 64, 32, 16, 8) if m % tm == 0]
    for tm in row_tiles[:2]:
        if need(tm, n, 1) <= VMEM_TILE_BUDGET:
            return tm, n, 1
    for tm in row_tiles:
        for tn in (2048, 1280, 1024, 896, 512, 256, 128):
            if n % tn == 0 and need(tm, tn, 2) <= VMEM_TILE_BUDGET:
                return tm, tn, 2
    raise ValueError("no matmul tile fits VMEM")


def _w_spec(w, layer, block, index, buffers):
    mode = pl.Buffered(buffers)
    if layer is None:
        return pl.BlockSpec(block, index, pipeline_mode=mode)
    return pl.BlockSpec((None,) + block, lambda *g: (layer,) + index(*g), pipeline_mode=mode)


def _matmul(x, w, layer, n, gain=None, res=None, small=None, emit=False):
    m, k = x.shape
    norm, residual, has_small = gain is not None, res is not None, small is not None
    ns = small.shape[1] if has_small else 0
    tm, tn, w_bufs = _mm_tiles(m, k, n, ns, x.dtype.itemsize, w.dtype.itemsize, norm, residual, emit)
    assert not emit or tm == m
    in_specs = [pl.BlockSpec((tm, k), lambda i, j: (i, 0))]
    args = [x]
    if norm:
        in_specs.append(pl.BlockSpec((1, k), lambda i, j: (0, 0)))
        args.append(gain.reshape(1, k))
    in_specs.append(_w_spec(w, layer, (k, tn), lambda i, j: (0, j), w_bufs))
    args.append(w)
    if has_small:
        in_specs.append(pl.BlockSpec((k, ns), lambda i, j: (0, 0)))
        args.append(small)
    if residual:
        in_specs.append(pl.BlockSpec((tm, tn), lambda i, j: (i, j)))
        args.append(res)
    out_specs = [pl.BlockSpec((tm, tn), lambda i, j: (i, j))]
    out_shape = [jax.ShapeDtypeStruct((m, n), F32)]
    if has_small:
        out_specs.append(pl.BlockSpec((tm, ns), lambda i, j: (i, 0)))
        out_shape.append(jax.ShapeDtypeStruct((m, ns), F32))
    if emit:
        out_specs.append(pl.BlockSpec((k, tn), lambda i, j: (0, j)))
        out_shape.append(jax.ShapeDtypeStruct((k, n), BF16))
    outs = pl.pallas_call(
        functools.partial(_mm_body, norm=norm, residual=residual, small=has_small, emit=emit),
        grid=(m // tm, n // tn),
        in_specs=in_specs,
        out_specs=out_specs,
        out_shape=out_shape,
        scratch_shapes=[pltpu.VMEM((tm, k), BF16)] if norm else [],
        compiler_params=_cparams(2),
        name="matmul",
    )(*args)
    return outs if len(outs) > 1 else outs[0]


def _ffn_body(*refs, tm, shift, halo, tiles_per_seq, nj, final_norm, emit):
    x_ref, gn_ref, wu_ref, wg_ref, cw_ref, cb_ref, wo_ref, buf0_ref, gf_ref, o_ref, bufo_ref = refs[:11]
    wub_ref, wgb_ref, wob_ref = refs[11:14] if emit else (None, None, None)
    xn_ref, gs_ref, carry_ref = refs[-3:]
    i = pl.program_id(0)
    j = pl.program_id(1)

    @pl.when(j == 0)
    def _():
        x = x_ref[...]
        xn_ref[...] = _rmsnorm(x, gn_ref[...]).astype(BF16)
        o_ref[...] = x

    first = (i % tiles_per_seq) == 0
    gs_ref[0:halo, :] = jnp.where(first, buf0_ref[0], carry_ref[j])
    base = halo - (FFN_CONV - 1) * shift
    xn = xn_ref[...]
    wu, wg = wu_ref[...], wg_ref[...]
    if emit:
        wu, wg = wu.astype(BF16), wg.astype(BF16)
        wub_ref[...] = wu
        wgb_ref[...] = wg
    gs_ref[halo:halo + tm, :] = jnp.dot(xn, wg, preferred_element_type=F32)
    u = jnp.dot(xn, wu, preferred_element_type=F32)
    c = gs_ref[base:base + tm, :] * cw_ref[0:1, :]
    for t in range(1, FFN_CONV):
        c = c + gs_ref[base + t * shift:base + t * shift + tm, :] * cw_ref[t:t + 1, :]
    c = c + cb_ref[...]
    h = (_silu(c) * u).astype(BF16)
    tail = gs_ref[tm:tm + halo, :]
    carry_ref[j] = tail
    bufo_ref[0] = tail
    wo = wo_ref[...]
    if emit:
        wo = wo.astype(BF16)
        wob_ref[...] = wo
    o_ref[...] += jnp.dot(h, wo, preferred_element_type=F32)

    if final_norm:
        @pl.when(j == nj - 1)
        def _():
            o_ref[...] = _rmsnorm(o_ref[...], gf_ref[...])


def _ffn(x, gain, weights, conv_w, conv_b, buf0, shift, rows_per_seq, final_gain):
    m, d = x.shape
    emit = weights[0].ndim == 3
    d_ff = weights[1].shape[1] if emit else weights[2].shape[0]
    tm = _largest_divisor(rows_per_seq, (FFN_ROW_TILE, 512, 256, 128, 64, 32, 16, 8))
    tf = _largest_divisor(d_ff, (FFN_COL_TILE, 256, 128))
    nj = d_ff // tf
    if emit:
        w_in, w_out, layer = weights
        assert m == tm
        w_args = [w_in, w_in, w_out]
        w_specs = [pl.BlockSpec((None, d, tf), lambda i, j: (layer, 0, j)),
                   pl.BlockSpec((None, d, tf), lambda i, j: (layer, 0, nj + j)),
                   pl.BlockSpec((None, tf, d), lambda i, j: (layer, j, 0))]
    else:
        w_args = list(weights)
        w_specs = [pl.BlockSpec((d, tf), lambda i, j: (0, j)),
                   pl.BlockSpec((d, tf), lambda i, j: (0, j)),
                   pl.BlockSpec((tf, d), lambda i, j: (j, 0))]
    tiles_per_seq = rows_per_seq // tm
    pre = (FFN_CONV - 1) * shift
    halo = _round_up(pre, SUBLANES)
    assert tm >= halo and tm % SUBLANES == 0
    buf0p = jnp.pad(buf0, ((0, 0), (halo - pre, 0), (0, 0)))
    final_norm = final_gain is not None
    gf = (final_gain if final_norm else gain).reshape(1, d)
    body = functools.partial(_ffn_body, tm=tm, shift=shift, halo=halo, tiles_per_seq=tiles_per_seq,
                             nj=nj, final_norm=final_norm, emit=emit)
    out_specs = [pl.BlockSpec((tm, d), lambda i, j: (i, 0)),
                 pl.BlockSpec((1, halo, tf), lambda i, j: (i, 0, j))]
    out_shape = [jax.ShapeDtypeStruct((m, d), F32),
                 jax.ShapeDtypeStruct((m // tm, halo, d_ff), F32)]
    if emit:
        out_specs += [pl.BlockSpec((d, tf), lambda i, j: (0, j)),
                      pl.BlockSpec((d, tf), lambda i, j: (0, j)),
                      pl.BlockSpec((tf, d), lambda i, j: (j, 0))]
        out_shape += [jax.ShapeDtypeStruct((d, d_ff), BF16), jax.ShapeDtypeStruct((d, d_ff), BF16),
                      jax.ShapeDtypeStruct((d_ff, d), BF16)]
    outs = pl.pallas_call(
        body,
        grid=(m // tm, nj),
        in_specs=[
            pl.BlockSpec((tm, d), lambda i, j: (i, 0)),
            pl.BlockSpec((1, d), lambda i, j: (0, 0)),
            w_specs[0],
            w_specs[1],
            pl.BlockSpec((FFN_CONV, tf), lambda i, j: (0, j)),
            pl.BlockSpec((1, tf), lambda i, j: (0, j)),
            w_specs[2],
            pl.BlockSpec((1, halo, tf), lambda i, j: (i // tiles_per_seq, 0, j)),
            pl.BlockSpec((1, d), lambda i, j: (0, 0)),
        ],
        out_specs=out_specs,
        out_shape=out_shape,
        scratch_shapes=[pltpu.VMEM((tm, d), BF16),
                        pltpu.VMEM((halo + tm, tf), F32),
                        pltpu.VMEM((nj, halo, tf), F32)],
        compiler_params=_cparams(2),
        name="conv_ffn",
    )(x, gain.reshape(1, d), w_args[0], w_args[1], conv_w, conv_b.reshape(1, d_ff), w_args[2], buf0p, gf)
    result = (outs[0], outs[1][tiles_per_seq - 1::tiles_per_seq, halo - pre:, :])
    return result + ((tuple(outs[2:]),) if emit else ())


def _row_from_col(col, eye):
    return jnp.sum(jnp.where(eye, col, 0.0), axis=0, keepdims=True)


def _rowsum_lanes(x):
    parts = jnp.concatenate(_split3_bf16(x), axis=1)
    return jnp.dot(parts, jnp.ones((parts.shape[1], LANES), BF16), preferred_element_type=F32)


def _cumsum_rows(x, tril):
    return jnp.dot(tril, x, preferred_element_type=F32, precision=lax.Precision.HIGHEST)


def _mlstm_body(*refs, chunk, dk, dv, has_init):
    q_ref, k_ref, v_ref, o_ref, gate_ref, bias_ref, gh_ref = refs[:7]
    c0_ref, n0_ref, m0_ref = refs[7:10] if has_init else (None, None, None)
    hg_ref, c_ref, n_ref, m_ref = refs[-4:]
    L = chunk
    H = MLSTM_HEADS

    @pl.when(pl.program_id(1) == 0)
    def _():
        if has_init:
            c_ref[...] = c0_ref[...]
            n_ref[...] = n0_ref[...]
            m_ref[...] = m0_ref[...]
        else:
            c_ref[...] = jnp.zeros_like(c_ref)
            n_ref[...] = jnp.zeros_like(n_ref)
            m_ref[...] = jnp.zeros_like(m_ref)

    ri = lax.broadcasted_iota(jnp.int32, (L, L), 0)
    ci = lax.broadcasted_iota(jnp.int32, (L, L), 1)
    causal = ri >= ci
    eye = ri == ci
    tril = causal.astype(F32)

    gates = gate_ref[...] + bias_ref[...]
    x = gates
    log_sig = jnp.minimum(x, 0.0) - jnp.log1p(jnp.exp(-jnp.abs(x)))
    bcum = _cumsum_rows(log_sig, tril)

    n_all = n_ref[0]
    m_all = m_ref[0]
    stack = lambda parts: jnp.concatenate(parts, axis=0)
    if L % LANES == 0:
        gates_t, bcum_t = gates.T, bcum.T
        row_of = lambda arr_t, col, r: arr_t[r:r + 1, :]
    else:
        gates_t = bcum_t = None
        row_of = lambda arr_t, col, r: _row_from_col(col, eye)
    ig_cols = [gates[:, h:h + 1] for h in range(H)]
    b_cols = [bcum[:, H + h:H + h + 1] for h in range(H)]
    m_prevs = [m_all[h:h + 1, 0:1] for h in range(H)]
    ig_col, b_col = stack(ig_cols), stack(b_cols)
    m_prev = stack([jnp.broadcast_to(mp, (L, 1)) for mp in m_prevs])
    ig_row = stack([jnp.broadcast_to(row_of(gates_t, ig_cols[h], h), (L, L)) for h in range(H)])
    b_row = stack([jnp.broadcast_to(row_of(bcum_t, b_cols[h], H + h), (L, L)) for h in range(H)])
    causal_all = stack([causal] * H)

    log_d = jnp.where(causal_all, b_col - b_row + ig_row, -jnp.inf)
    log_inter = b_col + m_prev
    m_t = jnp.maximum(log_inter, jnp.max(log_d, axis=-1, keepdims=True))
    d = jnp.exp(log_d - m_t)
    inter = jnp.exp(log_inter - m_t)

    qk, qc, qn, kfs, vbs, c_prevs = [], [], [], [], [], []
    for h in range(H):
        qf = q_ref[:, h * dk:(h + 1) * dk] * (dk ** -0.5)
        kf = k_ref[:, h * dk:(h + 1) * dk]
        qb = qf.astype(BF16)
        c_prev = c_ref[0, h]
        qk.append(lax.dot_general(qb, kf.astype(BF16), (((1,), (1,)), ((), ())), preferred_element_type=F32))
        qc.append(jnp.dot(qb, c_prev.astype(BF16), preferred_element_type=F32))
        qn.append(qf * n_all[h:h + 1, :])
        kfs.append(kf)
        vbs.append(v_ref[:, h * dv:(h + 1) * dv].astype(BF16))
        c_prevs.append(c_prev)

    s = stack(qk) * d
    sb = s.astype(BF16)
    sv = stack([jnp.dot(sb[h * L:(h + 1) * L], vbs[h], preferred_element_type=F32) for h in range(H)])
    num = inter * stack(qc) + sv
    if L % LANES == 0:
        rep = lambda col: jnp.broadcast_to(col, (H * L, LANES))
        wide = lambda a: jnp.concatenate([a] * (dv // LANES), axis=1)
        rowsum = _rowsum_lanes
    else:
        rep = wide = lambda a: a
        rowsum = lambda a: jnp.sum(a, axis=-1, keepdims=True)
    den = rep(inter) * rowsum(stack(qn)) + rowsum(s)
    hh = num / wide(jnp.maximum(jnp.abs(den), rep(jnp.exp(-m_t))))
    hn = hh * wide(lax.rsqrt(rowsum(hh * hh) * (1.0 / dv) + RMS_EPS))
    for h in range(H):
        lanes = slice(h * dv, (h + 1) * dv)
        out = jax.nn.sigmoid(o_ref[:, lanes]) * (hn[h * L:(h + 1) * L] * gh_ref[:, lanes])
        hg_ref[:, lanes] = out.astype(hg_ref.dtype)

    b_lasts = [bc[L - 1:L, :] for bc in b_cols]
    log_w = stack([jnp.broadcast_to(bl, (L, 1)) for bl in b_lasts]) - b_col + ig_col
    m_news = [jnp.maximum(b_lasts[h] + m_prevs[h], jnp.max(log_w[h * L:(h + 1) * L], axis=0, keepdims=True))
              for h in range(H)]
    w = jnp.exp(log_w - stack([jnp.broadcast_to(mn, (L, 1)) for mn in m_news]))

    head_row = lax.broadcasted_iota(jnp.int32, (H, LANES), 0)
    n_new, m_new_all = n_all, m_all
    for h in range(H):
        decay = jnp.exp(b_lasts[h] + m_prevs[h] - m_news[h])
        kw = kfs[h] * w[h * L:(h + 1) * L]
        upd = lax.dot_general(kw.astype(BF16), vbs[h], (((0,), (0,)), ((), ())), preferred_element_type=F32)
        c_ref[0, h] = decay * c_prevs[h] + upd
        n_row = decay * n_all[h:h + 1, :] + jnp.sum(kw, axis=0, keepdims=True)
        n_new = jnp.where(head_row == h, n_row, n_new)
        m_new_all = jnp.where(head_row == h, m_news[h], m_new_all)
    n_ref[0] = n_new
    m_ref[0] = m_new_all


def _mlstm_mix(proj, gates, bias, g_head, init, layer, B, T, dk, dv):
    H = MLSTM_HEADS
    L = math.gcd(T, MLSTM_CHUNK)
    nc = T // L
    qw, vw = H * dk, H * dv
    assert qw * 2 == vw and vw % LANES == 0
    hg_dtype = BF16 if L % 16 == 0 else F32
    row = lambda b, c: b * nc + c
    c_spec = pl.BlockSpec((None, 1, H, dk, dv), lambda b, c: (layer, b, 0, 0, 0))
    in_specs = [
        pl.BlockSpec((L, qw), lambda b, c: (row(b, c), 0)),
        pl.BlockSpec((L, qw), lambda b, c: (row(b, c), 1)),
        pl.BlockSpec((L, vw), lambda b, c: (row(b, c), 1)),
        pl.BlockSpec((L, vw), lambda b, c: (row(b, c), 2)),
        pl.BlockSpec((L, LANES), lambda b, c: (row(b, c), 0)),
        pl.BlockSpec((1, LANES), lambda b, c: (0, 0)),
        pl.BlockSpec((1, vw), lambda b, c: (0, 0)),
    ]
    args = [proj, proj, proj, proj, gates, bias, g_head.reshape(1, vw)]
    if init is not None:
        c_all, n0, m0 = init
        in_specs += [c_spec,
                     pl.BlockSpec((1, H, dk), lambda b, c: (b, 0, 0)),
                     pl.BlockSpec((1, H, LANES), lambda b, c: (b, 0, 0))]
        args += [c_all, n0, jnp.broadcast_to(m0[:, :, None], (B, H, LANES))]
    hg, c, n, m = pl.pallas_call(
        functools.partial(_mlstm_body, chunk=L, dk=dk, dv=dv, has_init=init is not None),
        grid=(B, nc),
        in_specs=in_specs,
        out_specs=[
            pl.BlockSpec((L, vw), lambda b, c: (row(b, c), 0)),
            pl.BlockSpec((1, H, dk, dv), lambda b, c: (b, 0, 0, 0)),
            pl.BlockSpec((1, H, dk), lambda b, c: (b, 0, 0)),
            pl.BlockSpec((1, H, LANES), lambda b, c: (b, 0, 0)),
        ],
        out_shape=[jax.ShapeDtypeStruct((B * T, vw), hg_dtype),
                   jax.ShapeDtypeStruct((B, H, dk, dv), F32),
                   jax.ShapeDtypeStruct((B, H, dk), F32),
                   jax.ShapeDtypeStruct((B, H, LANES), F32)],
        compiler_params=_cparams(2),
        name="mlstm",
    )(*args)
    return hg, c, n, m[:, :, 0]


def _swa_body(sink_ref, q_ref, kp_ref, kc_ref, vp_ref, vc_ref, o_ref, *, tq, hd, mask_first):
    W = WINDOW
    G = SWA_GROUP
    S = 2 * W
    kk = jnp.concatenate([kp_ref[...], kc_ref[...]], axis=0)
    vv = jnp.concatenate([vp_ref[...], vc_ref[...]], axis=0)
    qi = lax.broadcasted_iota(jnp.int32, (tq, S), 0)
    kj = lax.broadcasted_iota(jnp.int32, (tq, S), 1)
    dist = W + qi - kj
    valid = (dist >= 0) & (dist <= W)
    if mask_first:
        valid = valid & ((kj >= W) | (pl.program_id(1) > 0))
    distf = dist.astype(F32)
    lane_s = lax.broadcasted_iota(jnp.int32, (S, LANES), 1)
    lane_q = lax.broadcasted_iota(jnp.int32, (tq, LANES), 1)
    low_s = lane_s < hd
    low_q = lane_q < hd
    heads_per_slab = LANES // hd
    assert heads_per_slab == 2

    for kh in range(SWA_KV_HEADS):
        slab = (kh // heads_per_slab) * LANES
        keep_low = (kh % heads_per_slab) == 0
        k128 = kk[:, slab:slab + LANES]
        v128 = vv[:, slab:slab + LANES]
        k_rot = pltpu.roll(k128, hd, axis=1)
        v_rot = pltpu.roll(v128, hd, axis=1)
        own = low_s if keep_low else jnp.logical_not(low_s)
        k2 = jnp.where(own, k128, k_rot).astype(BF16)
        v2 = jnp.where(own, v128, v_rot).astype(BF16)

        q_parts = []
        for g in range(G):
            h = kh * G + g
            qs = q_ref[:, (h // 2) * LANES:(h // 2 + 1) * LANES]
            keep = low_q if h % 2 == 0 else jnp.logical_not(low_q)
            q_parts.append(jnp.where(keep, qs, 0.0))
        q_stack = jnp.concatenate(q_parts, axis=0).astype(BF16)
        s_all = lax.dot_general(q_stack, k2, (((1,), (1,)), ((), ())), preferred_element_type=F32)
        s_all = s_all * (hd ** -0.5)

        p_parts = []
        for g in range(G):
            h = kh * G + g
            slope = 2.0 ** (-8.0 * (h + 1) / SWA_HEADS)
            sink = sink_ref[h]
            s = s_all[g * tq:(g + 1) * tq, :] - slope * distf
            s = jnp.where(valid, s, -jnp.inf)
            mx = jnp.maximum(jnp.max(s, axis=-1, keepdims=True), sink)
            p = jnp.exp(s - mx)
            p = p / (jnp.sum(p, axis=-1, keepdims=True) + jnp.exp(sink - mx))
            p_parts.append(p)
        p_stack = jnp.concatenate(p_parts, axis=0).astype(BF16)
        o_all = jnp.dot(p_stack, v2, preferred_element_type=F32)

        for g2 in range(G // 2):
            h = kh * G + 2 * g2
            o_even = o_all[(2 * g2) * tq:(2 * g2 + 1) * tq, :]
            o_odd = o_all[(2 * g2 + 1) * tq:(2 * g2 + 2) * tq, :]
            o_ref[:, (h // 2) * LANES:(h // 2 + 1) * LANES] = jnp.where(low_q, o_even, o_odd).astype(o_ref.dtype)


def _swa_call(sinks, q_arr, q_blk, k_arr, kp_map, kc_map, v_arr, vp_map, vc_map, grid, tq, n_rows, hd,
              mask_first):
    kvw = SWA_KV_HEADS * hd
    qw = SWA_HEADS * hd
    out_dtype = BF16 if tq % 16 == 0 else F32
    return pl.pallas_call(
        functools.partial(_swa_body, tq=tq, hd=hd, mask_first=mask_first),
        grid=grid,
        in_specs=[
            pl.BlockSpec(memory_space=pltpu.SMEM),
            pl.BlockSpec((tq, qw), q_blk),
            pl.BlockSpec((WINDOW, kvw), kp_map),
            pl.BlockSpec((WINDOW, kvw), kc_map),
            pl.BlockSpec((WINDOW, kvw), vp_map),
            pl.BlockSpec((WINDOW, kvw), vc_map),
        ],
        out_specs=pl.BlockSpec((tq, qw), q_blk),
        out_shape=jax.ShapeDtypeStruct((n_rows, qw), out_dtype),
        compiler_params=_cparams(2),
        name="swa",
    )(sinks, q_arr, k_arr, k_arr, v_arr, v_arr)


def _conv_silu(u_ref, init_ref, carry_ref, gs_ref, cw_ref, cb_ref, first, L):
    halo = SUBLANES
    gs_ref[0:halo, :] = jnp.where(first, init_ref[0], carry_ref[...])
    gs_ref[halo:halo + L, :] = u_ref[...]
    carry_ref[...] = gs_ref[L:L + halo, :]
    base = halo - (SSD_CONV - 1)
    cw = cw_ref[...]
    c = gs_ref[base:base + L, :] * cw[0:1, :]
    for t in range(1, SSD_CONV):
        c = c + gs_ref[base + t:base + t + L, :] * cw[t:t + 1, :]
    c = c + cb_ref[...]
    return _silu(c)


def _split3_bf16(x):
    hi = x.astype(BF16)
    r = x - hi.astype(F32)
    mid = r.astype(BF16)
    lo = (r - mid.astype(F32)).astype(BF16)
    return hi, mid, lo


def _ssd_body(z_ref, xs_ref, bc_ref, dt_ref, dtb_ref, alog_ref, ex_ref, dsk_ref, gn_ref,
              cwx_ref, cwbc_ref, cbx_ref, cbbc_ref, ix_ref, ibc_ref, s0_ref,
              y_ref, s_ref,
              gx_ref, gbc_ref, kx_ref, kbc_ref, *, chunk, groups, hpg, p_dim):
    L = chunk
    N = SSD_STATE
    gw = hpg * p_dim
    first = pl.program_id(1) == 0

    @pl.when(first)
    def _():
        s_ref[...] = s0_ref[...]

    xs_all = _conv_silu(xs_ref, ix_ref, kx_ref, gx_ref, cwx_ref, cbx_ref, first, L)
    bc_all = _conv_silu(bc_ref, ibc_ref, kbc_ref, gbc_ref, cwbc_ref, cbbc_ref, first, L)

    ri = lax.broadcasted_iota(jnp.int32, (L, L), 0)
    ci = lax.broadcasted_iota(jnp.int32, (L, L), 1)
    causal = ri >= ci
    eye = ri == ci
    tril = causal.astype(F32)

    dt = jax.nn.softplus(dt_ref[...] + dtb_ref[...])
    acum = _cumsum_rows(dt * (-jnp.exp(alog_ref[...])), tril)
    a_last = acum[L - 1:L, :]
    eacum = jnp.exp(acum)
    dtw = dt * jnp.exp(a_last - acum)
    ea_last = jnp.exp(a_last)
    acum_t = acum.T if L % LANES == 0 else None

    per_head = jnp.concatenate([dt, eacum, dtw], axis=0)
    ex = jnp.dot(jnp.concatenate(_split3_bf16(per_head), axis=1), ex_ref[...],
                 preferred_element_type=F32)

    lane = lax.broadcasted_iota(jnp.int32, (L, LANES), 1)
    low = lane < p_dim
    srow = lax.broadcasted_iota(jnp.int32, (LANES, 1), 0)
    heads_per_slab = LANES // p_dim
    assert heads_per_slab == 2

    gated = []
    for g in range(groups):
        lanes_g = slice(g * gw, (g + 1) * gw)
        dt_e, ea_e, dtw_e = ex[0:L, lanes_g], ex[L:2 * L, lanes_g], ex[2 * L:3 * L, lanes_g]

        xs = xs_all[:, g * gw:(g + 1) * gw]
        bmb = bc_all[:, g * N:(g + 1) * N].astype(BF16)
        cmb = bc_all[:, (groups + g) * N:(groups + g + 1) * N].astype(BF16)
        cb = lax.dot_general(cmb, bmb, (((1,), (1,)), ((), ())), preferred_element_type=F32)
        xdt = (xs * dt_e).astype(BF16)
        xw = (xs * dtw_e).astype(BF16)
        s_prev = s_ref[0, g * gw:(g + 1) * gw, :]
        y_inter = lax.dot_general(cmb, s_prev.astype(BF16), (((1,), (1,)), ((), ())),
                                  preferred_element_type=F32)
        upd = lax.dot_general(xw, bmb, (((0,), (0,)), ((), ())), preferred_element_type=F32)

        y_parts = []
        for pr in range(hpg // heads_per_slab):
            h0 = g * hpg + 2 * pr
            ms = []
            for hh in (h0, h0 + 1):
                a_col = acum[:, hh:hh + 1]
                a_row = acum_t[hh:hh + 1, :] if acum_t is not None else _row_from_col(a_col, eye)
                dec = jnp.exp(jnp.where(causal, a_col - a_row, -jnp.inf))
                ms.append((cb * dec).astype(BF16))
            mcat = jnp.concatenate(ms, axis=1)
            xb = xdt[:, pr * LANES:(pr + 1) * LANES]
            zero = jnp.zeros_like(xb)
            bd = jnp.concatenate([jnp.where(low, xb, zero), jnp.where(low, zero, xb)], axis=0)
            y_parts.append(jnp.dot(mcat, bd, preferred_element_type=F32))

            scale = jnp.where(srow < p_dim, ea_last[:, h0:h0 + 1], ea_last[:, h0 + 1:h0 + 2])
            rows = slice(g * gw + pr * LANES, g * gw + (pr + 1) * LANES)
            s_ref[0, rows, :] = s_prev[pr * LANES:(pr + 1) * LANES] * scale + upd[pr * LANES:(pr + 1) * LANES]

        y = jnp.concatenate(y_parts, axis=1) + y_inter * ea_e
        y = y + dsk_ref[:, g * gw:(g + 1) * gw] * xs
        gated.append(y * _silu(z_ref[:, g * gw:(g + 1) * gw]))

    sq = gated[0] * gated[0]
    for y in gated[1:]:
        sq = sq + y * y
    inv = lax.rsqrt(jnp.sum(sq, axis=-1, keepdims=True) * (1.0 / (groups * gw)) + RMS_EPS)
    for g, y in enumerate(gated):
        y_ref[:, g * gw:(g + 1) * gw] = (y * inv * gn_ref[:, g * gw:(g + 1) * gw]).astype(y_ref.dtype)


def _ssd_mix(proj, dt, d_inner, conv_w, conv_b, dt_bias, a_log, d_skip, g_norm, s0, conv0, B, T):
    G, P, N = SSD_GROUPS, SSD_HEAD_DIM, SSD_STATE
    heads = d_inner // P
    hpg = heads // G
    L = math.gcd(T, SSD_CHUNK)
    nc = T // L
    bcw = 2 * G * N
    dtw = LANES
    conv_dim = d_inner + bcw
    assert d_inner % bcw == 0 and heads <= LANES
    bc_blk = (2 * d_inner) // bcw
    pad_heads = lambda v: jnp.pad(v, (0, LANES - heads)).reshape(1, LANES)
    expand = (jnp.arange(3 * LANES)[:, None] % LANES == jnp.arange(d_inner)[None, :] // P).astype(BF16)
    halo = SUBLANES
    conv0p = jnp.pad(conv0, ((0, 0), (halo - (SSD_CONV - 1), 0), (0, 0)))
    conv_b2 = conv_b.reshape(1, conv_dim)
    dsk = jnp.repeat(d_skip, P).reshape(1, d_inner)
    row = lambda b, c: b * nc + c
    const = lambda b, c: (0, 0)
    y, s = pl.pallas_call(
        functools.partial(_ssd_body, chunk=L, groups=G, hpg=hpg, p_dim=P),
        grid=(B, nc),
        in_specs=[
            pl.BlockSpec((L, d_inner), lambda b, c: (row(b, c), 0)),
            pl.BlockSpec((L, d_inner), lambda b, c: (row(b, c), 1)),
            pl.BlockSpec((L, bcw), lambda b, c: (row(b, c), bc_blk)),
            pl.BlockSpec((L, dtw), lambda b, c: (row(b, c), 0)),
            pl.BlockSpec((1, dtw), const),
            pl.BlockSpec((1, dtw), const),
            pl.BlockSpec((3 * LANES, d_inner), const),
            pl.BlockSpec((1, d_inner), const),
            pl.BlockSpec((1, d_inner), const),
            pl.BlockSpec((SSD_CONV, d_inner), const),
            pl.BlockSpec((SSD_CONV, bcw), lambda b, c: (0, d_inner // bcw)),
            pl.BlockSpec((1, d_inner), const),
            pl.BlockSpec((1, bcw), lambda b, c: (0, d_inner // bcw)),
            pl.BlockSpec((1, halo, d_inner), lambda b, c: (b, 0, 0)),
            pl.BlockSpec((1, halo, bcw), lambda b, c: (b, 0, d_inner // bcw)),
            pl.BlockSpec((1, d_inner, N), lambda b, c: (b, 0, 0)),
        ],
        out_specs=[
            pl.BlockSpec((L, d_inner), lambda b, c: (row(b, c), 0)),
            pl.BlockSpec((1, d_inner, N), lambda b, c: (b, 0, 0)),
        ],
        out_shape=[jax.ShapeDtypeStruct((B * T, d_inner), BF16 if L % 16 == 0 else F32),
                   jax.ShapeDtypeStruct((B, d_inner, N), F32)],
        scratch_shapes=[pltpu.VMEM((halo + L, d_inner), F32), pltpu.VMEM((halo + L, bcw), F32),
                        pltpu.VMEM((halo, d_inner), F32), pltpu.VMEM((halo, bcw), F32)],
        compiler_params=_cparams(2),
        name="ssd",
    )(proj, proj, proj, dt, pad_heads(dt_bias), pad_heads(a_log), expand, dsk, g_norm.reshape(1, d_inner),
      conv_w, conv_w, conv_b2, conv_b2, conv0p, conv0p, s0.reshape(B, d_inner, N))
    return y, s.reshape(B, heads, P, N)


def _gate_weight(w, j):
    gates = 2 * MLSTM_HEADS
    return jnp.pad(w[j, :, -gates:], ((0, 0), (0, LANES - gates)))


def _dt_weight(w, j, d_inner):
    heads = d_inner // SSD_HEAD_DIM
    return jnp.pad(w[j, :, -heads:], ((0, 0), (0, LANES - heads)))


def _main_columns_bf16(w, n):
    return [w[j, :, :n].astype(BF16) for j in range(w.shape[0])]


def _trunk(x3, st, W, E):
    B, T, D = x3.shape
    fresh = st is None

    def mm(name, j, n, xin, **kw):
        if (name, j) in E:
            return _matmul(xin, E[name, j], None, n, **kw)
        outs = _matmul(xin, W[name], j, n, emit=True, **kw)
        E[name, j] = outs[-1]
        return outs[0] if len(outs) == 2 else outs[:-1]

    x = x3.reshape(B * T, D)
    depth = W["norm_mix"].shape[0]
    o_c, o_n, o_m, o_k, o_v, o_s, o_cv, o_f = [], [], [], [], [], [], [], []
    for i in range(depth):
        j = i // N_MIXERS
        kind = i % N_MIXERS
        if kind == 0:
            dv = D // MLSTM_HEADS
            dk = dv // 2
            init = None if fresh else (st["c"], st["n"][j], st["m"][j])
            proj, gates = mm("w_in_a", j, 2 * MLSTM_HEADS * (dk + dv), x, gain=W["norm_mix"][i],
                             small=_gate_weight(W["w_in_a"], j))
            bias = jnp.pad(jnp.concatenate([W["b_ig_a"][j], W["b_fg_a"][j]]), (0, LANES - 2 * MLSTM_HEADS))
            hg, c, n, m = _mlstm_mix(proj, gates, bias.reshape(1, LANES), W["g_head_a"][j], init, j,
                                     B, T, dk, dv)
            x = mm("w_out_a", j, D, hg, res=x)
            o_c.append(c); o_n.append(n); o_m.append(m)
        elif kind == 1:
            hd = D // SWA_HEADS
            kvw = SWA_KV_HEADS * hd
            qw = SWA_HEADS * hd
            proj = mm("w_in_b", j, qw + 2 * kvw, x, gain=W["norm_mix"][i])
            p3 = proj.reshape(B, T, qw + 2 * kvw)
            k_new, v_new = p3[:, :, qw:qw + kvw], p3[:, :, qw + kvw:]
            kb, vb = qw // kvw, qw // kvw + 1
            if fresh:
                nb = T // WINDOW
                prev = lambda b, n: (b * nb + jnp.maximum(n - 1, 0), kb)
                cur = lambda b, n: (b * nb + n, kb)
                prev_v = lambda b, n: (b * nb + jnp.maximum(n - 1, 0), vb)
                cur_v = lambda b, n: (b * nb + n, vb)
                o = _swa_call(W["sinks_b"][j], proj, lambda b, n: (b * nb + n, 0), proj, prev, cur,
                              proj, prev_v, cur_v, (B, nb), WINDOW, B * T, hd, True)
                kc, vc = k_new[:, -WINDOW:], v_new[:, -WINDOW:]
            else:
                kbuf = st["k"][j].reshape(B, WINDOW, kvw)
                vbuf = st["v"][j].reshape(B, WINDOW, kvw)
                zpad = jnp.zeros((B, WINDOW - T, kvw), F32)
                kk = jnp.concatenate([kbuf, k_new, zpad], axis=1).reshape(B * 2 * WINDOW, kvw)
                vv = jnp.concatenate([vbuf, v_new, zpad], axis=1).reshape(B * 2 * WINDOW, kvw)
                o = _swa_call(W["sinks_b"][j], proj, lambda b, n: (b, 0), kk, lambda b, n: (2 * b, 0),
                              lambda b, n: (2 * b + 1, 0), vv, lambda b, n: (2 * b, 0),
                              lambda b, n: (2 * b + 1, 0), (B, 1), T, B * T, hd, False)
                kc = jnp.concatenate([kbuf[:, T:], k_new], axis=1)
                vc = jnp.concatenate([vbuf[:, T:], v_new], axis=1)
            x = mm("w_out_b", j, D, o, res=x)
            o_k.append(kc.reshape(B, WINDOW, SWA_KV_HEADS, hd))
            o_v.append(vc.reshape(B, WINDOW, SWA_KV_HEADS, hd))
        else:
            d_inner = W["w_out_c"].shape[1]
            heads = d_inner // SSD_HEAD_DIM
            conv_dim = d_inner + 2 * SSD_GROUPS * SSD_STATE
            if fresh:
                s0 = jnp.zeros((B, heads, SSD_HEAD_DIM, SSD_STATE), F32)
                cb0 = jnp.zeros((B, SSD_CONV - 1, conv_dim), F32)
            else:
                s0, cb0 = st["ssm"][j], st["conv"][j]
            proj, dt = mm("w_in_c", j, d_inner + conv_dim, x, gain=W["norm_mix"][i],
                          small=_dt_weight(W["w_in_c"], j, d_inner))
            y, s = _ssd_mix(proj, dt, d_inner, W["conv_w_c"][j], W["conv_b_c"][j], W["dt_bias_c"][j],
                            W["a_log_c"][j], W["d_skip_c"][j], W["g_norm_c"][j], s0, cb0, B, T)
            x = mm("w_out_c", j, D, y, res=x)
            xbc = proj.reshape(B, T, -1)[:, -(SSD_CONV - 1):, d_inner:d_inner + conv_dim]
            o_s.append(s); o_cv.append(xbc)

        final_gain = W["norm_final"] if i == depth - 1 else None
        d_ff = W["w_ffn_out"].shape[1]
        if fresh:
            fb0 = jnp.zeros((B, FFN_CONV - 1, d_ff), F32)
            x, fb = _ffn(x, W["norm_ffn"][i], E["w_ffn", i], W["ffn_conv_w"][i], W["ffn_conv_b"][i],
                         fb0, 1, T, final_gain)
        else:
            xt = x.reshape(B, T, D).transpose(1, 0, 2).reshape(T * B, D)
            fb0 = st["ffn"][i].transpose(1, 0, 2).reshape(1, (FFN_CONV - 1) * B, d_ff)
            xt, fb, E["w_ffn", i] = _ffn(xt, W["norm_ffn"][i], (W["w_ffn_in"], W["w_ffn_out"], i),
                                         W["ffn_conv_w"][i], W["ffn_conv_b"][i], fb0, B, T * B, final_gain)
            x = xt.reshape(T, B, D).transpose(1, 0, 2).reshape(B * T, D)
            fb = fb.reshape(FFN_CONV - 1, B, d_ff).transpose(1, 0, 2)
        o_f.append(fb)
    return (x.reshape(B, T, D), jnp.stack(o_c), jnp.stack(o_n), jnp.stack(o_m), jnp.stack(o_k),
            jnp.stack(o_v), jnp.stack(o_s), jnp.stack(o_cv), jnp.stack(o_f))


def kernel(x_prompt, x_sample, state_mlstm_c, state_mlstm_n, state_mlstm_m, cache_swa_k, cache_swa_v, state_ssm, state_ssm_conv, state_ffn_conv, norm_mix, norm_ffn, norm_final, w_in_a, b_ig_a, b_fg_a, g_head_a, w_out_a, w_in_b, sinks_b, w_out_b, w_in_c, conv_w_c, conv_b_c, dt_bias_c, a_log_c, d_skip_c, g_norm_c, w_out_c, w_ffn_in, ffn_conv_w, ffn_conv_b, w_ffn_out):
    W = dict(norm_mix=norm_mix, norm_ffn=norm_ffn, norm_final=norm_final,
             w_in_a=w_in_a, b_ig_a=b_ig_a, b_fg_a=b_fg_a, g_head_a=g_head_a, w_out_a=w_out_a,
             w_in_b=w_in_b, sinks_b=sinks_b, w_out_b=w_out_b,
             w_in_c=w_in_c, conv_w_c=conv_w_c, conv_b_c=conv_b_c, dt_bias_c=dt_bias_c, a_log_c=a_log_c,
             d_skip_c=d_skip_c, g_norm_c=g_norm_c, w_out_c=w_out_c,
             w_ffn_in=w_ffn_in, ffn_conv_w=ffn_conv_w, ffn_conv_b=ffn_conv_b, w_ffn_out=w_ffn_out)
    st = dict(c=state_mlstm_c, n=state_mlstm_n, m=state_mlstm_m, k=cache_swa_k, v=cache_swa_v,
              ssm=state_ssm, conv=state_ssm_conv, ffn=state_ffn_conv)
    E = {}
    for j, wj in enumerate(_main_columns_bf16(w_in_c, w_in_c.shape[2] - w_out_c.shape[1] // SSD_HEAD_DIM)):
        E["w_in_c", j] = wj
    ys, sc, sn, sm, sk, sv, ss, scv, sf = _trunk(x_sample, st, W, E)
    yp, pc, pn, pm, pk, pv, ps, pcv, pf = _trunk(x_prompt, None, W, E)
    return (yp, ys, pc, sc, pn, sn, pm, sm, pk, sk, pv, sv, ps, ss, pcv, scv, pf, sf)
```

```python
import functools
import math

import jax
import jax.numpy as jnp
from jax import lax
from jax.experimental import pallas as pl
from jax.experimental.pallas import tpu as pltpu

F32 = jnp.float32
BF16 = jnp.bfloat16

RMS_EPS = 1e-6
N_MIXERS = 3

MLSTM_HEADS = 8
SWA_HEADS = 32
SWA_KV_HEADS = 4
SWA_GROUP = SWA_HEADS // SWA_KV_HEADS
WINDOW = 128
SSD_HEAD_DIM = 64
SSD_GROUPS = 8
SSD_STATE = 128
SSD_CONV = 4
FFN_CONV = 3

LANES = 128
SUBLANES = 8
VMEM_LIMIT_BYTES = 56 << 20
VMEM_TILE_BUDGET = 44 << 20

FFN_ROW_TILE = 1024
FFN_COL_TILE = 512
MLSTM_CHUNK = 128
SSD_CHUNK = 128


def _cparams(n_axes):
    return pltpu.CompilerParams(dimension_semantics=("arbitrary",) * n_axes,
                                vmem_limit_bytes=VMEM_LIMIT_BYTES)


def _round_up(a, b):
    return (a + b - 1) // b * b


def _largest_divisor(n, candidates):
    for c in candidates:
        if n % c == 0:
            return c
    raise ValueError(f"no tile in {candidates} divides {n}")


def _rmsnorm(x, g):
    return x * lax.rsqrt(jnp.mean(x * x, axis=-1, keepdims=True) + RMS_EPS) * g


def _silu(x):
    return x * jax.nn.sigmoid(x)


def _mm_body(*refs, norm, residual, small, emit):
    it = iter(refs)
    x_ref = next(it)
    g_ref = next(it) if norm else None
    w_ref = next(it)
    w2_ref = next(it) if small else None
    r_ref = next(it) if residual else None
    o_ref = next(it)
    o2_ref = next(it) if small else None
    wb_ref = next(it) if emit else None
    xn_ref = next(it) if norm else None
    first = pl.program_id(1) == 0
    if norm:
        @pl.when(first)
        def _():
            xn_ref[...] = _rmsnorm(x_ref[...], g_ref[...]).astype(BF16)
        lhs = xn_ref[...]
    else:
        lhs = x_ref[...].astype(BF16)
    if small:
        @pl.when(first)
        def _():
            o2_ref[...] = jnp.dot(lhs, w2_ref[...].astype(BF16), preferred_element_type=F32)
    w = w_ref[...]
    if emit:
        w = w.astype(BF16)
        wb_ref[...] = w
    acc = jnp.dot(lhs, w, preferred_element_type=F32)
    if residual:
        acc = acc + r_ref[...]
    o_ref[...] = acc.astype(o_ref.dtype)


def _mm_tiles(m, k, n, ns, x_bytes, w_bytes, norm, residual, emit):
    def need(tm, tn, w_bufs):
        b = 2 * tm * k * x_bytes + w_bufs * k * tn * w_bytes + 2 * tm * tn * 4
        b += tm * k * 2 if norm else 0
        b += 2 * tm * tn * 4 if residual else 0
        b += 3 * k * tn * 2 if emit else 0
        return b + 2 * k * ns * 4 + 2 * tm * ns * 4

    row_tiles = [tm for tm in (1024, 512, 256, 128, 64, 32, 16, 8) if m % tm == 0]
    for tm in row_tiles[:2]:
        if need(tm, n, 1) <= VMEM_TILE_BUDGET:
            return tm, n, 1
    for tm in row_tiles:
        for tn in (2048, 1280, 1024, 896, 512, 256, 128):
            if n % tn == 0 and need(tm, tn, 2) <= VMEM_TILE_BUDGET:
                return tm, tn, 2
    raise ValueError("no matmul tile fits VMEM")


def _w_spec(w, layer, block, index, buffers):
    mode = pl.Buffered(buffers)
    if layer is None:
        return pl.BlockSpec(block, index, pipeline_mode=mode)
    return pl.BlockSpec((None,) + block, lambda *g: (layer,) + index(*g), pipeline_mode=mode)


def _matmul(x, w, layer, n, gain=None, res=None, small=None, emit=False):
    m, k = x.shape
    norm, residual, has_small = gain is not None, res is not None, small is not None
    ns = small.shape[1] if has_small else 0
    tm, tn, w_bufs = _mm_tiles(m, k, n, ns, x.dtype.itemsize, w.dtype.itemsize, norm, residual, emit)
    assert not emit or tm == m
    in_specs = [pl.BlockSpec((tm, k), lambda i, j: (i, 0))]
    args = [x]
    if norm:
        in_specs.append(pl.BlockSpec((1, k), lambda i, j: (0, 0)))
        args.append(gain.reshape(1, k))
    in_specs.append(_w_spec(w, layer, (k, tn), lambda i, j: (0, j), w_bufs))
    args.append(w)
    if has_small:
        in_specs.append(pl.BlockSpec((k, ns), lambda i, j: (0, 0)))
        args.append(small)
    if residual:
        in_specs.append(pl.BlockSpec((tm, tn), lambda i, j: (i, j)))
        args.append(res)
    out_specs = [pl.BlockSpec((tm, tn), lambda i, j: (i, j))]
    out_shape = [jax.ShapeDtypeStruct((m, n), F32)]
    if has_small:
        out_specs.append(pl.BlockSpec((tm, ns), lambda i, j: (i, 0)))
        out_shape.append(jax.ShapeDtypeStruct((m, ns), F32))
    if emit:
        out_specs.append(pl.BlockSpec((k, tn), lambda i, j: (0, j)))
        out_shape.append(jax.ShapeDtypeStruct((k, n), BF16))
    outs = pl.pallas_call(
        functools.partial(_mm_body, norm=norm, residual=residual, small=has_small, emit=emit),
        grid=(m // tm, n // tn),
        in_specs=in_specs,
        out_specs=out_specs,
        out_shape=out_shape,
        scratch_shapes=[pltpu.VMEM((tm, k), BF16)] if norm else [],
        compiler_params=_cparams(2),
        name="matmul",
    )(*args)
    return outs if len(outs) > 1 else outs[0]


def _ffn_body(*refs, tm, shift, halo, tiles_per_seq, nj, final_norm, emit):
    x_ref, gn_ref, wu_ref, wg_ref, cw_ref, cb_ref, wo_ref, buf0_ref, gf_ref, o_ref, bufo_ref = refs[:11]
    wub_ref, wgb_ref, wob_ref = refs[11:14] if emit else (None, None, None)
    xn_ref, gs_ref, carry_ref = refs[-3:]
    i = pl.program_id(0)
    j = pl.program_id(1)

    @pl.when(j == 0)
    def _():
        x = x_ref[...]
        xn_ref[...] = _rmsnorm(x, gn_ref[...]).astype(BF16)
        o_ref[...] = x

    first = (i % tiles_per_seq) == 0
    gs_ref[0:halo, :] = jnp.where(first, buf0_ref[0], carry_ref[j])
    base = halo - (FFN_CONV - 1) * shift
    xn = xn_ref[...]
    wu, wg = wu_ref[...], wg_ref[...]
    if emit:
        wu, wg = wu.astype(BF16), wg.astype(BF16)
        wub_ref[...] = wu
        wgb_ref[...] = wg
    gs_ref[halo:halo + tm, :] = jnp.dot(xn, wg, preferred_element_type=F32)
    u = jnp.dot(xn, wu, preferred_element_type=F32)
    c = gs_ref[base:base + tm, :] * cw_ref[0:1, :]
    for t in range(1, FFN_CONV):
        c = c + gs_ref[base + t * shift:base + t * shift + tm, :] * cw_ref[t:t + 1, :]
    c = c + cb_ref[...]
    h = (_silu(c) * u).astype(BF16)
    tail = gs_ref[tm:tm + halo, :]
    carry_ref[j] = tail
    bufo_ref[0] = tail
    wo = wo_ref[...]
    if emit:
        wo = wo.astype(BF16)
        wob_ref[...] = wo
    o_ref[...] += jnp.dot(h, wo, preferred_element_type=F32)

    if final_norm:
        @pl.when(j == nj - 1)
        def _():
            o_ref[...] = _rmsnorm(o_ref[...], gf_ref[...])


def _ffn(x, gain, weights, conv_w, conv_b, buf0, shift, rows_per_seq, final_gain):
    m, d = x.shape
    emit = weights[0].ndim == 3
    d_ff = weights[1].shape[1] if emit else weights[2].shape[0]
    tm = _largest_divisor(rows_per_seq, (FFN_ROW_TILE, 512, 256, 128, 64, 32, 16, 8))
    tf = _largest_divisor(d_ff, (FFN_COL_TILE, 256, 128))
    nj = d_ff // tf
    if emit:
        w_in, w_out, layer = weights
        assert m == tm
        w_args = [w_in, w_in, w_out]
        w_specs = [pl.BlockSpec((None, d, tf), lambda i, j: (layer, 0, j)),
                   pl.BlockSpec((None, d, tf), lambda i, j: (layer, 0, nj + j)),
                   pl.BlockSpec((None, tf, d), lambda i, j: (layer, j, 0))]
    else:
        w_args = list(weights)
        w_specs = [pl.BlockSpec((d, tf), lambda i, j: (0, j)),
                   pl.BlockSpec((d, tf), lambda i, j: (0, j)),
                   pl.BlockSpec((tf, d), lambda i, j: (j, 0))]
    tiles_per_seq = rows_per_seq // tm
    pre = (FFN_CONV - 1) * shift
    halo = _round_up(pre, SUBLANES)
    assert tm >= halo and tm % SUBLANES == 0
    buf0p = jnp.pad(buf0, ((0, 0), (halo - pre, 0), (0, 0)))
    final_norm = final_gain is not None
    gf = (final_gain if final_norm else gain).reshape(1, d)
    body = functools.partial(_ffn_body, tm=tm, shift=shift, halo=halo, tiles_per_seq=tiles_per_seq,
                             nj=nj, final_norm=final_norm, emit=emit)
    out_specs = [pl.BlockSpec((tm, d), lambda i, j: (i, 0)),
                 pl.BlockSpec((1, halo, tf), lambda i, j: (i, 0, j))]
    out_shape = [jax.ShapeDtypeStruct((m, d), F32),
                 jax.ShapeDtypeStruct((m // tm, halo, d_ff), F32)]
    if emit:
        out_specs += [pl.BlockSpec((d, tf), lambda i, j: (0, j)),
                      pl.BlockSpec((d, tf), lambda i, j: (0, j)),
                      pl.BlockSpec((tf, d), lambda i, j: (j, 0))]
        out_shape += [jax.ShapeDtypeStruct((d, d_ff), BF16), jax.ShapeDtypeStruct((d, d_ff), BF16),
                      jax.ShapeDtypeStruct((d_ff, d), BF16)]
    outs = pl.pallas_call(
        body,
        grid=(m // tm, nj),
        in_specs=[
            pl.BlockSpec((tm, d), lambda i, j: (i, 0)),
            pl.BlockSpec((1, d), lambda i, j: (0, 0)),
            w_specs[0],
            w_specs[1],
            pl.BlockSpec((FFN_CONV, tf), lambda i, j: (0, j)),
            pl.BlockSpec((1, tf), lambda i, j: (0, j)),
            w_specs[2],
            pl.BlockSpec((1, halo, tf), lambda i, j: (i // tiles_per_seq, 0, j)),
            pl.BlockSpec((1, d), lambda i, j: (0, 0)),
        ],
        out_specs=out_specs,
        out_shape=out_shape,
        scratch_shapes=[pltpu.VMEM((tm, d), BF16),
                        pltpu.VMEM((halo + tm, tf), F32),
                        pltpu.VMEM((nj, halo, tf), F32)],
        compiler_params=_cparams(2),
        name="conv_ffn",
    )(x, gain.reshape(1, d), w_args[0], w_args[1], conv_w, conv_b.reshape(1, d_ff), w_args[2], buf0p, gf)
    result = (outs[0], outs[1][tiles_per_seq - 1::tiles_per_seq, halo - pre:, :])
    return result + ((tuple(outs[2:]),) if emit else ())


def _row_from_col(col, eye):
    return jnp.sum(jnp.where(eye, col, 0.0), axis=0, keepdims=True)


def _rowsum_lanes(x):
    parts = jnp.concatenate(_split3_bf16(x), axis=1)
    return jnp.dot(parts, jnp.ones((parts.shape[1], LANES), BF16), preferred_element_type=F32)


def _cumsum_rows(x, tril):
    return jnp.dot(tril, x, preferred_element_type=F32, precision=lax.Precision.HIGHEST)


def _mlstm_body(*refs, chunk, dk, dv, has_init):
    q_ref, k_ref, v_ref, o_ref, gate_ref, bias_ref, gh_ref = refs[:7]
    c0_ref, n0_ref, m0_ref = refs[7:10] if has_init else (None, None, None)
    hg_ref, c_ref, n_ref, m_ref = refs[-4:]
    L = chunk
    H = MLSTM_HEADS

    @pl.when(pl.program_id(1) == 0)
    def _():
        if has_init:
            c_ref[...] = c0_ref[...]
            n_ref[...] = n0_ref[...]
            m_ref[...] = m0_ref[...]
        else:
            c_ref[...] = jnp.zeros_like(c_ref)
            n_ref[...] = jnp.zeros_like(n_ref)
            m_ref[...] = jnp.zeros_like(m_ref)

    ri = lax.broadcasted_iota(jnp.int32, (L, L), 0)
    ci = lax.broadcasted_iota(jnp.int32, (L, L), 1)
    causal = ri >= ci
    eye = ri == ci
    tril = causal.astype(F32)

    gates = gate_ref[...] + bias_ref[...]
    x = gates
    log_sig = jnp.minimum(x, 0.0) - jnp.log1p(jnp.exp(-jnp.abs(x)))
    bcum = _cumsum_rows(log_sig, tril)

    n_all = n_ref[0]
    m_all = m_ref[0]
    stack = lambda parts: jnp.concatenate(parts, axis=0)
    if L % LANES == 0:
        gates_t, bcum_t = gates.T, bcum.T
        row_of = lambda arr_t, col, r: arr_t[r:r + 1, :]
    else:
        gates_t = bcum_t = None
        row_of = lambda arr_t, col, r: _row_from_col(col, eye)
    ig_cols = [gates[:, h:h + 1] for h in range(H)]
    b_cols = [bcum[:, H + h:H + h + 1] for h in range(H)]
    m_prevs = [m_all[h:h + 1, 0:1] for h in range(H)]
    ig_col, b_col = stack(ig_cols), stack(b_cols)
    m_prev = stack([jnp.broadcast_to(mp, (L, 1)) for mp in m_prevs])
    ig_row = stack([jnp.broadcast_to(row_of(gates_t, ig_cols[h], h), (L, L)) for h in range(H)])
    b_row = stack([jnp.broadcast_to(row_of(bcum_t, b_cols[h], H + h), (L, L)) for h in range(H)])
    causal_all = stack([causal] * H)

    log_d = jnp.where(causal_all, b_col - b_row + ig_row, -jnp.inf)
    log_inter = b_col + m_prev
    m_t = jnp.maximum(log_inter, jnp.max(log_d, axis=-1, keepdims=True))
    d = jnp.exp(log_d - m_t)
    inter = jnp.exp(log_inter - m_t)

    qk, qc, qn, kfs, vbs, c_prevs = [], [], [], [], [], []
    for h in range(H):
        qf = q_ref[:, h * dk:(h + 1) * dk] * (dk ** -0.5)
        kf = k_ref[:, h * dk:(h + 1) * dk]
        qb = qf.astype(BF16)
        c_prev = c_ref[0, h]
        qk.append(lax.dot_general(qb, kf.astype(BF16), (((1,), (1,)), ((), ())), preferred_element_type=F32))
        qc.append(jnp.dot(qb, c_prev.astype(BF16), preferred_element_type=F32))
        qn.append(qf * n_all[h:h + 1, :])
        kfs.append(kf)
        vbs.append(v_ref[:, h * dv:(h + 1) * dv].astype(BF16))
        c_prevs.append(c_prev)

    s = stack(qk) * d
    sb = s.astype(BF16)
    sv = stack([jnp.dot(sb[h * L:(h + 1) * L], vbs[h], preferred_element_type=F32) for h in range(H)])
    num = inter * stack(qc) + sv
    if L % LANES == 0:
        rep = lambda col: jnp.broadcast_to(col, (H * L, LANES))
        wide = lambda a: jnp.concatenate([a] * (dv // LANES), axis=1)
        rowsum = _rowsum_lanes
    else:
        rep = wide = lambda a: a
        rowsum = lambda a: jnp.sum(a, axis=-1, keepdims=True)
    den = rep(inter) * rowsum(stack(qn)) + rowsum(s)
    hh = num / wide(jnp.maximum(jnp.abs(den), rep(jnp.exp(-m_t))))
    hn = hh * wide(lax.rsqrt(rowsum(hh * hh) * (1.0 / dv) + RMS_EPS))
    for h in range(H):
        lanes = slice(h * dv, (h + 1) * dv)
        out = jax.nn.sigmoid(o_ref[:, lanes]) * (hn[h * L:(h + 1) * L] * gh_ref[:, lanes])
        hg_ref[:, lanes] = out.astype(hg_ref.dtype)

    b_lasts = [bc[L - 1:L, :] for bc in b_cols]
    log_w = stack([jnp.broadcast_to(bl, (L, 1)) for bl in b_lasts]) - b_col + ig_col
    m_news = [jnp.maximum(b_lasts[h] + m_prevs[h], jnp.max(log_w[h * L:(h + 1) * L], axis=0, keepdims=True))
              for h in range(H)]
    w = jnp.exp(log_w - stack([jnp.broadcast_to(mn, (L, 1)) for mn in m_news]))

    head_row = lax.broadcasted_iota(jnp.int32, (H, LANES), 0)
    n_new, m_new_all = n_all, m_all
    for h in range(H):
        decay = jnp.exp(b_lasts[h] + m_prevs[h] - m_news[h])
        kw = kfs[h] * w[h * L:(h + 1) * L]
        upd = lax.dot_general(kw.astype(BF16), vbs[h], (((0,), (0,)), ((), ())), preferred_element_type=F32)
        c_ref[0, h] = decay * c_prevs[h] + upd
        n_row = decay * n_all[h:h + 1, :] + jnp.sum(kw, axis=0, keepdims=True)
        n_new = jnp.where(head_row == h, n_row, n_new)
        m_new_all = jnp.where(head_row == h, m_news[h], m_new_all)
    n_ref[0] = n_new
    m_ref[0] = m_new_all


def _mlstm_mix(proj, gates, bias, g_head, init, layer, B, T, dk, dv):
    H = MLSTM_HEADS
    L = math.gcd(T, MLSTM_CHUNK)
    nc = T // L
    qw, vw = H * dk, H * dv
    assert qw * 2 == vw and vw % LANES == 0
    hg_dtype = BF16 if L % 16 == 0 else F32
    row = lambda b, c: b * nc + c
    c_spec = pl.BlockSpec((None, 1, H, dk, dv), lambda b, c: (layer, b, 0, 0, 0))
    in_specs = [
        pl.BlockSpec((L, qw), lambda b, c: (row(b, c), 0)),
        pl.BlockSpec((L, qw), lambda b, c: (row(b, c), 1)),
        pl.BlockSpec((L, vw), lambda b, c: (row(b, c), 1)),
        pl.BlockSpec((L, vw), lambda b, c: (row(b, c), 2)),
        pl.BlockSpec((L, LANES), lambda b, c: (row(b, c), 0)),
        pl.BlockSpec((1, LANES), lambda b, c: (0, 0)),
        pl.BlockSpec((1, vw), lambda b, c: (0, 0)),
    ]
    args = [proj, proj, proj, proj, gates, bias, g_head.reshape(1, vw)]
    if init is not None:
        c_all, n0, m0 = init
        in_specs += [c_spec,
                     pl.BlockSpec((1, H, dk), lambda b, c: (b, 0, 0)),
                     pl.BlockSpec((1, H, LANES), lambda b, c: (b, 0, 0))]
        args += [c_all, n0, jnp.broadcast_to(m0[:, :, None], (B, H, LANES))]
    hg, c, n, m = pl.pallas_call(
        functools.partial(_mlstm_body, chunk=L, dk=dk, dv=dv, has_init=init is not None),
        grid=(B, nc),
        in_specs=in_specs,
        out_specs=[
            pl.BlockSpec((L, vw), lambda b, c: (row(b, c), 0)),
            pl.BlockSpec((1, H, dk, dv), lambda b, c: (b, 0, 0, 0)),
            pl.BlockSpec((1, H, dk), lambda b, c: (b, 0, 0)),
            pl.BlockSpec((1, H, LANES), lambda b, c: (b, 0, 0)),
        ],
        out_shape=[jax.ShapeDtypeStruct((B * T, vw), hg_dtype),
                   jax.ShapeDtypeStruct((B, H, dk, dv), F32),
                   jax.ShapeDtypeStruct((B, H, dk), F32),
                   jax.ShapeDtypeStruct((B, H, LANES), F32)],
        compiler_params=_cparams(2),
        name="mlstm",
    )(*args)
    return hg, c, n, m[:, :, 0]


def _swa_body(sink_ref, q_ref, kp_ref, kc_ref, vp_ref, vc_ref, o_ref, *, tq, hd, mask_first):
    W = WINDOW
    G = SWA_GROUP
    S = 2 * W
    kk = jnp.concatenate([kp_ref[...], kc_ref[...]], axis=0)
    vv = jnp.concatenate([vp_ref[...], vc_ref[...]], axis=0)
    qi = lax.broadcasted_iota(jnp.int32, (tq, S), 0)
    kj = lax.broadcasted_iota(jnp.int32, (tq, S), 1)
    dist = W + qi - kj
    valid = (dist >= 0) & (dist <= W)
    if mask_first:
        valid = valid & ((kj >= W) | (pl.program_id(1) > 0))
    distf = dist.astype(F32)
    lane_s = lax.broadcasted_iota(jnp.int32, (S, LANES), 1)
    lane_q = lax.broadcasted_iota(jnp.int32, (tq, LANES), 1)
    low_s = lane_s < hd
    low_q = lane_q < hd
    heads_per_slab = LANES // hd
    assert heads_per_slab == 2

    for kh in range(SWA_KV_HEADS):
        slab = (kh // heads_per_slab) * LANES
        keep_low = (kh % heads_per_slab) == 0
        k128 = kk[:, slab:slab + LANES]
        v128 = vv[:, slab:slab + LANES]
        k_rot = pltpu.roll(k128, hd, axis=1)
        v_rot = pltpu.roll(v128, hd, axis=1)
        own = low_s if keep_low else jnp.logical_not(low_s)
        k2 = jnp.where(own, k128, k_rot).astype(BF16)
        v2 = jnp.where(own, v128, v_rot).astype(BF16)

        q_parts = []
        for g in range(G):
            h = kh * G + g
            qs = q_ref[:, (h // 2) * LANES:(h // 2 + 1) * LANES]
            keep = low_q if h % 2 == 0 else jnp.logical_not(low_q)
            q_parts.append(jnp.where(keep, qs, 0.0))
        q_stack = jnp.concatenate(q_parts, axis=0).astype(BF16)
        s_all = lax.dot_general(q_stack, k2, (((1,), (1,)), ((), ())), preferred_element_type=F32)
        s_all = s_all * (hd ** -0.5)

        p_parts = []
        for g in range(G):
            h = kh * G + g
            slope = 2.0 ** (-8.0 * (h + 1) / SWA_HEADS)
            sink = sink_ref[h]
            s = s_all[g * tq:(g + 1) * tq, :] - slope * distf
            s = jnp.where(valid, s, -jnp.inf)
            mx = jnp.maximum(jnp.max(s, axis=-1, keepdims=True), sink)
            p = jnp.exp(s - mx)
            p = p / (jnp.sum(p, axis=-1, keepdims=True) + jnp.exp(sink - mx))
            p_parts.append(p)
        p_stack = jnp.concatenate(p_parts, axis=0).astype(BF16)
        o_all = jnp.dot(p_stack, v2, preferred_element_type=F32)

        for g2 in range(G // 2):
            h = kh * G + 2 * g2
            o_even = o_all[(2 * g2) * tq:(2 * g2 + 1) * tq, :]
            o_odd = o_all[(2 * g2 + 1) * tq:(2 * g2 + 2) * tq, :]
            o_ref[:, (h // 2) * LANES:(h // 2 + 1) * LANES] = jnp.where(low_q, o_even, o_odd).astype(o_ref.dtype)


def _swa_call(sinks, q_arr, q_blk, k_arr, kp_map, kc_map, v_arr, vp_map, vc_map, grid, tq, n_rows, hd,
              mask_first):
    kvw = SWA_KV_HEADS * hd
    qw = SWA_HEADS * hd
    out_dtype = BF16 if tq % 16 == 0 else F32
    return pl.pallas_call(
        functools.partial(_swa_body, tq=tq, hd=hd, mask_first=mask_first),
        grid=grid,
        in_specs=[
            pl.BlockSpec(memory_space=pltpu.SMEM),
            pl.BlockSpec((tq, qw), q_blk),
            pl.BlockSpec((WINDOW, kvw), kp_map),
            pl.BlockSpec((WINDOW, kvw), kc_map),
            pl.BlockSpec((WINDOW, kvw), vp_map),
            pl.BlockSpec((WINDOW, kvw), vc_map),
        ],
        out_specs=pl.BlockSpec((tq, qw), q_blk),
        out_shape=jax.ShapeDtypeStruct((n_rows, qw), out_dtype),
        compiler_params=_cparams(2),
        name="swa",
    )(sinks, q_arr, k_arr, k_arr, v_arr, v_arr)


def _conv_silu(u_ref, init_ref, carry_ref, gs_ref, cw_ref, cb_ref, first, L, lanes):
    halo = SUBLANES
    gs_ref[0:halo, lanes] = jnp.where(first, init_ref[0, :, lanes], carry_ref[:, lanes])
    gs_ref[halo:halo + L, lanes] = u_ref[:, lanes]
    carry_ref[:, lanes] = gs_ref[L:L + halo, lanes]
    base = halo - (SSD_CONV - 1)
    c = gs_ref[base:base + L, lanes] * cw_ref[0:1, lanes]
    for t in range(1, SSD_CONV):
        c = c + gs_ref[base + t:base + t + L, lanes] * cw_ref[t:t + 1, lanes]
    c = c + cb_ref[:, lanes]
    return _silu(c)


def _split3_bf16(x):
    hi = x.astype(BF16)
    r = x - hi.astype(F32)
    mid = r.astype(BF16)
    lo = (r - mid.astype(F32)).astype(BF16)
    return hi, mid, lo


def _ssd_body(z_ref, xs_ref, bc_ref, dt_ref, dtb_ref, alog_ref, ex_ref, dsk_ref, gn_ref,
              cwx_ref, cwbc_ref, cbx_ref, cbbc_ref, ix_ref, ibc_ref, s0_ref,
              y_ref, s_ref,
              gx_ref, gbc_ref, kx_ref, kbc_ref, *, chunk, groups, hpg, p_dim):
    L = chunk
    N = SSD_STATE
    gw = hpg * p_dim
    first = pl.program_id(1) == 0

    @pl.when(first)
    def _():
        s_ref[...] = s0_ref[...]

    conv_x = functools.partial(_conv_silu, xs_ref, ix_ref, kx_ref, gx_ref, cwx_ref, cbx_ref, first, L)
    conv_bc = functools.partial(_conv_silu, bc_ref, ibc_ref, kbc_ref, gbc_ref, cwbc_ref, cbbc_ref, first, L)

    ri = lax.broadcasted_iota(jnp.int32, (L, L), 0)
    ci = lax.broadcasted_iota(jnp.int32, (L, L), 1)
    causal = ri >= ci
    eye = ri == ci
    tril = causal.astype(F32)

    dt = jax.nn.softplus(dt_ref[...] + dtb_ref[...])
    acum = _cumsum_rows(dt * (-jnp.exp(alog_ref[...])), tril)
    a_last = acum[L - 1:L, :]
    eacum = jnp.exp(acum)
    dtw = dt * jnp.exp(a_last - acum)
    ea_last = jnp.exp(a_last)
    acum_t = acum.T if L % LANES == 0 else None

    per_head = jnp.concatenate([dt, eacum, dtw], axis=0)
    per_head3 = jnp.concatenate(_split3_bf16(per_head), axis=1)

    lane = lax.broadcasted_iota(jnp.int32, (L, LANES), 1)
    low = lane < p_dim
    srow = lax.broadcasted_iota(jnp.int32, (LANES, 1), 0)
    heads_per_slab = LANES // p_dim
    assert heads_per_slab == 2

    gated = []
    for g in range(groups):
        lanes_g = slice(g * gw, (g + 1) * gw)
        ex = jnp.dot(per_head3, ex_ref[:, lanes_g], preferred_element_type=F32)
        dt_e, ea_e, dtw_e = ex[0:L], ex[L:2 * L], ex[2 * L:3 * L]

        xs = conv_x(slice(g * gw, (g + 1) * gw))
        bmb = conv_bc(slice(g * N, (g + 1) * N)).astype(BF16)
        cmb = conv_bc(slice((groups + g) * N, (groups + g + 1) * N)).astype(BF16)
        cb = lax.dot_general(cmb, bmb, (((1,), (1,)), ((), ())), preferred_element_type=F32)
        xdt = (xs * dt_e).astype(BF16)
        xw = (xs * dtw_e).astype(BF16)
        s_prev = s_ref[0, g * gw:(g + 1) * gw, :]
        y_inter = lax.dot_general(cmb, s_prev.astype(BF16), (((1,), (1,)), ((), ())),
                                  preferred_element_type=F32)
        upd = lax.dot_general(xw, bmb, (((0,), (0,)), ((), ())), preferred_element_type=F32)

        y_parts = []
        for pr in range(hpg // heads_per_slab):
            h0 = g * hpg + 2 * pr
            ms = []
            for hh in (h0, h0 + 1):
                a_col = acum[:, hh:hh + 1]
                a_row = acum_t[hh:hh + 1, :] if acum_t is not None else _row_from_col(a_col, eye)
                dec = jnp.exp(jnp.where(causal, a_col - a_row, -jnp.inf))
                ms.append((cb * dec).astype(BF16))
            mcat = jnp.concatenate(ms, axis=1)
            xb = xdt[:, pr * LANES:(pr + 1) * LANES]
            zero = jnp.zeros_like(xb)
            bd = jnp.concatenate([jnp.where(low, xb, zero), jnp.where(low, zero, xb)], axis=0)
            y_parts.append(jnp.dot(mcat, bd, preferred_element_type=F32))

            scale = jnp.where(srow < p_dim, ea_last[:, h0:h0 + 1], ea_last[:, h0 + 1:h0 + 2])
            rows = slice(g * gw + pr * LANES, g * gw + (pr + 1) * LANES)
            s_ref[0, rows, :] = s_prev[pr * LANES:(pr + 1) * LANES] * scale + upd[pr * LANES:(pr + 1) * LANES]

        y = jnp.concatenate(y_parts, axis=1) + y_inter * ea_e
        y = y + dsk_ref[:, g * gw:(g + 1) * gw] * xs
        gated.append(y * _silu(z_ref[:, g * gw:(g + 1) * gw]))

    sq = gated[0] * gated[0]
    for y in gated[1:]:
        sq = sq + y * y
    inv = lax.rsqrt(jnp.sum(sq, axis=-1, keepdims=True) * (1.0 / (groups * gw)) + RMS_EPS)
    for g, y in enumerate(gated):
        y_ref[:, g * gw:(g + 1) * gw] = (y * inv * gn_ref[:, g * gw:(g + 1) * gw]).astype(y_ref.dtype)


def _ssd_mix(proj, dt, d_inner, conv_w, conv_b, dt_bias, a_log, d_skip, g_norm, s0, conv0, B, T):
    G, P, N = SSD_GROUPS, SSD_HEAD_DIM, SSD_STATE
    heads = d_inner // P
    hpg = heads // G
    L = math.gcd(T, SSD_CHUNK)
    nc = T // L
    bcw = 2 * G * N
    dtw = LANES
    conv_dim = d_inner + bcw
    assert d_inner % bcw == 0 and heads <= LANES
    bc_blk = (2 * d_inner) // bcw
    pad_heads = lambda v: jnp.pad(v, (0, LANES - heads)).reshape(1, LANES)
    expand = (jnp.arange(3 * LANES)[:, None] % LANES == jnp.arange(d_inner)[None, :] // P).astype(BF16)
    halo = SUBLANES
    conv0p = jnp.pad(conv0, ((0, 0), (halo - (SSD_CONV - 1), 0), (0, 0)))
    conv_b2 = conv_b.reshape(1, conv_dim)
    dsk = jnp.repeat(d_skip, P).reshape(1, d_inner)
    row = lambda b, c: b * nc + c
    const = lambda b, c: (0, 0)
    y, s = pl.pallas_call(
        functools.partial(_ssd_body, chunk=L, groups=G, hpg=hpg, p_dim=P),
        grid=(B, nc),
        in_specs=[
            pl.BlockSpec((L, d_inner), lambda b, c: (row(b, c), 0)),
            pl.BlockSpec((L, d_inner), lambda b, c: (row(b, c), 1)),
            pl.BlockSpec((L, bcw), lambda b, c: (row(b, c), bc_blk)),
            pl.BlockSpec((L, dtw), lambda b, c: (row(b, c), 0)),
            pl.BlockSpec((1, dtw), const),
            pl.BlockSpec((1, dtw), const),
            pl.BlockSpec((3 * LANES, d_inner), const),
            pl.BlockSpec((1, d_inner), const),
            pl.BlockSpec((1, d_inner), const),
            pl.BlockSpec((SSD_CONV, d_inner), const),
            pl.BlockSpec((SSD_CONV, bcw), lambda b, c: (0, d_inner // bcw)),
            pl.BlockSpec((1, d_inner), const),
            pl.BlockSpec((1, bcw), lambda b, c: (0, d_inner // bcw)),
            pl.BlockSpec((1, halo, d_inner), lambda b, c: (b, 0, 0)),
            pl.BlockSpec((1, halo, bcw), lambda b, c: (b, 0, d_inner // bcw)),
            pl.BlockSpec((1, d_inner, N), lambda b, c: (b, 0, 0)),
        ],
        out_specs=[
            pl.BlockSpec((L, d_inner), lambda b, c: (row(b, c), 0)),
            pl.BlockSpec((1, d_inner, N), lambda b, c: (b, 0, 0)),
        ],
        out_shape=[jax.ShapeDtypeStruct((B * T, d_inner), BF16 if L % 16 == 0 else F32),
                   jax.ShapeDtypeStruct((B, d_inner, N), F32)],
        scratch_shapes=[pltpu.VMEM((halo + L, d_inner), F32), pltpu.VMEM((halo + L, bcw), F32),
                        pltpu.VMEM((halo, d_inner), F32), pltpu.VMEM((halo, bcw), F32)],
        compiler_params=_cparams(2),
        name="ssd",
    )(proj, proj, proj, dt, pad_heads(dt_bias), pad_heads(a_log), expand, dsk, g_norm.reshape(1, d_inner),
      conv_w, conv_w, conv_b2, conv_b2, conv0p, conv0p, s0.reshape(B, d_inner, N))
    return y, s.reshape(B, heads, P, N)


def _gate_weight(w, j):
    gates = 2 * MLSTM_HEADS
    return jnp.pad(w[j, :, -gates:], ((0, 0), (0, LANES - gates)))


def _dt_weight(w, j, d_inner):
    heads = d_inner // SSD_HEAD_DIM
    return jnp.pad(w[j, :, -heads:], ((0, 0), (0, LANES - heads)))


def _main_columns_bf16(w, n):
    return [w[j, :, :n].astype(BF16) for j in range(w.shape[0])]


def _trunk(x3, st, W, E):
    B, T, D = x3.shape
    fresh = st is None

    def mm(name, j, n, xin, **kw):
        if (name, j) in E:
            return _matmul(xin, E[name, j], None, n, **kw)
        outs = _matmul(xin, W[name], j, n, emit=True, **kw)
        E[name, j] = outs[-1]
        return outs[0] if len(outs) == 2 else outs[:-1]

    x = x3.reshape(B * T, D)
    depth = W["norm_mix"].shape[0]
    o_c, o_n, o_m, o_k, o_v, o_s, o_cv, o_f = [], [], [], [], [], [], [], []
    for i in range(depth):
        j = i // N_MIXERS
        kind = i % N_MIXERS
        if kind == 0:
            dv = D // MLSTM_HEADS
            dk = dv // 2
            init = None if fresh else (st["c"], st["n"][j], st["m"][j])
            proj, gates = mm("w_in_a", j, 2 * MLSTM_HEADS * (dk + dv), x, gain=W["norm_mix"][i],
                             small=_gate_weight(W["w_in_a"], j))
            bias = jnp.pad(jnp.concatenate([W["b_ig_a"][j], W["b_fg_a"][j]]), (0, LANES - 2 * MLSTM_HEADS))
            hg, c, n, m = _mlstm_mix(proj, gates, bias.reshape(1, LANES), W["g_head_a"][j], init, j,
                                     B, T, dk, dv)
            x = mm("w_out_a", j, D, hg, res=x)
            o_c.append(c); o_n.append(n); o_m.append(m)
        elif kind == 1:
            hd = D // SWA_HEADS
            kvw = SWA_KV_HEADS * hd
            qw = SWA_HEADS * hd
            proj = mm("w_in_b", j, qw + 2 * kvw, x, gain=W["norm_mix"][i])
            p3 = proj.reshape(B, T, qw + 2 * kvw)
            k_new, v_new = p3[:, :, qw:qw + kvw], p3[:, :, qw + kvw:]
            kb, vb = qw // kvw, qw // kvw + 1
            if fresh:
                nb = T // WINDOW
                prev = lambda b, n: (b * nb + jnp.maximum(n - 1, 0), kb)
                cur = lambda b, n: (b * nb + n, kb)
                prev_v = lambda b, n: (b * nb + jnp.maximum(n - 1, 0), vb)
                cur_v = lambda b, n: (b * nb + n, vb)
                o = _swa_call(W["sinks_b"][j], proj, lambda b, n: (b * nb + n, 0), proj, prev, cur,
                              proj, prev_v, cur_v, (B, nb), WINDOW, B * T, hd, True)
                kc, vc = k_new[:, -WINDOW:], v_new[:, -WINDOW:]
            else:
                kbuf = st["k"][j].reshape(B, WINDOW, kvw)
                vbuf = st["v"][j].reshape(B, WINDOW, kvw)
                zpad = jnp.zeros((B, WINDOW - T, kvw), F32)
                kk = jnp.concatenate([kbuf, k_new, zpad], axis=1).reshape(B * 2 * WINDOW, kvw)
                vv = jnp.concatenate([vbuf, v_new, zpad], axis=1).reshape(B * 2 * WINDOW, kvw)
                o = _swa_call(W["sinks_b"][j], proj, lambda b, n: (b, 0), kk, lambda b, n: (2 * b, 0),
                              lambda b, n: (2 * b + 1, 0), vv, lambda b, n: (2 * b, 0),
                              lambda b, n: (2 * b + 1, 0), (B, 1), T, B * T, hd, False)
                kc = jnp.concatenate([kbuf[:, T:], k_new], axis=1)
                vc = jnp.concatenate([vbuf[:, T:], v_new], axis=1)
            x = mm("w_out_b", j, D, o, res=x)
            o_k.append(kc.reshape(B, WINDOW, SWA_KV_HEADS, hd))
            o_v.append(vc.reshape(B, WINDOW, SWA_KV_HEADS, hd))
        else:
            d_inner = W["w_out_c"].shape[1]
            heads = d_inner // SSD_HEAD_DIM
            conv_dim = d_inner + 2 * SSD_GROUPS * SSD_STATE
            if fresh:
                s0 = jnp.zeros((B, heads, SSD_HEAD_DIM, SSD_STATE), F32)
                cb0 = jnp.zeros((B, SSD_CONV - 1, conv_dim), F32)
            else:
                s0, cb0 = st["ssm"][j], st["conv"][j]
            proj, dt = mm("w_in_c", j, d_inner + conv_dim, x, gain=W["norm_mix"][i],
                          small=_dt_weight(W["w_in_c"], j, d_inner))
            y, s = _ssd_mix(proj, dt, d_inner, W["conv_w_c"][j], W["conv_b_c"][j], W["dt_bias_c"][j],
                            W["a_log_c"][j], W["d_skip_c"][j], W["g_norm_c"][j], s0, cb0, B, T)
            x = mm("w_out_c", j, D, y, res=x)
            xbc = proj.reshape(B, T, -1)[:, -(SSD_CONV - 1):, d_inner:d_inner + conv_dim]
            o_s.append(s); o_cv.append(xbc)

        final_gain = W["norm_final"] if i == depth - 1 else None
        d_ff = W["w_ffn_out"].shape[1]
        if fresh:
            fb0 = jnp.zeros((B, FFN_CONV - 1, d_ff), F32)
            x, fb = _ffn(x, W["norm_ffn"][i], E["w_ffn", i], W["ffn_conv_w"][i], W["ffn_conv_b"][i],
                         fb0, 1, T, final_gain)
        else:
            xt = x.reshape(B, T, D).transpose(1, 0, 2).reshape(T * B, D)
            fb0 = st["ffn"][i].transpose(1, 0, 2).reshape(1, (FFN_CONV - 1) * B, d_ff)
            xt, fb, E["w_ffn", i] = _ffn(xt, W["norm_ffn"][i], (W["w_ffn_in"], W["w_ffn_out"], i),
                                         W["ffn_conv_w"][i], W["ffn_conv_b"][i], fb0, B, T * B, final_gain)
            x = xt.reshape(T, B, D).transpose(1, 0, 2).reshape(B * T, D)
            fb = fb.reshape(FFN_CONV - 1, B, d_ff).transpose(1, 0, 2)
        o_f.append(fb)
    return (x.reshape(B, T, D), jnp.stack(o_c), jnp.stack(o_n), jnp.stack(o_m), jnp.stack(o_k),
            jnp.stack(o_v), jnp.stack(o_s), jnp.stack(o_cv), jnp.stack(o_f))


def kernel(x_prompt, x_sample, state_mlstm_c, state_mlstm_n, state_mlstm_m, cache_swa_k, cache_swa_v, state_ssm, state_ssm_conv, state_ffn_conv, norm_mix, norm_ffn, norm_final, w_in_a, b_ig_a, b_fg_a, g_head_a, w_out_a, w_in_b, sinks_b, w_out_b, w_in_c, conv_w_c, conv_b_c, dt_bias_c, a_log_c, d_skip_c, g_norm_c, w_out_c, w_ffn_in, ffn_conv_w, ffn_conv_b, w_ffn_out):
    W = dict(norm_mix=norm_mix, norm_ffn=norm_ffn, norm_final=norm_final,
             w_in_a=w_in_a, b_ig_a=b_ig_a, b_fg_a=b_fg_a, g_head_a=g_head_a, w_out_a=w_out_a,
             w_in_b=w_in_b, sinks_b=sinks_b, w_out_b=w_out_b,
             w_in_c=w_in_c, conv_w_c=conv_w_c, conv_b_c=conv_b_c, dt_bias_c=dt_bias_c, a_log_c=a_log_c,
             d_skip_c=d_skip_c, g_norm_c=g_norm_c, w_out_c=w_out_c,
             w_ffn_in=w_ffn_in, ffn_conv_w=ffn_conv_w, ffn_conv_b=ffn_conv_b, w_ffn_out=w_ffn_out)
    st = dict(c=state_mlstm_c, n=state_mlstm_n, m=state_mlstm_m, k=cache_swa_k, v=cache_swa_v,
              ssm=state_ssm, conv=state_ssm_conv, ffn=state_ffn_conv)
    E = {}
    for j, wj in enumerate(_main_columns_bf16(w_in_c, w_in_c.shape[2] - w_out_c.shape[1] // SSD_HEAD_DIM)):
        E["w_in_c", j] = wj
    ys, sc, sn, sm, sk, sv, ss, scv, sf = _trunk(x_sample, st, W, E)
    yp, pc, pn, pm, pk, pv, ps, pcv, pf = _trunk(x_prompt, None, W, E)
    return (yp, ys, pc, sc, pn, sn, pm, sm, pk, sk, pv, sv, ps, ss, pcv, scv, pf, sf)
```
